```python
import math
import jax, jax.numpy as jnp
from jax import lax
import numpy as np

D_MODEL = 2048
BATCH = 2
SEQ = 16384
DEPTH = 2

CTX_LEN = 256
GRID_W = 64
NORM_EPS = 1e-6
N_BRANCH = 4
BRANCH_WIDTH = D_MODEL // 4
HEAD_DIM = 128

NA_HEAD_DIM = HEAD_DIM
NA_HEADS = BRANCH_WIDTH // NA_HEAD_DIM
NA_WIDTH = BRANCH_WIDTH
NA_WIN_ROWS = 8
NA_WIN_COLS = 16

ML_HEAD_DIM = HEAD_DIM
ML_HEADS = BRANCH_WIDTH // ML_HEAD_DIM
ML_WIDTH = BRANCH_WIDTH
ML_CHUNK = 128
ML_CONV = 4
ROPE_BASE = 10000.0

LRU_WIDTH = BRANCH_WIDTH
LRU_BLOCKS = 4
LRU_BW = LRU_WIDTH // LRU_BLOCKS
LRU_C = 8.0
LRU_CONV = 4

HY_WIDTH = BRANCH_WIDTH
HY_GROUPS = 4
HY_ORDER = 2
HY_SHORT = 3
HY_EMB = 33
HY_FFN = 64
HY_N_FILT = HY_ORDER * 2 * HY_WIDTH

N_EXPERTS = 128
TOP_K = 8
N_GROUPS = 8
TOPK_GROUPS = 4
EXPERT_HIDDEN = 256
SHARED_HIDDEN = 256
ROUTE_SCALE = 2.5
MOE_BLOCK = 128

IN_SPLITS = (3 * NA_WIDTH, 3 * ML_WIDTH, ML_WIDTH, 4 * ML_HEADS, LRU_WIDTH, LRU_WIDTH, (HY_ORDER + 1) * HY_WIDTH, N_BRANCH * D_MODEL)
IN_WIDTH = sum(IN_SPLITS)

kernel_name = 'hybrid_na_mlstm_rglru_hyena_moe_dit'


def rms_norm(x, g):
    xf = x.astype(jnp.float32)
    y = xf * lax.rsqrt(jnp.mean(xf * xf, axis=-1, keepdims=True) + NORM_EPS)
    return (y * g.astype(jnp.float32)).astype(x.dtype)


def centred_dwconv(x, w, b):
    K, L = w.shape[0], x.shape[1]
    pl = (K - 1) // 2
    xp = jnp.pad(x, ((0, 0), (pl, K - 1 - pl), (0, 0)))
    y = b
    for j in range(K):
        y = y + xp[:, j:j + L] * w[j]
    return y


def to_heads(a, n_heads):
    B, L, W = a.shape
    return a.reshape(B, L, n_heads, W // n_heads).transpose(0, 2, 1, 3)


def from_heads(a):
    B, H, L, Dh = a.shape
    return a.transpose(0, 2, 1, 3).reshape(B, L, H * Dh)


def rope_2d(x, rows, cols):
    dh = x.shape[-1]
    half, quarter = dh // 2, dh // 4
    inv = ROPE_BASE ** (-jnp.arange(quarter, dtype=jnp.float32) / quarter)
    def rot(xa, pos):
        ang = pos.astype(jnp.float32)[:, None] * inv
        cos, sin = jnp.cos(ang).astype(xa.dtype), jnp.sin(ang).astype(xa.dtype)
        x1, x2 = xa[..., :quarter], xa[..., quarter:]
        return jnp.concatenate([x1 * cos - x2 * sin, x1 * sin + x2 * cos], axis=-1)
    return jnp.concatenate([rot(x[..., :half], rows), rot(x[..., half:], cols)], axis=-1)


def ctx_attention(q, k, v):
    s = jnp.einsum('bhqd,bhkd->bhqk', q, k).astype(jnp.float32)
    p = jax.nn.softmax(s, axis=-1).astype(v.dtype)
    return jnp.einsum('bhqk,bhkd->bhqd', p, v)


def neighbourhood_attention(q, k, v, kc, vc, rpb):
    B, H, S, Dh = q.shape
    rows = S // GRID_W
    kr = min(NA_WIN_ROWS, rows)
    qg = q.reshape(B, H, rows, GRID_W, Dh)
    kg = k.reshape(B, H, rows, GRID_W, Dh)
    vg = v.reshape(B, H, rows, GRID_W, Dh)
    col = np.arange(GRID_W)
    col_idx = np.clip(col - NA_WIN_COLS // 2, 0, GRID_W - NA_WIN_COLS)[:, None] + np.arange(NA_WIN_COLS)
    rpb_c = rpb[:, :, col_idx - col[:, None] + NA_WIN_COLS - 1]
    n_lat = kr * NA_WIN_COLS
    def row_block(r):
        r0 = jnp.clip(r - kr // 2, 0, rows - kr)
        qr = lax.dynamic_index_in_dim(qg, r, axis=2, keepdims=False)
        kw = lax.dynamic_slice_in_dim(kg, r0, kr, axis=2)[:, :, :, col_idx]
        vw = lax.dynamic_slice_in_dim(vg, r0, kr, axis=2)[:, :, :, col_idx]
        bias = jnp.take(rpb_c, r0 + jnp.arange(kr) - r + NA_WIN_ROWS - 1, axis=1)
        s_lat = jnp.einsum('bhqd,bhrqjd->bhqrj', qr, kw).astype(jnp.float32) + bias.transpose(0, 2, 1, 3)
        s_ctx = jnp.einsum('bhqd,bhkd->bhqk', qr, kc).astype(jnp.float32)
        p = jax.nn.softmax(jnp.concatenate([s_lat.reshape(B, H, GRID_W, n_lat), s_ctx], axis=-1), axis=-1).astype(v.dtype)
        p_lat = p[..., :n_lat].reshape(B, H, GRID_W, kr, NA_WIN_COLS)
        return jnp.einsum('bhqrj,bhrqjd->bhqd', p_lat, vw) + jnp.einsum('bhqk,bhkd->bhqd', p[..., n_lat:], vc)
    out = lax.map(row_block, jnp.arange(rows))
    return out.transpose(1, 2, 0, 3, 4).reshape(B, H, S, Dh)


def na_branch(pc, pl, qg, kg, rpb, need_ctx):
    def qkv(p):
        q, k, v = jnp.split(p, 3, axis=-1)
        q = rms_norm(to_heads(q, NA_HEADS), qg) * (NA_HEAD_DIM ** -0.5)
        k = rms_norm(to_heads(k, NA_HEADS), kg)
        return q, k, to_heads(v, NA_HEADS)
    qc, kc, vc = qkv(pc)
    ql, kl, vl = qkv(pl)
    yl = from_heads(neighbourhood_attention(ql, kl, vl, kc, vc, rpb))
    yc = from_heads(ctx_attention(qc, kc, vc)) if need_ctx else None
    return yc, yl


def mlstm_scan(q, k, v, ig, fg, state):
    B, H, L, Dh = q.shape
    nc = L // ML_CHUNK
    def chunks(a):
        return jnp.moveaxis(a.reshape(B, H, nc, ML_CHUNK, *a.shape[3:]), 2, 0)
    lower = jnp.tril(jnp.ones((ML_CHUNK, ML_CHUNK), dtype=bool))
    def step(carry, xs):
        C, n, m = carry
        qc, kc, vc, ic, fc = xs
        b = jnp.cumsum(jax.nn.log_sigmoid(fc), axis=-1)
        d = jnp.where(lower, b[..., :, None] - b[..., None, :] + ic[..., None, :], -jnp.inf)
        m_t = jnp.maximum(b + m[..., None], jnp.max(d, axis=-1))
        w = jnp.exp(d - m_t[..., None]) * jnp.einsum('bhtd,bhsd->bhts', qc, kc)
        carry_w = jnp.exp(b + m[..., None] - m_t)
        num = jnp.einsum('bhts,bhsd->bhtd', w, vc) + carry_w[..., None] * jnp.einsum('bhtd,bhde->bhte', qc, C)
        den = jnp.sum(w, axis=-1) + carry_w * jnp.einsum('bhtd,bhd->bht', qc, n)
        h = num / jnp.maximum(jnp.abs(den), jnp.exp(-m_t))[..., None]
        g = b[..., -1:] - b + ic
        m_new = jnp.maximum(b[..., -1] + m, jnp.max(g, axis=-1))
        keep = jnp.exp(b[..., -1] + m - m_new)
        wg = jnp.exp(g - m_new[..., None])
        C = keep[..., None, None] * C + jnp.einsum('bhs,bhsd,bhse->bhde', wg, kc, vc)
        n = keep[..., None] * n + jnp.einsum('bhs,bhsd->bhd', wg, kc)
        return (C, n, m_new), h
    state, hs = lax.scan(step, state, (chunks(q), chunks(k), chunks(v), chunks(ig), chunks(fg)))
    return jnp.moveaxis(hs, 0, 2).reshape(B, H, L, Dh), state


def mlstm_branch(pc_qkv, pl_qkv, pc_o, pl_o, pc_if, pl_if, conv_w, conv_b, gate_b, out_g, rows, cols, need_ctx):
    def prep(p_qkv, p_if, rotary):
        q, k, v = jnp.split(p_qkv, 3, axis=-1)
        qk = jax.nn.silu(centred_dwconv(jnp.concatenate([q, k], axis=-1), conv_w, conv_b))
        q, k = jnp.split(qk, 2, axis=-1)
        q = to_heads(q, ML_HEADS).astype(jnp.float32)
        k = to_heads(k, ML_HEADS).astype(jnp.float32)
        v = to_heads(v, ML_HEADS).astype(jnp.float32)
        if rotary:
            q, k = rope_2d(q, rows, cols), rope_2d(k, rows, cols)
        k = k * (ML_HEAD_DIM ** -0.5)
        B, L, _ = p_if.shape
        g = p_if.astype(jnp.float32).reshape(B, L, 2, 2, ML_HEADS) + gate_b
        return q, k, v, g.transpose(2, 3, 0, 4, 1)
    qc, kc, vc, gc = prep(pc_qkv, pc_if, False)
    ql, kl, vl, gl = prep(pl_qkv, pl_if, True)
    B = ql.shape[0]
    zero = (jnp.zeros((B, ML_HEADS, ML_HEAD_DIM, ML_HEAD_DIM), jnp.float32),
            jnp.zeros((B, ML_HEADS, ML_HEAD_DIM), jnp.float32), jnp.zeros((B, ML_HEADS), jnp.float32))
    flip = lambda a: jnp.flip(a, axis=2)
    hc_f, st_f = mlstm_scan(qc, kc, vc, gc[0, 0], gc[0, 1], zero)
    hl_f, _ = mlstm_scan(ql, kl, vl, gl[0, 0], gl[0, 1], st_f)
    hc_b, st_b = mlstm_scan(flip(qc), flip(kc), flip(vc), flip(gc[1, 0]), flip(gc[1, 1]), zero)
    hl_b, _ = mlstm_scan(flip(ql), flip(kl), flip(vl), flip(gl[1, 0]), flip(gl[1, 1]), st_b)
    def readout(h, o):
        h = from_heads(rms_norm(h, out_g.reshape(ML_HEADS, 1, ML_HEAD_DIM)))
        return (h * jax.nn.sigmoid(o.astype(jnp.float32))).astype(o.dtype)
    yl = readout(hl_f + flip(hl_b), pl_o)
    yc = readout(hc_f + flip(hc_b), pc_o) if need_ctx else None
    return yc, yl


def rglru_scan(x, wa, ba, wx, bx, lam, h0):
    B, L, W = x.shape
    xb = x.reshape(B, L, LRU_BLOCKS, LRU_BW)
    r = jax.nn.sigmoid(jnp.einsum('blnd,nde->blne', xb, wa).reshape(B, L, W) + ba)
    i = jax.nn.sigmoid(jnp.einsum('blnd,nde->blne', xb, wx).reshape(B, L, W) + bx)
    log_a = -LRU_C * r * jax.nn.softplus(-lam)
    u = jnp.sqrt(-jnp.expm1(2.0 * log_a)) * (i * x)
    def combine(e1, e2):
        return e1[0] * e2[0], e2[0] * e1[1] + e2[1]
    a_cum, h = lax.associative_scan(combine, (jnp.exp(log_a), u), axis=1)
    h = h + a_cum * h0[:, None, :]
    return h, h[:, -1]


def lru_branch(pc_x, pl_x, pc_g, pl_g, conv_w, conv_b, wa, ba, wx, bx, lam, need_ctx):
    xc = centred_dwconv(pc_x, conv_w, conv_b).astype(jnp.float32)
    xl = centred_dwconv(pl_x, conv_w, conv_b).astype(jnp.float32)
    h0 = jnp.zeros((xl.shape[0], LRU_WIDTH), jnp.float32)
    flip = lambda a: jnp.flip(a, axis=1)
    hc_f, s_f = rglru_scan(xc, wa[0], ba[0], wx[0], bx[0], lam[0], h0)
    hl_f, _ = rglru_scan(xl, wa[0], ba[0], wx[0], bx[0], lam[0], s_f)
    hc_b, s_b = rglru_scan(flip(xc), wa[1], ba[1], wx[1], bx[1], lam[1], h0)
    hl_b, _ = rglru_scan(flip(xl), wa[1], ba[1], wx[1], bx[1], lam[1], s_b)
    yl = ((hl_f + flip(hl_b)) * jax.nn.gelu(pl_g.astype(jnp.float32))).astype(pl_g.dtype)
    yc = ((hc_f + flip(hc_b)) * jax.nn.gelu(pc_g.astype(jnp.float32))).astype(pc_g.dtype) if need_ctx else None
    return yc, yl


def hyena_filters(L, w1, b1, w2, b2, w3, freq, decay):
    n = jnp.arange(L, dtype=jnp.float32)
    t = n / max(L - 1, 1)
    bands = (HY_EMB - 1) // 2
    f = jnp.linspace(1e-4, bands - 1, bands, dtype=jnp.float32)
    ang = (2.0 * math.pi / L) * n[:, None] * f
    z = jnp.concatenate([t[:, None], jnp.cos(ang), -jnp.sin(ang)], axis=-1)
    a = jnp.sin(freq * (z @ w1 + b1))
    a = jnp.sin(freq * (a @ w2 + b2))
    h = ((a @ w3) * jnp.exp(-t[:, None] * jnp.abs(decay))).astype(jnp.float32)
    h = h.reshape(L, HY_ORDER, 2, HY_WIDTH)
    l1 = jnp.sum(jnp.abs(h[:, :, 0]), axis=0) + jnp.sum(jnp.abs(h[1:, :, 1]), axis=0)
    return h / l1[None, :, None]


def long_conv(u, hf, hb, d):
    B, L, W = u.shape
    taps = jnp.concatenate([hf, jnp.zeros((1, W), hf.dtype), hb[:0:-1]], axis=0)
    y = jnp.fft.irfft(jnp.fft.rfft(u, n=2 * L, axis=1) * jnp.fft.rfft(taps, axis=0)[None], n=2 * L, axis=1)[:, :L]
    return y + u * d


def hyena_branch(pc, pl, conv_w, conv_b, w1, b1, w2, b2, w3, freq, decay, dskip, need_ctx):
    def run(p):
        L = p.shape[1]
        u = centred_dwconv(p, conv_w, conv_b).astype(jnp.float32)
        parts = jnp.split(u, HY_ORDER + 1, axis=-1)
        filt = hyena_filters(L, w1, b1, w2, b2, w3, freq, decay)
        z = parts[0]
        for o in range(HY_ORDER):
            z = parts[o + 1] * long_conv(z, filt[:, o, 0], filt[:, o, 1], dskip[o])
        return z.astype(p.dtype)
    yl = run(pl)
    yc = run(pc) if need_ctx else None
    return yc, yl


def merge_branches(ys, gates, br_w, br_gate_b, w_o):
    B, L, GW = gates.shape
    g = jax.nn.sigmoid(gates.reshape(B, L, N_BRANCH, GW // N_BRANCH) + br_gate_b)
    z = sum(g[:, :, i] * (y @ br_w[i]) for i, y in enumerate(ys))
    return z @ w_o


def moe_ffn(h, router_w, router_b, w_gate, w_up, w_down, sw_gate, sw_up, sw_down):
    T, D = h.shape
    E = router_w.shape[-1]
    scores = jax.nn.sigmoid((h @ router_w).astype(jnp.float32))
    biased = scores + router_b
    grp_score = jnp.sum(lax.top_k(biased.reshape(T, N_GROUPS, E // N_GROUPS), 2)[0], axis=-1)
    _, grp_idx = lax.top_k(grp_score, TOPK_GROUPS)
    grp_mask = jnp.sum(jax.nn.one_hot(grp_idx, N_GROUPS), axis=1) > 0
    masked = jnp.where(jnp.repeat(grp_mask, E // N_GROUPS, axis=1), biased, -jnp.inf)
    _, idx = lax.top_k(masked, TOP_K)
    wts = jnp.take_along_axis(scores, idx, axis=-1)
    wts = wts / jnp.sum(wts, axis=-1, keepdims=True) * ROUTE_SCALE
    n_assign = T * TOP_K
    flat_e = idx.reshape(-1)
    order = jnp.argsort(flat_e)
    se = flat_e[order]
    stok = (order // TOP_K).astype(jnp.int32)
    sw = wts.reshape(-1)[order]
    counts = jnp.bincount(flat_e, length=E)
    padded = (counts + MOE_BLOCK - 1) // MOE_BLOCK * MOE_BLOCK
    pend = jnp.cumsum(padded)
    cstart = jnp.cumsum(counts) - counts
    dest = (pend - padded)[se] + jnp.arange(n_assign) - cstart[se]
    n_blocks = -(-n_assign // MOE_BLOCK) + E
    cap = n_blocks * MOE_BLOCK
    slot_tok = jnp.full((cap,), T, jnp.int32).at[dest].set(stok)
    slot_w = jnp.zeros((cap,), jnp.float32).at[dest].set(sw)
    block_exp = jnp.minimum(jnp.searchsorted(pend, jnp.arange(n_blocks) * MOE_BLOCK, side='right'), E - 1)
    h_pad = jnp.concatenate([h, jnp.zeros((1, D), h.dtype)], axis=0)
    def block_step(acc, xs):
        tok, wt, e = xs
        xb = h_pad[tok]
        a = jax.nn.silu(xb @ w_gate[e]) * (xb @ w_up[e])
        return acc.at[tok].add((a @ w_down[e]) * wt[:, None].astype(h.dtype)), None
    acc, _ = lax.scan(block_step, jnp.zeros((T + 1, D), h.dtype),
                      (slot_tok.reshape(n_blocks, MOE_BLOCK), slot_w.reshape(n_blocks, MOE_BLOCK), block_exp))
    shared = (jax.nn.silu(h @ sw_gate) * (h @ sw_up)) @ sw_down
    return acc[:T] + shared


def setup_inputs(seed: int = 0) -> dict:
    key = jax.random.key(seed)
    keys = iter(jax.random.split(key, 64))
    def nrm(shape, scale=1.0):
        return jax.random.normal(next(keys), shape, jnp.float32) * scale
    L, D = DEPTH, D_MODEL
    E, HE, HS = N_EXPERTS, EXPERT_HIDDEN, SHARED_HIDDEN
    a_lru = jax.random.uniform(next(keys), (L, 2, LRU_WIDTH), jnp.float32, 0.9, 0.999) ** (1.0 / LRU_C)
    decay_lo, decay_hi = abs(math.log(1e-2)) / 1.5, abs(math.log(1e-2)) / 0.3
    gw = HY_WIDTH // HY_GROUPS
    decay_base = jnp.tile(jnp.linspace(decay_lo, decay_hi, gw, dtype=jnp.float32), HY_N_FILT // gw)
    return {
        'x': nrm((BATCH, SEQ, D)),
        'c': nrm((BATCH, D)),
        'ctx': nrm((BATCH, CTX_LEN, D)),
        'c_ctx': nrm((D,)),
        'norm1_g': 1.0 + nrm((L, D), 0.05),
        'norm2_g': 1.0 + nrm((L, D), 0.05),
        'ada_w': nrm((L, D, 6 * D), 0.5 * D ** -0.5),
        'ada_b': nrm((L, 6 * D), 0.02),
        'w_in': nrm((L, D, IN_WIDTH), D ** -0.5),
        'na_qnorm_g': 1.0 + nrm((L, NA_HEAD_DIM), 0.05),
        'na_knorm_g': 1.0 + nrm((L, NA_HEAD_DIM), 0.05),
        'na_rpb': nrm((L, NA_HEADS, 2 * NA_WIN_ROWS - 1, 2 * NA_WIN_COLS - 1), 0.05),
        'ml_conv_w': nrm((L, ML_CONV, 2 * ML_WIDTH), ML_CONV ** -0.5),
        'ml_conv_b': nrm((L, 2 * ML_WIDTH), 0.02),
        'ml_gate_b': jnp.stack([nrm((L, 2, ML_HEADS), 0.1),
                                jnp.linspace(3.0, 6.0, ML_HEADS, dtype=jnp.float32) + nrm((L, 2, ML_HEADS), 0.1)], axis=2),
        'ml_out_g': 1.0 + nrm((L, ML_WIDTH), 0.05),
        'lru_conv_w': nrm((L, LRU_CONV, LRU_WIDTH), LRU_CONV ** -0.5),
        'lru_conv_b': nrm((L, LRU_WIDTH), 0.02),
        'lru_wa': nrm((L, 2, LRU_BLOCKS, LRU_BW, LRU_BW), LRU_BW ** -0.5),
        'lru_ba': nrm((L, 2, LRU_WIDTH), 0.02),
        'lru_wx': nrm((L, 2, LRU_BLOCKS, LRU_BW, LRU_BW), LRU_BW ** -0.5),
        'lru_bx': nrm((L, 2, LRU_WIDTH), 0.02),
        'lru_lambda': jnp.log(a_lru) - jnp.log1p(-a_lru),
        'hy_conv_w': nrm((L, HY_SHORT, (HY_ORDER + 1) * HY_WIDTH), HY_SHORT ** -0.5),
        'hy_conv_b': nrm((L, (HY_ORDER + 1) * HY_WIDTH), 0.02),
        'hy_w1': nrm((L, HY_EMB, HY_FFN), HY_EMB ** -0.5),
        'hy_b1': nrm((L, HY_FFN), 0.02),
        'hy_w2': nrm((L, HY_FFN, HY_FFN), HY_FFN ** -0.5),
        'hy_b2': nrm((L, HY_FFN), 0.02),
        'hy_w3': nrm((L, HY_FFN, HY_N_FILT), HY_FFN ** -0.5),
        'hy_freq': 1.0 + nrm((L, HY_FFN), 0.05),
        'hy_decay': decay_base * (1.0 + nrm((L, HY_N_FILT), 0.05)),
        'hy_dskip': nrm((L, HY_ORDER, HY_WIDTH), 0.5),
        'br_w': nrm((L, N_BRANCH, BRANCH_WIDTH, D), BRANCH_WIDTH ** -0.5),
        'br_gate_b': nrm((L, N_BRANCH, D), 0.02),
        'w_o': nrm((L, D, D), D ** -0.5),
        'router_w': nrm((L, D, E), D ** -0.5),
        'router_b': nrm((L, E), 0.01),
        'exp_w_gate': nrm((L, E, D, HE), D ** -0.5),
        'exp_w_up': nrm((L, E, D, HE), D ** -0.5),
        'exp_w_down': nrm((L, E, HE, D), HE ** -0.5),
        'sh_w_gate': nrm((L, D, HS), D ** -0.5),
        'sh_w_up': nrm((L, D, HS), D ** -0.5),
        'sh_w_down': nrm((L, HS, D), HS ** -0.5),
    }


def reference(x, c, ctx, c_ctx, norm1_g, norm2_g, ada_w, ada_b, w_in, na_qnorm_g, na_knorm_g, na_rpb,
              ml_conv_w, ml_conv_b, ml_gate_b, ml_out_g, lru_conv_w, lru_conv_b, lru_wa, lru_ba, lru_wx, lru_bx,
              lru_lambda, hy_conv_w, hy_conv_b, hy_w1, hy_b1, hy_w2, hy_b2, hy_w3, hy_freq, hy_decay, hy_dskip,
              br_w, br_gate_b, w_o, router_w, router_b, exp_w_gate, exp_w_up, exp_w_down,
              sh_w_gate, sh_w_up, sh_w_down):
    dt = x.dtype
    B, S, D = x.shape
    pos = jnp.arange(S)
    rows, cols = pos // GRID_W, pos % GRID_W
    split_at = np.cumsum(IN_SPLITS)[:-1].tolist()
    xl, xc = x, ctx
    for l in range(DEPTH):
        need_ctx = l < DEPTH - 1
        mod_l = jax.nn.silu(c) @ ada_w[l] + ada_b[l]
        mod_c = jax.nn.silu(c_ctx) @ ada_w[l] + ada_b[l]
        sh1_l, sc1_l, g1_l, sh2_l, sc2_l, g2_l = jnp.split(mod_l[:, None, :], 6, axis=-1)
        sh1_c, sc1_c, g1_c, sh2_c, sc2_c, g2_c = jnp.split(mod_c, 6, axis=-1)
        hl = rms_norm(xl, norm1_g[l]) * (1 + sc1_l) + sh1_l
        hc = rms_norm(xc, norm1_g[l]) * (1 + sc1_c) + sh1_c
        pl = jnp.split(hl @ w_in[l], split_at, axis=-1)
        pc = jnp.split(hc @ w_in[l], split_at, axis=-1)
        ya_c, ya_l = na_branch(pc[0], pl[0], na_qnorm_g[l], na_knorm_g[l], na_rpb[l], need_ctx)
        yb_c, yb_l = mlstm_branch(pc[1], pl[1], pc[2], pl[2], pc[3], pl[3], ml_conv_w[l], ml_conv_b[l],
                                  ml_gate_b[l], ml_out_g[l], rows, cols, need_ctx)
        yc_c, yc_l = lru_branch(pc[4], pl[4], pc[5], pl[5], lru_conv_w[l], lru_conv_b[l], lru_wa[l], lru_ba[l],
                                lru_wx[l], lru_bx[l], lru_lambda[l], need_ctx)
        yd_c, yd_l = hyena_branch(pc[6], pl[6], hy_conv_w[l], hy_conv_b[l], hy_w1[l], hy_b1[l], hy_w2[l], hy_b2[l],
                                  hy_w3[l], hy_freq[l], hy_decay[l], hy_dskip[l], need_ctx)
        out_l = merge_branches((ya_l, yb_l, yc_l, yd_l), pl[7], br_w[l], br_gate_b[l], w_o[l])
        xl = xl + (g1_l * out_l).astype(dt)
        hl2 = rms_norm(xl, norm2_g[l]) * (1 + sc2_l) + sh2_l
        if need_ctx:
            out_c = merge_branches((ya_c, yb_c, yc_c, yd_c), pc[7], br_w[l], br_gate_b[l], w_o[l])
            xc = xc + (g1_c * out_c).astype(dt)
            hc2 = rms_norm(xc, norm2_g[l]) * (1 + sc2_c) + sh2_c
            tokens = jnp.concatenate([hl2.reshape(-1, D), hc2.reshape(-1, D)], axis=0)
            f = moe_ffn(tokens, router_w[l], router_b[l], exp_w_gate[l], exp_w_up[l], exp_w_down[l],
                        sh_w_gate[l], sh_w_up[l], sh_w_down[l])
            f_l = f[:B * S].reshape(hl2.shape)
            xc = xc + (g2_c * f[B * S:].reshape(hc2.shape)).astype(dt)
        else:
            f_l = moe_ffn(hl2.reshape(-1, D), router_w[l], router_b[l], exp_w_gate[l], exp_w_up[l], exp_w_down[l],
                          sh_w_gate[l], sh_w_up[l], sh_w_down[l]).reshape(hl2.shape)
        xl = xl + (g2_l * f_l).astype(dt)
    return xl
```

```python
import functools
import math

import numpy as np
import jax
import jax.numpy as jnp
from jax import lax
from jax.experimental import pallas as pl
from jax.experimental.pallas import tpu as pltpu

D_MODEL = 2048
DEPTH = 2
CTX_LEN = 256
GRID_W = 64
NORM_EPS = 1e-6
N_BRANCH = 4
BRANCH_WIDTH = D_MODEL // 4
HEAD_DIM = 128

NA_HEADS = BRANCH_WIDTH // HEAD_DIM
NA_WIN_ROWS = 8
NA_WIN_COLS = 16

ML_HEADS = BRANCH_WIDTH // HEAD_DIM
ML_WIDTH = BRANCH_WIDTH
ML_CHUNK = 128
ROPE_BASE = 10000.0

LRU_WIDTH = BRANCH_WIDTH
LRU_BLOCKS = 4
LRU_BW = LRU_WIDTH // LRU_BLOCKS
LRU_C = 8.0

HY_WIDTH = BRANCH_WIDTH
HY_ORDER = 2
HY_EMB = 33

N_EXPERTS = 128
TOP_K = 8
N_GROUPS = 8
TOPK_GROUPS = 4
ROUTE_SCALE = 2.5

IN_SPLITS = (3 * BRANCH_WIDTH, 3 * ML_WIDTH, ML_WIDTH, 4 * ML_HEADS, LRU_WIDTH, LRU_WIDTH,
             (HY_ORDER + 1) * HY_WIDTH, N_BRANCH * D_MODEL)
IF_START = sum(IN_SPLITS[:3])
IF_WIDTH = IN_SPLITS[3]
MAIN_SPLITS = IN_SPLITS[:3] + IN_SPLITS[4:]
MAIN_OFF = np.concatenate([[0], np.cumsum(MAIN_SPLITS)]).tolist()
MAIN_WIDTH = MAIN_OFF[-1]
GATE_OFF = MAIN_OFF[6]

LANES = 128
ROW_TILE = 512
PROJ_COL_TILE = 1024
MERGE_COL_TILE = 512
MOE_ROW_TILE = 256
VMEM_LIMIT = 56 * 1024 * 1024

F32 = jnp.float32
BF16 = jnp.bfloat16


def _cparams(sem):
    return pltpu.CompilerParams(dimension_semantics=sem, vmem_limit_bytes=VMEM_LIMIT)


def _seg_of_tile(i, ctx_tiles, tiles_per_batch):
    return jnp.where(i < ctx_tiles, 0, 1 + (i - ctx_tiles) // tiles_per_batch)


def _norm_mod(x, g, shift, scale):
    ms = jnp.mean(x * x, axis=-1, keepdims=True)
    return (x * lax.rsqrt(ms + NORM_EPS) * g) * (1.0 + scale) + shift


def _proj_kernel(x_ref, g_ref, mod_ref, w_ref, o_ref, h_scr):
    @pl.when(pl.program_id(1) == 0)
    def _():
        h = _norm_mod(x_ref[...], g_ref[...], mod_ref[0, 0:1, :], mod_ref[0, 1:2, :])
        h_scr[...] = h.astype(BF16)
    o_ref[...] = jnp.dot(h_scr[...], w_ref[...], preferred_element_type=F32)


def norm_mod_project(x, g, mod, w, ctx_tiles, tiles_per_batch):
    T, D = x.shape
    N = w.shape[1]
    seg = functools.partial(_seg_of_tile, ctx_tiles=ctx_tiles, tiles_per_batch=tiles_per_batch)
    return pl.pallas_call(
        _proj_kernel,
        grid=(T // ROW_TILE, N // PROJ_COL_TILE),
        in_specs=[
            pl.BlockSpec((ROW_TILE, D), lambda i, j: (i, 0)),
            pl.BlockSpec((1, D), lambda i, j: (0, 0)),
            pl.BlockSpec((1, 2, D), lambda i, j: (seg(i), 0, 0)),
            pl.BlockSpec((D, PROJ_COL_TILE), lambda i, j: (0, j)),
        ],
        out_specs=pl.BlockSpec((ROW_TILE, PROJ_COL_TILE), lambda i, j: (i, j)),
        out_shape=jax.ShapeDtypeStruct((T, N), F32),
        scratch_shapes=[pltpu.VMEM((ROW_TILE, D), BF16)],
        compiler_params=_cparams(("arbitrary", "arbitrary")),
        name="norm_mod_project",
    )(x, g, mod, w)


def _narrow_kernel(x_ref, g_ref, mod_ref, w_ref, o_ref, h_ref):
    h = _norm_mod(x_ref[...], g_ref[...], mod_ref[0, 0:1, :], mod_ref[0, 1:2, :])
    h_ref[...] = h.astype(BF16)
    o_ref[...] = jnp.dot(h, w_ref[...], preferred_element_type=F32, precision=lax.Precision.HIGHEST)


def norm_mod_narrow(x, g, mod, w, ctx_tiles, tiles_per_batch):
    T, D = x.shape
    seg = functools.partial(_seg_of_tile, ctx_tiles=ctx_tiles, tiles_per_batch=tiles_per_batch)
    return pl.pallas_call(
        _narrow_kernel,
        grid=(T // ROW_TILE,),
        in_specs=[
            pl.BlockSpec((ROW_TILE, D), lambda i: (i, 0)),
            pl.BlockSpec((1, D), lambda i: (0, 0)),
            pl.BlockSpec((1, 2, D), lambda i: (seg(i), 0, 0)),
            pl.BlockSpec((D, LANES), lambda i: (0, 0)),
        ],
        out_specs=[pl.BlockSpec((ROW_TILE, LANES), lambda i: (i, 0)),
                   pl.BlockSpec((ROW_TILE, D), lambda i: (i, 0))],
        out_shape=[jax.ShapeDtypeStruct((T, LANES), F32), jax.ShapeDtypeStruct((T, D), BF16)],
        compiler_params=_cparams(("arbitrary",)),
        name="norm_mod_narrow",
    )(x, g, mod, w)


def _merge_kernel(ya_ref, yb_ref, yc_ref, yd_ref, ga_ref, gb_ref, gc_ref, gd_ref, bias_ref, bw_ref, wo_ref,
                  x_ref, g1_ref, o_ref, acc_ref):
    n = pl.program_id(1)
    z = None
    for b, (y_ref, gate_ref) in enumerate(((ya_ref, ga_ref), (yb_ref, gb_ref), (yc_ref, gc_ref), (yd_ref, gd_ref))):
        t = jnp.dot(y_ref[...].astype(BF16), bw_ref[b], preferred_element_type=F32)
        t = jax.nn.sigmoid(gate_ref[...] + bias_ref[b:b + 1, :]) * t
        z = t if z is None else z + t
    part = jnp.dot(z.astype(BF16), wo_ref[...], preferred_element_type=F32)

    @pl.when(n == 0)
    def _():
        acc_ref[...] = part

    @pl.when(n > 0)
    def _():
        acc_ref[...] += part

    @pl.when(n == pl.num_programs(1) - 1)
    def _():
        o_ref[...] = x_ref[...] + g1_ref[0] * acc_ref[...]


def merge_branches(ys, p, bias, bw, wo, x, g1, ctx_tiles, tiles_per_batch):
    T, D = x.shape
    ct = MERGE_COL_TILE
    seg = functools.partial(_seg_of_tile, ctx_tiles=ctx_tiles, tiles_per_batch=tiles_per_batch)
    y_spec = pl.BlockSpec((ROW_TILE, BRANCH_WIDTH), lambda i, n: (i, 0))

    def gate_spec(b):
        base = (GATE_OFF + b * D) // ct
        return pl.BlockSpec((ROW_TILE, ct), lambda i, n: (i, base + n))

    return pl.pallas_call(
        _merge_kernel,
        grid=(T // ROW_TILE, D // ct),
        in_specs=[y_spec, y_spec, y_spec, y_spec,
                  gate_spec(0), gate_spec(1), gate_spec(2), gate_spec(3),
                  pl.BlockSpec((N_BRANCH, ct), lambda i, n: (0, n)),
                  pl.BlockSpec((N_BRANCH, BRANCH_WIDTH, ct), lambda i, n: (0, 0, n)),
                  pl.BlockSpec((ct, D), lambda i, n: (n, 0)),
                  pl.BlockSpec((ROW_TILE, D), lambda i, n: (i, 0)),
                  pl.BlockSpec((1, 1, D), lambda i, n: (seg(i), 0, 0))],
        out_specs=pl.BlockSpec((ROW_TILE, D), lambda i, n: (i, 0)),
        out_shape=jax.ShapeDtypeStruct((T, D), F32),
        scratch_shapes=[pltpu.VMEM((ROW_TILE, D), F32)],
        compiler_params=_cparams(("arbitrary", "arbitrary")),
        name="merge_branches",
    )(*ys, p, p, p, p, bias, bw, wo, x, g1)


def _expert_kernel(be_ref, nb_ref, xs_ref, wt_ref, wg_ref, wu_ref, wd_ref, o_ref, wg_s, wu_s, wd_s):
    i = pl.program_id(0)
    live = i < nb_ref[0]

    @pl.when(live & ((i == 0) | (be_ref[i] != be_ref[jnp.maximum(i - 1, 0)])))
    def _():
        wg_s[...] = wg_ref[0].astype(BF16)
        wu_s[...] = wu_ref[0].astype(BF16)
        wd_s[...] = wd_ref[0].astype(BF16)

    @pl.when(live)
    def _():
        xb = xs_ref[...]
        a = jax.nn.silu(jnp.dot(xb, wg_s[...], preferred_element_type=F32))
        a = a * jnp.dot(xb, wu_s[...], preferred_element_type=F32)
        y = jnp.dot(a.astype(BF16), wd_s[...], preferred_element_type=F32)
        o_ref[...] = (y * wt_ref[...]).astype(o_ref.dtype)

    @pl.when(i >= nb_ref[0])
    def _():
        o_ref[...] = jnp.zeros_like(o_ref)


def routed_experts(block_exp, n_used, xs, slot_w, wg, wu, wd):
    cap, D = xs.shape
    H = wg.shape[-1]
    tb = MOE_ROW_TILE

    def row(i, be, nb):
        return (jnp.minimum(i, nb[0] - 1), 0)

    def wsel(i, be, nb):
        return (be[jnp.minimum(i, nb[0] - 1)], 0, 0)

    grid_spec = pltpu.PrefetchScalarGridSpec(
        num_scalar_prefetch=2,
        grid=(cap // tb,),
        in_specs=[pl.BlockSpec((tb, D), row),
                  pl.BlockSpec((tb, 1), row),
                  pl.BlockSpec((1, D, H), wsel),
                  pl.BlockSpec((1, D, H), wsel),
                  pl.BlockSpec((1, H, D), wsel)],
        out_specs=pl.BlockSpec((tb, D), lambda i, be, nb: (i, 0)),
        scratch_shapes=[pltpu.VMEM((D, H), BF16), pltpu.VMEM((D, H), BF16), pltpu.VMEM((H, D), BF16)],
    )
    return pl.pallas_call(
        _expert_kernel,
        grid_spec=grid_spec,
        out_shape=jax.ShapeDtypeStruct((cap, D), BF16),
        compiler_params=_cparams(("arbitrary",)),
        name="routed_experts",
    )(block_exp, n_used, xs, slot_w, wg, wu, wd)


def _ffn_out_kernel(h_ref, r_ref, x_ref, g2_ref, sg_ref, su_ref, sd_ref, o_ref):
    h = h_ref[...]
    a = jax.nn.silu(jnp.dot(h, sg_ref[...], preferred_element_type=F32))
    a = a * jnp.dot(h, su_ref[...], preferred_element_type=F32)
    shared = jnp.dot(a.astype(BF16), sd_ref[...], preferred_element_type=F32)
    o_ref[...] = x_ref[...] + g2_ref[0] * (r_ref[...] + shared)


def ffn_out(h, routed, x, g2, sg, su, sd, ctx_tiles, tiles_per_batch):
    T, D = x.shape
    H = sg.shape[-1]
    seg = functools.partial(_seg_of_tile, ctx_tiles=ctx_tiles, tiles_per_batch=tiles_per_batch)
    row = pl.BlockSpec((ROW_TILE, D), lambda i: (i, 0))
    return pl.pallas_call(
        _ffn_out_kernel,
        grid=(T // ROW_TILE,),
        in_specs=[row, row, row,
                  pl.BlockSpec((1, 1, D), lambda i: (seg(i), 0, 0)),
                  pl.BlockSpec((D, H), lambda i: (0, 0)),
                  pl.BlockSpec((D, H), lambda i: (0, 0)),
                  pl.BlockSpec((H, D), lambda i: (0, 0))],
        out_specs=row,
        out_shape=jax.ShapeDtypeStruct((T, D), F32),
        compiler_params=_cparams(("arbitrary",)),
        name="ffn_out",
    )(h, routed, x, g2, sg, su, sd)


def route_tokens(logits, router_b):
    T, E = logits.shape
    scores = jax.nn.sigmoid(logits)
    biased = scores + router_b
    grp_score = jnp.sum(lax.top_k(biased.reshape(T, N_GROUPS, E // N_GROUPS), 2)[0], axis=-1)
    _, grp_idx = lax.top_k(grp_score, TOPK_GROUPS)
    grp_mask = jnp.sum(jax.nn.one_hot(grp_idx, N_GROUPS), axis=1) > 0
    masked = jnp.where(jnp.repeat(grp_mask, E // N_GROUPS, axis=1), biased, -jnp.inf)
    _, idx = lax.top_k(masked, TOP_K)
    wts = jnp.take_along_axis(scores, idx, axis=-1)
    wts = wts / jnp.sum(wts, axis=-1, keepdims=True) * ROUTE_SCALE
    return idx, wts


def slot_layout(idx, wts, T):
    E, tb = N_EXPERTS, MOE_ROW_TILE
    n_assign = T * TOP_K
    flat_e = idx.reshape(-1)
    order = jnp.argsort(flat_e)
    se = flat_e[order]
    stok = (order // TOP_K).astype(jnp.int32)
    sw = wts.reshape(-1)[order]
    counts = jnp.sum(jax.nn.one_hot(flat_e, E, dtype=jnp.int32), axis=0)
    padded = (counts + tb - 1) // tb * tb
    pend = jnp.cumsum(padded)
    cstart = jnp.cumsum(counts) - counts
    dest = ((pend - padded)[se] + jnp.arange(n_assign) - cstart[se]).astype(jnp.int32)
    n_blocks = -(-n_assign // tb) + E
    cap = n_blocks * tb
    slot_tok = jnp.full((cap,), T, jnp.int32).at[dest].set(stok)
    slot_w = jnp.zeros((cap,), F32).at[dest].set(sw)
    block_exp = jnp.minimum(jnp.searchsorted(pend, jnp.arange(n_blocks) * tb, side='right'), E - 1).astype(jnp.int32)
    n_used = (pend[-1] // tb).astype(jnp.int32).reshape(1)
    pos = jnp.zeros((n_assign,), jnp.int32).at[order].set(dest)
    return slot_tok, slot_w, block_exp, n_used, pos


def rms_norm(x, g):
    xf = x.astype(F32)
    y = xf * lax.rsqrt(jnp.mean(xf * xf, axis=-1, keepdims=True) + NORM_EPS)
    return (y * g.astype(F32)).astype(x.dtype)


def centred_dwconv(x, w, b):
    K, L = w.shape[0], x.shape[1]
    lo = (K - 1) // 2
    xp = jnp.pad(x, ((0, 0), (lo, K - 1 - lo), (0, 0)))
    y = b
    for j in range(K):
        y = y + xp[:, j:j + L] * w[j]
    return y


def to_heads(a, n_heads):
    B, L, W = a.shape
    return a.reshape(B, L, n_heads, W // n_heads).transpose(0, 2, 1, 3)


def from_heads(a):
    B, H, L, Dh = a.shape
    return a.transpose(0, 2, 1, 3).reshape(B, L, H * Dh)


def rope_2d(x, rows, cols):
    dh = x.shape[-1]
    half, quarter = dh // 2, dh // 4
    inv = ROPE_BASE ** (-jnp.arange(quarter, dtype=F32) / quarter)

    def rot(xa, pos):
        ang = pos.astype(F32)[:, None] * inv
        cos, sin = jnp.cos(ang).astype(xa.dtype), jnp.sin(ang).astype(xa.dtype)
        x1, x2 = xa[..., :quarter], xa[..., quarter:]
        return jnp.concatenate([x1 * cos - x2 * sin, x1 * sin + x2 * cos], axis=-1)
    return jnp.concatenate([rot(x[..., :half], rows), rot(x[..., half:], cols)], axis=-1)


def ctx_attention(q, k, v):
    s = jnp.einsum('bhqd,bhkd->bhqk', q, k).astype(F32)
    p = jax.nn.softmax(s, axis=-1).astype(v.dtype)
    return jnp.einsum('bhqk,bhkd->bhqd', p, v)


def neighbourhood_attention(q, k, v, kc, vc, rpb):
    B, H, S, Dh = q.shape
    rows = S // GRID_W
    kr = min(NA_WIN_ROWS, rows)
    qg = q.reshape(B, H, rows, GRID_W, Dh)
    kg = k.reshape(B, H, rows, GRID_W, Dh)
    vg = v.reshape(B, H, rows, GRID_W, Dh)
    col = np.arange(GRID_W)
    col_idx = np.clip(col - NA_WIN_COLS // 2, 0, GRID_W - NA_WIN_COLS)[:, None] + np.arange(NA_WIN_COLS)
    rpb_c = rpb[:, :, col_idx - col[:, None] + NA_WIN_COLS - 1]
    n_lat = kr * NA_WIN_COLS

    def row_block(r):
        r0 = jnp.clip(r - kr // 2, 0, rows - kr)
        qr = lax.dynamic_index_in_dim(qg, r, axis=2, keepdims=False)
        kw = lax.dynamic_slice_in_dim(kg, r0, kr, axis=2)[:, :, :, col_idx]
        vw = lax.dynamic_slice_in_dim(vg, r0, kr, axis=2)[:, :, :, col_idx]
        bias = jnp.take(rpb_c, r0 + jnp.arange(kr) - r + NA_WIN_ROWS - 1, axis=1)
        s_lat = jnp.einsum('bhqd,bhrqjd->bhqrj', qr, kw).astype(F32) + bias.transpose(0, 2, 1, 3)
        s_ctx = jnp.einsum('bhqd,bhkd->bhqk', qr, kc).astype(F32)
        p = jax.nn.softmax(jnp.concatenate([s_lat.reshape(B, H, GRID_W, n_lat), s_ctx], axis=-1), axis=-1).astype(v.dtype)
        p_lat = p[..., :n_lat].reshape(B, H, GRID_W, kr, NA_WIN_COLS)
        return jnp.einsum('bhqrj,bhrqjd->bhqd', p_lat, vw) + jnp.einsum('bhqk,bhkd->bhqd', p[..., n_lat:], vc)
    out = lax.map(row_block, jnp.arange(rows))
    return out.transpose(1, 2, 0, 3, 4).reshape(B, H, S, Dh)


def na_branch(pc, pl_, qg, kg, rpb, need_ctx):
    def qkv(p):
        q, k, v = jnp.split(p, 3, axis=-1)
        q = rms_norm(to_heads(q, NA_HEADS), qg) * (HEAD_DIM ** -0.5)
        k = rms_norm(to_heads(k, NA_HEADS), kg)
        return q, k, to_heads(v, NA_HEADS)
    qc, kc, vc = qkv(pc)
    ql, kl, vl = qkv(pl_)
    yl = from_heads(neighbourhood_attention(ql, kl, vl, kc, vc, rpb))
    yc = from_heads(ctx_attention(qc, kc, vc)) if need_ctx else None
    return yc, yl


def mlstm_scan(q, k, v, ig, fg, state):
    B, H, L, Dh = q.shape
    nc = L // ML_CHUNK

    def chunks(a):
        return jnp.moveaxis(a.reshape(B, H, nc, ML_CHUNK, *a.shape[3:]), 2, 0)
    lower = jnp.tril(jnp.ones((ML_CHUNK, ML_CHUNK), dtype=bool))

    def step(carry, xs):
        C, n, m = carry
        qc, kc, vc, ic, fc = xs
        b = jnp.cumsum(jax.nn.log_sigmoid(fc), axis=-1)
        d = jnp.where(lower, b[..., :, None] - b[..., None, :] + ic[..., None, :], -jnp.inf)
        m_t = jnp.maximum(b + m[..., None], jnp.max(d, axis=-1))
        w = jnp.exp(d - m_t[..., None]) * jnp.einsum('bhtd,bhsd->bhts', qc, kc)
        carry_w = jnp.exp(b + m[..., None] - m_t)
        num = jnp.einsum('bhts,bhsd->bhtd', w, vc) + carry_w[..., None] * jnp.einsum('bhtd,bhde->bhte', qc, C)
        den = jnp.sum(w, axis=-1) + carry_w * jnp.einsum('bhtd,bhd->bht', qc, n)
        h = num / jnp.maximum(jnp.abs(den), jnp.exp(-m_t))[..., None]
        g = b[..., -1:] - b + ic
        m_new = jnp.maximum(b[..., -1] + m, jnp.max(g, axis=-1))
        keep = jnp.exp(b[..., -1] + m - m_new)
        wg = jnp.exp(g - m_new[..., None])
        C = keep[..., None, None] * C + jnp.einsum('bhs,bhsd,bhse->bhde', wg, kc, vc)
        n = keep[..., None] * n + jnp.einsum('bhs,bhsd->bhd', wg, kc)
        return (C, n, m_new), h
    state, hs = lax.scan(step, state, (chunks(q), chunks(k), chunks(v), chunks(ig), chunks(fg)))
    return jnp.moveaxis(hs, 0, 2).reshape(B, H, L, Dh), state


def mlstm_branch(pc_qkv, pl_qkv, pc_o, pl_o, pc_if, pl_if, conv_w, conv_b, gate_b, out_g, rows, cols, need_ctx):
    def prep(p_qkv, p_if, rotary):
        q, k, v = jnp.split(p_qkv, 3, axis=-1)
        qk = jax.nn.silu(centred_dwconv(jnp.concatenate([q, k], axis=-1), conv_w, conv_b))
        q, k = jnp.split(qk, 2, axis=-1)
        q = to_heads(q, ML_HEADS).astype(F32)
        k = to_heads(k, ML_HEADS).astype(F32)
        v = to_heads(v, ML_HEADS).astype(F32)
        if rotary:
            q, k = rope_2d(q, rows, cols), rope_2d(k, rows, cols)
        k = k * (HEAD_DIM ** -0.5)
        B, L, _ = p_if.shape
        g = p_if.astype(F32).reshape(B, L, 2, 2, ML_HEADS) + gate_b
        return q, k, v, g.transpose(2, 3, 0, 4, 1)
    qc, kc, vc, gc = prep(pc_qkv, pc_if, False)
    ql, kl, vl, gl = prep(pl_qkv, pl_if, True)
    B = ql.shape[0]
    zero = (jnp.zeros((B, ML_HEADS, HEAD_DIM, HEAD_DIM), F32),
            jnp.zeros((B, ML_HEADS, HEAD_DIM), F32), jnp.zeros((B, ML_HEADS), F32))
    flip = lambda a: jnp.flip(a, axis=2)
    hc_f, st_f = mlstm_scan(qc, kc, vc, gc[0, 0], gc[0, 1], zero)
    hl_f, _ = mlstm_scan(ql, kl, vl, gl[0, 0], gl[0, 1], st_f)
    hc_b, st_b = mlstm_scan(flip(qc), flip(kc), flip(vc), flip(gc[1, 0]), flip(gc[1, 1]), zero)
    hl_b, _ = mlstm_scan(flip(ql), flip(kl), flip(vl), flip(gl[1, 0]), flip(gl[1, 1]), st_b)

    def readout(h, o):
        h = from_heads(rms_norm(h, out_g.reshape(ML_HEADS, 1, HEAD_DIM)))
        return (h * jax.nn.sigmoid(o.astype(F32))).astype(o.dtype)
    yl = readout(hl_f + flip(hl_b), pl_o)
    yc = readout(hc_f + flip(hc_b), pc_o) if need_ctx else None
    return yc, yl


def rglru_scan(x, wa, ba, wx, bx, lam, h0):
    B, L, W = x.shape
    xb = x.reshape(B, L, LRU_BLOCKS, LRU_BW)
    r = jax.nn.sigmoid(jnp.einsum('blnd,nde->blne', xb, wa).reshape(B, L, W) + ba)
    i = jax.nn.sigmoid(jnp.einsum('blnd,nde->blne', xb, wx).reshape(B, L, W) + bx)
    log_a = -LRU_C * r * jax.nn.softplus(-lam)
    u = jnp.sqrt(-jnp.expm1(2.0 * log_a)) * (i * x)

    def combine(e1, e2):
        return e1[0] * e2[0], e2[0] * e1[1] + e2[1]
    a_cum, h = lax.associative_scan(combine, (jnp.exp(log_a), u), axis=1)
    h = h + a_cum * h0[:, None, :]
    return h, h[:, -1]


def lru_branch(pc_x, pl_x, pc_g, pl_g, conv_w, conv_b, wa, ba, wx, bx, lam, need_ctx):
    xc = centred_dwconv(pc_x, conv_w, conv_b).astype(F32)
    xl = centred_dwconv(pl_x, conv_w, conv_b).astype(F32)
    h0 = jnp.zeros((xl.shape[0], LRU_WIDTH), F32)
    flip = lambda a: jnp.flip(a, axis=1)
    hc_f, s_f = rglru_scan(xc, wa[0], ba[0], wx[0], bx[0], lam[0], h0)
    hl_f, _ = rglru_scan(xl, wa[0], ba[0], wx[0], bx[0], lam[0], s_f)
    hc_b, s_b = rglru_scan(flip(xc), wa[1], ba[1], wx[1], bx[1], lam[1], h0)
    hl_b, _ = rglru_scan(flip(xl), wa[1], ba[1], wx[1], bx[1], lam[1], s_b)
    yl = ((hl_f + flip(hl_b)) * jax.nn.gelu(pl_g.astype(F32))).astype(pl_g.dtype)
    yc = ((hc_f + flip(hc_b)) * jax.nn.gelu(pc_g.astype(F32))).astype(pc_g.dtype) if need_ctx else None
    return yc, yl


def hyena_filters(L, w1, b1, w2, b2, w3, freq, decay):
    n = jnp.arange(L, dtype=F32)
    t = n / max(L - 1, 1)
    bands = (HY_EMB - 1) // 2
    f = jnp.linspace(1e-4, bands - 1, bands, dtype=F32)
    ang = (2.0 * math.pi / L) * n[:, None] * f
    z = jnp.concatenate([t[:, None], jnp.cos(ang), -jnp.sin(ang)], axis=-1)
    a = jnp.sin(freq * (z @ w1 + b1))
    a = jnp.sin(freq * (a @ w2 + b2))
    h = ((a @ w3) * jnp.exp(-t[:, None] * jnp.abs(decay))).astype(F32)
    h = h.reshape(L, HY_ORDER, 2, HY_WIDTH)
    l1 = jnp.sum(jnp.abs(h[:, :, 0]), axis=0) + jnp.sum(jnp.abs(h[1:, :, 1]), axis=0)
    return h / l1[None, :, None]


def long_conv(u, hf, hb, d):
    B, L, W = u.shape
    taps = jnp.concatenate([hf, jnp.zeros((1, W), hf.dtype), hb[:0:-1]], axis=0)
    y = jnp.fft.irfft(jnp.fft.rfft(u, n=2 * L, axis=1) * jnp.fft.rfft(taps, axis=0)[None], n=2 * L, axis=1)[:, :L]
    return y + u * d


def hyena_branch(pc, pl_, conv_w, conv_b, w1, b1, w2, b2, w3, freq, decay, dskip, need_ctx):
    def run(p):
        L = p.shape[1]
        u = centred_dwconv(p, conv_w, conv_b).astype(F32)
        parts = jnp.split(u, HY_ORDER + 1, axis=-1)
        filt = hyena_filters(L, w1, b1, w2, b2, w3, freq, decay)
        z = parts[0]
        for o in range(HY_ORDER):
            z = parts[o + 1] * long_conv(z, filt[:, o, 0], filt[:, o, 1], dskip[o])
        return z.astype(p.dtype)
    yl = run(pl_)
    yc = run(pc) if need_ctx else None
    return yc, yl


def kernel(x, c, ctx, c_ctx, norm1_g, norm2_g, ada_w, ada_b, w_in, na_qnorm_g, na_knorm_g, na_rpb,
           ml_conv_w, ml_conv_b, ml_gate_b, ml_out_g, lru_conv_w, lru_conv_b, lru_wa, lru_ba, lru_wx, lru_bx,
           lru_lambda, hy_conv_w, hy_conv_b, hy_w1, hy_b1, hy_w2, hy_b2, hy_w3, hy_freq, hy_decay, hy_dskip,
           br_w, br_gate_b, w_o, router_w, router_b, exp_w_gate, exp_w_up, exp_w_down,
           sh_w_gate, sh_w_up, sh_w_down):
    B, S, D = x.shape
    n_ctx = B * ctx.shape[1]
    T = n_ctx + B * S
    assert n_ctx % ROW_TILE == 0 and S % ROW_TILE == 0
    ctx_tiles, tiles_per_batch = n_ctx // ROW_TILE, S // ROW_TILE
    tiles = dict(ctx_tiles=ctx_tiles, tiles_per_batch=tiles_per_batch)
    pos = jnp.arange(S)
    rows, cols = pos // GRID_W, pos % GRID_W

    xa = jnp.concatenate([ctx.reshape(n_ctx, D), x.reshape(B * S, D)], axis=0)
    for l in range(DEPTH):
        need_ctx = l < DEPTH - 1
        cond = jnp.concatenate([c_ctx[None, :], c], axis=0)
        mod = (jax.nn.silu(cond) @ ada_w[l] + ada_b[l]).reshape(1 + B, 6, D)
        sh1sc1, g1, sh2sc2, g2 = mod[:, 0:2], mod[:, 2:3], mod[:, 3:5], mod[:, 5:6]

        w_l = w_in[l]
        w_main = jnp.concatenate([w_l[:, :IF_START], w_l[:, IF_START + IF_WIDTH:]], axis=1).astype(BF16)
        w_if = jnp.pad(w_l[:, IF_START:IF_START + IF_WIDTH], ((0, 0), (0, LANES - IF_WIDTH)))
        p = norm_mod_project(xa, norm1_g[l][None, :], sh1sc1, w_main, **tiles)
        p_if, _ = norm_mod_narrow(xa, norm1_g[l][None, :], sh1sc1, w_if, **tiles)
        p_if = p_if[:, :IF_WIDTH]

        def part(k, lat):
            a = p[:, MAIN_OFF[k]:MAIN_OFF[k + 1]]
            return a[n_ctx:].reshape(B, S, -1) if lat else a[:n_ctx].reshape(B, CTX_LEN, -1)
        pc_if, pl_if = p_if[:n_ctx].reshape(B, CTX_LEN, -1), p_if[n_ctx:].reshape(B, S, -1)

        ya = na_branch(part(0, False), part(0, True), na_qnorm_g[l], na_knorm_g[l], na_rpb[l], need_ctx)
        yb = mlstm_branch(part(1, False), part(1, True), part(2, False), part(2, True), pc_if, pl_if,
                          ml_conv_w[l], ml_conv_b[l], ml_gate_b[l], ml_out_g[l], rows, cols, need_ctx)
        yc = lru_branch(part(3, False), part(3, True), part(4, False), part(4, True), lru_conv_w[l], lru_conv_b[l],
                        lru_wa[l], lru_ba[l], lru_wx[l], lru_bx[l], lru_lambda[l], need_ctx)
        yd = hyena_branch(part(5, False), part(5, True), hy_conv_w[l], hy_conv_b[l], hy_w1[l], hy_b1[l], hy_w2[l],
                          hy_b2[l], hy_w3[l], hy_freq[l], hy_decay[l], hy_dskip[l], need_ctx)

        def rows_of(y):
            y_c, y_l = y
            y_c = jnp.zeros((n_ctx, BRANCH_WIDTH), F32) if y_c is None else y_c.reshape(n_ctx, BRANCH_WIDTH)
            return jnp.concatenate([y_c, y_l.reshape(B * S, BRANCH_WIDTH)], axis=0)
        xa = merge_branches([rows_of(y) for y in (ya, yb, yc, yd)], p, br_gate_b[l], br_w[l].astype(BF16),
                            w_o[l].astype(BF16), xa, g1, **tiles)

        logits, h2 = norm_mod_narrow(xa, norm2_g[l][None, :], sh2sc2, router_w[l], **tiles)
        idx, wts = route_tokens(logits, router_b[l])
        slot_tok, slot_w, block_exp, n_used, slot_of = slot_layout(idx, wts, T)
        h2_pad = jnp.concatenate([h2, jnp.zeros((1, D), BF16)], axis=0)
        ys = routed_experts(block_exp, n_used, h2_pad[slot_tok], slot_w[:, None], exp_w_gate[l], exp_w_up[l],
                            exp_w_down[l])
        routed = jnp.sum(ys[slot_of].reshape(T, TOP_K, D).astype(F32), axis=1)
        xa = ffn_out(h2, routed, xa, g2, sh_w_gate[l].astype(BF16), sh_w_up[l].astype(BF16),
                     sh_w_down[l].astype(BF16), **tiles)
    return xa[n_ctx:].reshape(B, S, D)
```

```python
import functools
import math

import numpy as np
import jax
import jax.numpy as jnp
from jax import lax
from jax.experimental import pallas as pl
from jax.experimental.pallas import tpu as pltpu

D_MODEL = 2048
DEPTH = 2
CTX_LEN = 256
GRID_W = 64
NORM_EPS = 1e-6
N_BRANCH = 4
BRANCH_WIDTH = D_MODEL // 4
HEAD_DIM = 128

NA_HEADS = BRANCH_WIDTH // HEAD_DIM
NA_WIN_ROWS = 8
NA_WIN_COLS = 16

ML_HEADS = BRANCH_WIDTH // HEAD_DIM
ML_WIDTH = BRANCH_WIDTH
ML_CHUNK = 128
ROPE_BASE = 10000.0

LRU_WIDTH = BRANCH_WIDTH
LRU_BLOCKS = 4
LRU_BW = LRU_WIDTH // LRU_BLOCKS
LRU_C = 8.0
LRU_CONV = 4

HY_WIDTH = BRANCH_WIDTH
HY_ORDER = 2
HY_EMB = 33

N_EXPERTS = 128
TOP_K = 8
N_GROUPS = 8
GROUP_SIZE = N_EXPERTS // N_GROUPS
TOPK_GROUPS = 4
ROUTE_SCALE = 2.5

IN_SPLITS = (3 * BRANCH_WIDTH, 3 * ML_WIDTH, ML_WIDTH, 4 * ML_HEADS, LRU_WIDTH, LRU_WIDTH,
             (HY_ORDER + 1) * HY_WIDTH, N_BRANCH * D_MODEL)
IF_START = sum(IN_SPLITS[:3])
IF_WIDTH = IN_SPLITS[3]
MAIN_SPLITS = IN_SPLITS[:3] + IN_SPLITS[4:]
MAIN_OFF = np.concatenate([[0], np.cumsum(MAIN_SPLITS)]).tolist()
MAIN_WIDTH = MAIN_OFF[-1]
NA_OFF, LRU_X_OFF, LRU_G_OFF, GATE_OFF = MAIN_OFF[0], MAIN_OFF[3], MAIN_OFF[4], MAIN_OFF[6]

LANES = 128
SUBLANES = 8
ROW_TILE = 512
PROJ_COL_TILE = 1024
MERGE_COL_TILE = 512
MOE_ROW_TILE = 256
ROUTE_TILE = 512
SEQ_CHUNK = CTX_LEN
NA_G = 4
NEG = -1e30
VMEM_LIMIT = 56 * 1024 * 1024

F32 = jnp.float32
BF16 = jnp.bfloat16


def _cparams(sem):
    return pltpu.CompilerParams(dimension_semantics=sem, vmem_limit_bytes=VMEM_LIMIT)


def _seg_of_tile(i, ctx_tiles, tiles_per_batch):
    return jnp.where(i < ctx_tiles, 0, 1 + (i - ctx_tiles) // tiles_per_batch)


def _norm_mod(x, g, shift, scale):
    ms = jnp.mean(x * x, axis=-1, keepdims=True)
    return (x * lax.rsqrt(ms + NORM_EPS) * g) * (1.0 + scale) + shift


def _head_norm(x, g):
    return x * lax.rsqrt(jnp.mean(x * x, axis=-1, keepdims=True) + NORM_EPS) * g


def _proj_kernel(x_ref, g_ref, mod_ref, w_ref, o_ref, h_scr):
    @pl.when(pl.program_id(1) == 0)
    def _():
        h = _norm_mod(x_ref[...], g_ref[...], mod_ref[0, 0:1, :], mod_ref[0, 1:2, :])
        h_scr[...] = h.astype(BF16)
    o_ref[...] = jnp.dot(h_scr[...], w_ref[...], preferred_element_type=F32)


def norm_mod_project(x, g, mod, w, ctx_tiles, tiles_per_batch):
    T, D = x.shape
    N = w.shape[1]
    seg = functools.partial(_seg_of_tile, ctx_tiles=ctx_tiles, tiles_per_batch=tiles_per_batch)
    return pl.pallas_call(
        _proj_kernel,
        grid=(T // ROW_TILE, N // PROJ_COL_TILE),
        in_specs=[
            pl.BlockSpec((ROW_TILE, D), lambda i, j: (i, 0)),
            pl.BlockSpec((1, D), lambda i, j: (0, 0)),
            pl.BlockSpec((1, 2, D), lambda i, j: (seg(i), 0, 0)),
            pl.BlockSpec((D, PROJ_COL_TILE), lambda i, j: (0, j)),
        ],
        out_specs=pl.BlockSpec((ROW_TILE, PROJ_COL_TILE), lambda i, j: (i, j)),
        out_shape=jax.ShapeDtypeStruct((T, N), F32),
        scratch_shapes=[pltpu.VMEM((ROW_TILE, D), BF16)],
        compiler_params=_cparams(("arbitrary", "arbitrary")),
        name="norm_mod_project",
    )(x, g, mod, w)


def _narrow_kernel(x_ref, g_ref, mod_ref, w_ref, o_ref, h_ref):
    h = _norm_mod(x_ref[...], g_ref[...], mod_ref[0, 0:1, :], mod_ref[0, 1:2, :])
    h_ref[...] = h.astype(BF16)
    o_ref[...] = jnp.dot(h, w_ref[...], preferred_element_type=F32, precision=lax.Precision.HIGHEST)


def norm_mod_narrow(x, g, mod, w, ctx_tiles, tiles_per_batch):
    T, D = x.shape
    seg = functools.partial(_seg_of_tile, ctx_tiles=ctx_tiles, tiles_per_batch=tiles_per_batch)
    return pl.pallas_call(
        _narrow_kernel,
        grid=(T // ROW_TILE,),
        in_specs=[
            pl.BlockSpec((ROW_TILE, D), lambda i: (i, 0)),
            pl.BlockSpec((1, D), lambda i: (0, 0)),
            pl.BlockSpec((1, 2, D), lambda i: (seg(i), 0, 0)),
            pl.BlockSpec((D, LANES), lambda i: (0, 0)),
        ],
        out_specs=[pl.BlockSpec((ROW_TILE, LANES), lambda i: (i, 0)),
                   pl.BlockSpec((ROW_TILE, D), lambda i: (i, 0))],
        out_shape=[jax.ShapeDtypeStruct((T, LANES), F32), jax.ShapeDtypeStruct((T, D), BF16)],
        compiler_params=_cparams(("arbitrary",)),
        name="norm_mod_narrow",
    )(x, g, mod, w)


def _na_kernel(q_ref, k0_ref, k1_ref, k2_ref, v0_ref, v1_ref, v2_ref, kc_ref, vc_ref, bias_ref, qg_ref, kg_ref,
               o_ref, *, n_rows):
    s = pl.program_id(1)
    qb, kw = bias_ref.shape[1], bias_ref.shape[2]
    first_row = (s - 1) * NA_G
    q_row = first_row + lax.broadcasted_iota(jnp.int32, (qb, kw), 0) // GRID_W
    k_row = first_row - NA_G + lax.broadcasted_iota(jnp.int32, (qb, kw), 1) // GRID_W
    r0 = jnp.clip(q_row - NA_WIN_ROWS // 2, 0, n_rows - NA_WIN_ROWS)
    valid = (k_row >= r0) & (k_row < r0 + NA_WIN_ROWS) & (s > 0)
    nt = (((1,), (1,)), ((), ()))
    for h in range(NA_HEADS):
        hs = slice(h * HEAD_DIM, (h + 1) * HEAD_DIM)
        qn = (_head_norm(q_ref[:, hs], qg_ref[...]) * (HEAD_DIM ** -0.5)).astype(BF16)
        k_win = jnp.concatenate([k0_ref[:, hs], k1_ref[:, hs], k2_ref[:, hs]], axis=0)
        kn = _head_norm(k_win, kg_ref[...]).astype(BF16)
        s_lat = lax.dot_general(qn, kn, nt, preferred_element_type=F32)
        s_lat = jnp.where(valid, s_lat + bias_ref[h], NEG)
        kcn = _head_norm(kc_ref[:, hs], kg_ref[...]).astype(BF16)
        s_ctx = lax.dot_general(qn, kcn, nt, preferred_element_type=F32)
        m = jnp.maximum(jnp.max(s_lat, axis=-1, keepdims=True), jnp.max(s_ctx, axis=-1, keepdims=True))
        p_lat = jnp.exp(s_lat - m)
        p_ctx = jnp.exp(s_ctx - m)
        denom = jnp.sum(p_lat, axis=-1, keepdims=True) + jnp.sum(p_ctx, axis=-1, keepdims=True)
        v_win = jnp.concatenate([v0_ref[:, hs], v1_ref[:, hs], v2_ref[:, hs]], axis=0).astype(BF16)
        o = jnp.dot(p_lat.astype(BF16), v_win, preferred_element_type=F32)
        o = o + jnp.dot(p_ctx.astype(BF16), vc_ref[:, hs].astype(BF16), preferred_element_type=F32)
        o_ref[:, hs] = o / denom


def na_bias_table(rpb):
    qb, kw = NA_G * GRID_W, 3 * NA_G * GRID_W
    qg, qc = np.arange(qb)[:, None] // GRID_W, np.arange(qb)[:, None] % GRID_W
    kr, kc = np.arange(kw)[None, :] // GRID_W, np.arange(kw)[None, :] % GRID_W
    dr = kr - NA_G - qg
    c0 = np.clip(qc - NA_WIN_COLS // 2, 0, GRID_W - NA_WIN_COLS)
    ok = (kc >= c0) & (kc < c0 + NA_WIN_COLS) & (np.abs(dr) < NA_WIN_ROWS)
    ri = np.clip(dr + NA_WIN_ROWS - 1, 0, 2 * NA_WIN_ROWS - 2)
    ci = np.clip(kc - qc + NA_WIN_COLS - 1, 0, 2 * NA_WIN_COLS - 2)
    return jnp.where(ok[None], rpb[:, ri, ci], NEG)


def neighbourhood_attention(p, rpb, qg, kg, B, S):
    T = p.shape[0]
    qb = NA_G * GRID_W
    assert qb == CTX_LEN and NA_WIN_ROWS // 2 <= NA_G and NA_WIN_ROWS - NA_WIN_ROWS // 2 - 1 <= NA_G
    ctx_blocks, n_blk, n_rows = B * CTX_LEN // qb, S // qb, S // GRID_W
    c_q = NA_OFF // BRANCH_WIDTH
    c_k, c_v = c_q + 1, c_q + 2

    def q_map(b, s):
        return (jnp.where(s == 0, b, ctx_blocks + b * n_blk + s - 1), c_q)

    def win_spec(d, col):
        return pl.BlockSpec((qb, BRANCH_WIDTH),
                            lambda b, s: (ctx_blocks + b * n_blk + jnp.clip(s - 1 + d, 0, n_blk - 1), col))

    bias = na_bias_table(rpb)
    return pl.pallas_call(
        functools.partial(_na_kernel, n_rows=n_rows),
        grid=(B, 1 + n_blk),
        in_specs=[pl.BlockSpec((qb, BRANCH_WIDTH), q_map),
                  win_spec(-1, c_k), win_spec(0, c_k), win_spec(1, c_k),
                  win_spec(-1, c_v), win_spec(0, c_v), win_spec(1, c_v),
                  pl.BlockSpec((qb, BRANCH_WIDTH), lambda b, s: (b, c_k)),
                  pl.BlockSpec((qb, BRANCH_WIDTH), lambda b, s: (b, c_v)),
                  pl.BlockSpec(bias.shape, lambda b, s: (0, 0, 0)),
                  pl.BlockSpec((1, HEAD_DIM), lambda b, s: (0, 0)),
                  pl.BlockSpec((1, HEAD_DIM), lambda b, s: (0, 0))],
        out_specs=pl.BlockSpec((qb, BRANCH_WIDTH), lambda b, s: (q_map(b, s)[0], 0)),
        out_shape=jax.ShapeDtypeStruct((T, BRANCH_WIDTH), F32),
        compiler_params=_cparams(("arbitrary", "arbitrary")),
        name="neighbourhood_attention",
    )(p, p, p, p, p, p, p, p, p, bias, qg[None, :], kg[None, :])


def _lru_kernel(x_ref, prev_ref, next_ref, cw_ref, cb_ref, wa_ref, ba_ref, wx_ref, bx_ref, sp_ref, *rest, reverse):
    if reverse:
        hf_ref, g_ref, o_ref, carry = rest
    else:
        o_ref, carry = rest
    s = pl.program_id(1)
    n_lat = pl.num_programs(1) - 1
    tc, w = x_ref.shape

    @pl.when(s == 0)
    def _():
        carry[...] = jnp.zeros_like(carry)

    lat_idx = (n_lat - s) if reverse else (s - 1)
    has_prev = jnp.where((s > 0) & (lat_idx > 0), 1.0, 0.0)
    has_next = jnp.where((s > 0) & (lat_idx < n_lat - 1), 1.0, 0.0)
    x = x_ref[...]
    row = lax.broadcasted_iota(jnp.int32, (tc, w), 0)
    before = prev_ref[SUBLANES - 1:SUBLANES, :] * has_prev
    after0 = next_ref[0:1, :] * has_next
    after1 = next_ref[1:2, :] * has_next
    xm1 = jnp.where(row == 0, before, pltpu.roll(x, 1, 0))
    xp1 = jnp.where(row == tc - 1, after0, pltpu.roll(x, tc - 1, 0))
    xp2 = jnp.where(row == tc - 2, after0, jnp.where(row == tc - 1, after1, pltpu.roll(x, tc - 2, 0)))
    xc = cb_ref[...] + xm1 * cw_ref[0:1, :] + x * cw_ref[1:2, :] + xp1 * cw_ref[2:3, :] + xp2 * cw_ref[3:4, :]

    def block_diag(w_ref):
        return jnp.concatenate(
            [jnp.dot(xc[:, n * LRU_BW:(n + 1) * LRU_BW], w_ref[n], preferred_element_type=F32,
                     precision=lax.Precision.HIGHEST) for n in range(LRU_BLOCKS)], axis=-1)
    r = jax.nn.sigmoid(block_diag(wa_ref) + ba_ref[...])
    i = jax.nn.sigmoid(block_diag(wx_ref) + bx_ref[...])
    log_a = -LRU_C * r * sp_ref[...]
    a = jnp.exp(log_a)
    u = jnp.sqrt(1.0 - jnp.exp(2.0 * log_a)) * (i * xc)

    step = 1
    while step < tc:
        if reverse:
            keep = row < tc - step
            shift = tc - step
        else:
            keep = row >= step
            shift = step
        a_sh = jnp.where(keep, pltpu.roll(a, shift, 0), 1.0)
        u_sh = jnp.where(keep, pltpu.roll(u, shift, 0), 0.0)
        u = a * u_sh + u
        a = a * a_sh
        step *= 2
    h = u + a * carry[...]
    carry[...] = h[0:1, :] if reverse else h[tc - 1:tc, :]
    if reverse:
        o_ref[...] = (hf_ref[...] + h) * jax.nn.gelu(g_ref[...])
    else:
        o_ref[...] = h


def rglru(p, conv_w, conv_b, wa, ba, wx, bx, lam, B, S):
    T = p.shape[0]
    tc, w = SEQ_CHUNK, LRU_WIDTH
    assert CTX_LEN == tc and S % tc == 0 and LRU_CONV == 4
    n_lat, ctx_blocks = S // tc, B
    halo = tc // SUBLANES
    c_x, c_g = LRU_X_OFF // w, LRU_G_OFF // w
    sp = jax.nn.softplus(-lam)

    def call(reverse, extra_in, extra_specs):
        def blk(b, s):
            lat = (n_lat - s) if reverse else (s - 1)
            return jnp.where(s == 0, b, ctx_blocks + b * n_lat + lat)

        def full(shape):
            return pl.BlockSpec(shape, lambda b, s: (0,) * len(shape))
        d = 1 if reverse else 0
        chunk = pl.BlockSpec((tc, w), lambda b, s: (blk(b, s), c_x))
        return pl.pallas_call(
            functools.partial(_lru_kernel, reverse=reverse),
            grid=(B, 1 + n_lat),
            in_specs=[chunk,
                      pl.BlockSpec((SUBLANES, w), lambda b, s: (jnp.maximum(blk(b, s) * halo - 1, 0), c_x)),
                      pl.BlockSpec((SUBLANES, w), lambda b, s: (jnp.minimum((blk(b, s) + 1) * halo, T // SUBLANES - 1), c_x)),
                      full((LRU_CONV, w)), full((1, w)), full((LRU_BLOCKS, LRU_BW, LRU_BW)), full((1, w)),
                      full((LRU_BLOCKS, LRU_BW, LRU_BW)), full((1, w)), full((1, w))] + extra_specs(blk),
            out_specs=pl.BlockSpec((tc, w), lambda b, s: (blk(b, s), 0)),
            out_shape=jax.ShapeDtypeStruct((T, w), F32),
            scratch_shapes=[pltpu.VMEM((1, w), F32)],
            compiler_params=_cparams(("arbitrary", "arbitrary")),
            name="rglru_bwd" if reverse else "rglru_fwd",
        )(p, p, p, conv_w, conv_b[None, :], wa[d], ba[d][None, :], wx[d], bx[d][None, :], sp[d][None, :], *extra_in)

    h_fwd = call(False, [], lambda blk: [])
    return call(True, [h_fwd, p], lambda blk: [pl.BlockSpec((tc, w), lambda b, s: (blk(b, s), 0)),
                                               pl.BlockSpec((tc, w), lambda b, s: (blk(b, s), c_g))])


def _merge_kernel(ya_ref, yb_ref, yc_ref, yd_ref, ga_ref, gb_ref, gc_ref, gd_ref, bias_ref, bw_ref, wo_ref,
                  x_ref, g1_ref, o_ref, acc_ref):
    n = pl.program_id(1)
    z = None
    for b, (y_ref, gate_ref) in enumerate(((ya_ref, ga_ref), (yb_ref, gb_ref), (yc_ref, gc_ref), (yd_ref, gd_ref))):
        t = jnp.dot(y_ref[...].astype(BF16), bw_ref[b], preferred_element_type=F32)
        t = jax.nn.sigmoid(gate_ref[...] + bias_ref[b:b + 1, :]) * t
        z = t if z is None else z + t
    part = jnp.dot(z.astype(BF16), wo_ref[...], preferred_element_type=F32)

    @pl.when(n == 0)
    def _():
        acc_ref[...] = part

    @pl.when(n > 0)
    def _():
        acc_ref[...] += part

    @pl.when(n == pl.num_programs(1) - 1)
    def _():
        o_ref[...] = x_ref[...] + g1_ref[0] * acc_ref[...]


def merge_branches(ys, p, bias, bw, wo, x, g1, ctx_tiles, tiles_per_batch):
    T, D = x.shape
    ct = MERGE_COL_TILE
    seg = functools.partial(_seg_of_tile, ctx_tiles=ctx_tiles, tiles_per_batch=tiles_per_batch)
    y_spec = pl.BlockSpec((ROW_TILE, BRANCH_WIDTH), lambda i, n: (i, 0))

    def gate_spec(b):
        base = (GATE_OFF + b * D) // ct
        return pl.BlockSpec((ROW_TILE, ct), lambda i, n: (i, base + n))

    return pl.pallas_call(
        _merge_kernel,
        grid=(T // ROW_TILE, D // ct),
        in_specs=[y_spec, y_spec, y_spec, y_spec,
                  gate_spec(0), gate_spec(1), gate_spec(2), gate_spec(3),
                  pl.BlockSpec((N_BRANCH, ct), lambda i, n: (0, n)),
                  pl.BlockSpec((N_BRANCH, BRANCH_WIDTH, ct), lambda i, n: (0, 0, n)),
                  pl.BlockSpec((ct, D), lambda i, n: (n, 0)),
                  pl.BlockSpec((ROW_TILE, D), lambda i, n: (i, 0)),
                  pl.BlockSpec((1, 1, D), lambda i, n: (seg(i), 0, 0))],
        out_specs=pl.BlockSpec((ROW_TILE, D), lambda i, n: (i, 0)),
        out_shape=jax.ShapeDtypeStruct((T, D), F32),
        scratch_shapes=[pltpu.VMEM((ROW_TILE, D), F32)],
        compiler_params=_cparams(("arbitrary", "arbitrary")),
        name="merge_branches",
    )(*ys, p, p, p, p, bias, bw, wo, x, g1)


def _route_kernel(lg_ref, rb_ref, idx_ref, w_ref, rank_ref, cnt_ref, run_scr):
    @pl.when(pl.program_id(0) == 0)
    def _():
        run_scr[...] = jnp.zeros_like(run_scr)

    tm = lg_ref.shape[0]
    E = N_EXPERTS
    scores = jax.nn.sigmoid(lg_ref[...].T)
    biased = scores + rb_ref[...]
    ninf = -jnp.inf

    gscore = []
    for g in range(N_GROUPS):
        blk = biased[g * GROUP_SIZE:(g + 1) * GROUP_SIZE, :]
        m1 = jnp.max(blk, axis=0, keepdims=True)
        n_top = jnp.sum(jnp.where(blk == m1, 1.0, 0.0), axis=0, keepdims=True)
        m2 = jnp.max(jnp.where(blk < m1, blk, ninf), axis=0, keepdims=True)
        gscore.append(m1 + jnp.where(n_top >= 2.0, m1, m2))
    parts = []
    for g in range(N_GROUPS):
        beaten = jnp.zeros_like(gscore[g])
        for o in range(N_GROUPS):
            if o != g:
                wins = (gscore[o] >= gscore[g]) if o < g else (gscore[o] > gscore[g])
                beaten = beaten + jnp.where(wins, 1.0, 0.0)
        keep = beaten < float(TOPK_GROUPS)
        parts.append(jnp.where(keep, biased[g * GROUP_SIZE:(g + 1) * GROUP_SIZE, :], ninf))
    cur = jnp.concatenate(parts, axis=0)

    eidx = lax.broadcasted_iota(jnp.int32, (E, tm), 0).astype(F32)
    picks, wts = [], []
    sel = jnp.zeros((E, tm), F32)
    for _ in range(TOP_K):
        m = jnp.max(cur, axis=0, keepdims=True)
        ik = jnp.min(jnp.where(cur == m, eidx, float(E)), axis=0, keepdims=True)
        hit = eidx == ik
        picks.append(ik)
        wts.append(jnp.sum(jnp.where(hit, scores, 0.0), axis=0, keepdims=True))
        sel = sel + jnp.where(hit, 1.0, 0.0)
        cur = jnp.where(hit, ninf, cur)
    total = wts[0]
    for k in range(1, TOP_K):
        total = total + wts[k]

    before = lax.broadcasted_iota(jnp.int32, (tm, tm), 0) < lax.broadcasted_iota(jnp.int32, (tm, tm), 1)
    prefix = jnp.dot(sel.astype(BF16), jnp.where(before, 1.0, 0.0).astype(BF16), preferred_element_type=F32)
    base = prefix + run_scr[:, 0:1]
    for k in range(TOP_K):
        idx_ref[k:k + 1, :] = picks[k].astype(jnp.int32)
        w_ref[k:k + 1, :] = wts[k] / total * ROUTE_SCALE
        rank_ref[k:k + 1, :] = jnp.sum(jnp.where(eidx == picks[k], base, 0.0), axis=0, keepdims=True).astype(jnp.int32)
    run_scr[...] = run_scr[...] + jnp.sum(sel, axis=1, keepdims=True)
    cnt_ref[...] = run_scr[...]


def route_tokens(logits, router_b):
    T, E = logits.shape
    tm = ROUTE_TILE
    per_k = pl.BlockSpec((TOP_K, tm), lambda i: (0, i))
    idx, wts, rank, cnt = pl.pallas_call(
        _route_kernel,
        grid=(T // tm,),
        in_specs=[pl.BlockSpec((tm, E), lambda i: (i, 0)), pl.BlockSpec((E, 1), lambda i: (0, 0))],
        out_specs=[per_k, per_k, per_k, pl.BlockSpec((E, LANES), lambda i: (0, 0))],
        out_shape=[jax.ShapeDtypeStruct((TOP_K, T), jnp.int32), jax.ShapeDtypeStruct((TOP_K, T), F32),
                   jax.ShapeDtypeStruct((TOP_K, T), jnp.int32), jax.ShapeDtypeStruct((E, LANES), F32)],
        scratch_shapes=[pltpu.VMEM((E, LANES), F32)],
        compiler_params=_cparams(("arbitrary",)),
        name="route_tokens",
    )(logits, router_b[:, None])
    return idx, wts, rank, cnt[:, 0].astype(jnp.int32)


def slot_layout(idx, rank, counts, T):
    E, tb = N_EXPERTS, MOE_ROW_TILE
    padded = (counts + tb - 1) // tb * tb
    pend = jnp.cumsum(padded)
    slot_of = (pend - padded)[idx] + rank
    n_blocks = -(-T * TOP_K // tb) + E
    tok = jnp.broadcast_to(jnp.arange(T, dtype=jnp.int32), (TOP_K, T))
    slot_tok = jnp.full((n_blocks * tb,), T, jnp.int32).at[slot_of.reshape(-1)].set(tok.reshape(-1))
    block_exp = jnp.minimum(jnp.searchsorted(pend, jnp.arange(n_blocks) * tb, side='right'), E - 1).astype(jnp.int32)
    n_used = (pend[-1] // tb).astype(jnp.int32).reshape(1)
    return slot_of, slot_tok, block_exp, n_used


def _expert_kernel(be_ref, nb_ref, xs_ref, wg_ref, wu_ref, wd_ref, o_ref, wg_s, wu_s, wd_s):
    i = pl.program_id(0)
    live = i < nb_ref[0]

    @pl.when(live & ((i == 0) | (be_ref[i] != be_ref[jnp.maximum(i - 1, 0)])))
    def _():
        wg_s[...] = wg_ref[0].astype(BF16)
        wu_s[...] = wu_ref[0].astype(BF16)
        wd_s[...] = wd_ref[0].astype(BF16)

    @pl.when(live)
    def _():
        xb = xs_ref[...]
        a = jax.nn.silu(jnp.dot(xb, wg_s[...], preferred_element_type=F32))
        a = a * jnp.dot(xb, wu_s[...], preferred_element_type=F32)
        o_ref[...] = jnp.dot(a.astype(BF16), wd_s[...], preferred_element_type=F32).astype(o_ref.dtype)

    @pl.when(i >= nb_ref[0])
    def _():
        o_ref[...] = jnp.zeros_like(o_ref)


def routed_experts(block_exp, n_used, xs, wg, wu, wd):
    cap, D = xs.shape
    H = wg.shape[-1]
    tb = MOE_ROW_TILE

    def row(i, be, nb):
        return (jnp.minimum(i, nb[0] - 1), 0)

    def wsel(i, be, nb):
        return (be[jnp.minimum(i, nb[0] - 1)], 0, 0)

    grid_spec = pltpu.PrefetchScalarGridSpec(
        num_scalar_prefetch=2,
        grid=(cap // tb,),
        in_specs=[pl.BlockSpec((tb, D), row),
                  pl.BlockSpec((1, D, H), wsel),
                  pl.BlockSpec((1, D, H), wsel),
                  pl.BlockSpec((1, H, D), wsel)],
        out_specs=pl.BlockSpec((tb, D), lambda i, be, nb: (i, 0)),
        scratch_shapes=[pltpu.VMEM((D, H), BF16), pltpu.VMEM((D, H), BF16), pltpu.VMEM((H, D), BF16)],
    )
    return pl.pallas_call(
        _expert_kernel,
        grid_spec=grid_spec,
        out_shape=jax.ShapeDtypeStruct((cap, D), BF16),
        compiler_params=_cparams(("arbitrary",)),
        name="routed_experts",
    )(block_exp, n_used, xs, wg, wu, wd)


def _ffn_out_kernel(h_ref, r_ref, x_ref, g2_ref, sg_ref, su_ref, sd_ref, o_ref):
    h = h_ref[...]
    a = jax.nn.silu(jnp.dot(h, sg_ref[...], preferred_element_type=F32))
    a = a * jnp.dot(h, su_ref[...], preferred_element_type=F32)
    shared = jnp.dot(a.astype(BF16), sd_ref[...], preferred_element_type=F32)
    o_ref[...] = x_ref[...] + g2_ref[0] * (r_ref[...] + shared)


def ffn_out(h, routed, x, g2, sg, su, sd, ctx_tiles, tiles_per_batch):
    T, D = x.shape
    H = sg.shape[-1]
    seg = functools.partial(_seg_of_tile, ctx_tiles=ctx_tiles, tiles_per_batch=tiles_per_batch)
    row = pl.BlockSpec((ROW_TILE, D), lambda i: (i, 0))
    return pl.pallas_call(
        _ffn_out_kernel,
        grid=(T // ROW_TILE,),
        in_specs=[row, row, row,
                  pl.BlockSpec((1, 1, D), lambda i: (seg(i), 0, 0)),
                  pl.BlockSpec((D, H), lambda i: (0, 0)),
                  pl.BlockSpec((D, H), lambda i: (0, 0)),
                  pl.BlockSpec((H, D), lambda i: (0, 0))],
        out_specs=row,
        out_shape=jax.ShapeDtypeStruct((T, D), F32),
        compiler_params=_cparams(("arbitrary",)),
        name="ffn_out",
    )(h, routed, x, g2, sg, su, sd)


def rms_norm(x, g):
    xf = x.astype(F32)
    y = xf * lax.rsqrt(jnp.mean(xf * xf, axis=-1, keepdims=True) + NORM_EPS)
    return (y * g.astype(F32)).astype(x.dtype)


def centred_dwconv(x, w, b):
    K, L = w.shape[0], x.shape[1]
    lo = (K - 1) // 2
    xp = jnp.pad(x, ((0, 0), (lo, K - 1 - lo), (0, 0)))
    y = b
    for j in range(K):
        y = y + xp[:, j:j + L] * w[j]
    return y


def to_heads(a, n_heads):
    B, L, W = a.shape
    return a.reshape(B, L, n_heads, W // n_heads).transpose(0, 2, 1, 3)


def from_heads(a):
    B, H, L, Dh = a.shape
    return a.transpose(0, 2, 1, 3).reshape(B, L, H * Dh)


def rope_2d(x, rows, cols):
    dh = x.shape[-1]
    half, quarter = dh // 2, dh // 4
    inv = ROPE_BASE ** (-jnp.arange(quarter, dtype=F32) / quarter)

    def rot(xa, pos):
        ang = pos.astype(F32)[:, None] * inv
        cos, sin = jnp.cos(ang).astype(xa.dtype), jnp.sin(ang).astype(xa.dtype)
        x1, x2 = xa[..., :quarter], xa[..., quarter:]
        return jnp.concatenate([x1 * cos - x2 * sin, x1 * sin + x2 * cos], axis=-1)
    return jnp.concatenate([rot(x[..., :half], rows), rot(x[..., half:], cols)], axis=-1)


def mlstm_scan(q, k, v, ig, fg, state):
    B, H, L, Dh = q.shape
    nc = L // ML_CHUNK

    def chunks(a):
        return jnp.moveaxis(a.reshape(B, H, nc, ML_CHUNK, *a.shape[3:]), 2, 0)
    lower = jnp.tril(jnp.ones((ML_CHUNK, ML_CHUNK), dtype=bool))

    def step(carry, xs):
        C, n, m = carry
        qc, kc, vc, ic, fc = xs
        b = jnp.cumsum(jax.nn.log_sigmoid(fc), axis=-1)
        d = jnp.where(lower, b[..., :, None] - b[..., None, :] + ic[..., None, :], -jnp.inf)
        m_t = jnp.maximum(b + m[..., None], jnp.max(d, axis=-1))
        w = jnp.exp(d - m_t[..., None]) * jnp.einsum('bhtd,bhsd->bhts', qc, kc)
        carry_w = jnp.exp(b + m[..., None] - m_t)
        num = jnp.einsum('bhts,bhsd->bhtd', w, vc) + carry_w[..., None] * jnp.einsum('bhtd,bhde->bhte', qc, C)
        den = jnp.sum(w, axis=-1) + carry_w * jnp.einsum('bhtd,bhd->bht', qc, n)
        h = num / jnp.maximum(jnp.abs(den), jnp.exp(-m_t))[..., None]
        g = b[..., -1:] - b + ic
        m_new = jnp.maximum(b[..., -1] + m, jnp.max(g, axis=-1))
        keep = jnp.exp(b[..., -1] + m - m_new)
        wg = jnp.exp(g - m_new[..., None])
        C = keep[..., None, None] * C + jnp.einsum('bhs,bhsd,bhse->bhde', wg, kc, vc)
        n = keep[..., None] * n + jnp.einsum('bhs,bhsd->bhd', wg, kc)
        return (C, n, m_new), h
    state, hs = lax.scan(step, state, (chunks(q), chunks(k), chunks(v), chunks(ig), chunks(fg)))
    return jnp.moveaxis(hs, 0, 2).reshape(B, H, L, Dh), state


def mlstm_branch(pc_qkv, pl_qkv, pc_o, pl_o, pc_if, pl_if, conv_w, conv_b, gate_b, out_g, rows, cols, need_ctx):
    def prep(p_qkv, p_if, rotary):
        q, k, v = jnp.split(p_qkv, 3, axis=-1)
        qk = jax.nn.silu(centred_dwconv(jnp.concatenate([q, k], axis=-1), conv_w, conv_b))
        q, k = jnp.split(qk, 2, axis=-1)
        q = to_heads(q, ML_HEADS).astype(F32)
        k = to_heads(k, ML_HEADS).astype(F32)
        v = to_heads(v, ML_HEADS).astype(F32)
        if rotary:
            q, k = rope_2d(q, rows, cols), rope_2d(k, rows, cols)
        k = k * (HEAD_DIM ** -0.5)
        B, L, _ = p_if.shape
        g = p_if.astype(F32).reshape(B, L, 2, 2, ML_HEADS) + gate_b
        return q, k, v, g.transpose(2, 3, 0, 4, 1)
    qc, kc, vc, gc = prep(pc_qkv, pc_if, False)
    ql, kl, vl, gl = prep(pl_qkv, pl_if, True)
    B = ql.shape[0]
    zero = (jnp.zeros((B, ML_HEADS, HEAD_DIM, HEAD_DIM), F32),
            jnp.zeros((B, ML_HEADS, HEAD_DIM), F32), jnp.zeros((B, ML_HEADS), F32))
    flip = lambda a: jnp.flip(a, axis=2)
    hc_f, st_f = mlstm_scan(qc, kc, vc, gc[0, 0], gc[0, 1], zero)
    hl_f, _ = mlstm_scan(ql, kl, vl, gl[0, 0], gl[0, 1], st_f)
    hc_b, st_b = mlstm_scan(flip(qc), flip(kc), flip(vc), flip(gc[1, 0]), flip(gc[1, 1]), zero)
    hl_b, _ = mlstm_scan(flip(ql), flip(kl), flip(vl), flip(gl[1, 0]), flip(gl[1, 1]), st_b)

    def readout(h, o):
        h = from_heads(rms_norm(h, out_g.reshape(ML_HEADS, 1, HEAD_DIM)))
        return (h * jax.nn.sigmoid(o.astype(F32))).astype(o.dtype)
    yl = readout(hl_f + flip(hl_b), pl_o)
    yc = readout(hc_f + flip(hc_b), pc_o) if need_ctx else None
    return yc, yl


def hyena_filters(L, w1, b1, w2, b2, w3, freq, decay):
    n = jnp.arange(L, dtype=F32)
    t = n / max(L - 1, 1)
    bands = (HY_EMB - 1) // 2
    f = jnp.linspace(1e-4, bands - 1, bands, dtype=F32)
    ang = (2.0 * math.pi / L) * n[:, None] * f
    z = jnp.concatenate([t[:, None], jnp.cos(ang), -jnp.sin(ang)], axis=-1)
    a = jnp.sin(freq * (z @ w1 + b1))
    a = jnp.sin(freq * (a @ w2 + b2))
    h = ((a @ w3) * jnp.exp(-t[:, None] * jnp.abs(decay))).astype(F32)
    h = h.reshape(L, HY_ORDER, 2, HY_WIDTH)
    l1 = jnp.sum(jnp.abs(h[:, :, 0]), axis=0) + jnp.sum(jnp.abs(h[1:, :, 1]), axis=0)
    return h / l1[None, :, None]


def long_conv(u, hf, hb, d):
    B, L, W = u.shape
    taps = jnp.concatenate([hf, jnp.zeros((1, W), hf.dtype), hb[:0:-1]], axis=0)
    y = jnp.fft.irfft(jnp.fft.rfft(u, n=2 * L, axis=1) * jnp.fft.rfft(taps, axis=0)[None], n=2 * L, axis=1)[:, :L]
    return y + u * d


def hyena_branch(pc, pl_, conv_w, conv_b, w1, b1, w2, b2, w3, freq, decay, dskip, need_ctx):
    def run(p):
        L = p.shape[1]
        u = centred_dwconv(p, conv_w, conv_b).astype(F32)
        parts = jnp.split(u, HY_ORDER + 1, axis=-1)
        filt = hyena_filters(L, w1, b1, w2, b2, w3, freq, decay)
        z = parts[0]
        for o in range(HY_ORDER):
            z = parts[o + 1] * long_conv(z, filt[:, o, 0], filt[:, o, 1], dskip[o])
        return z.astype(p.dtype)
    yl = run(pl_)
    yc = run(pc) if need_ctx else None
    return yc, yl


def kernel(x, c, ctx, c_ctx, norm1_g, norm2_g, ada_w, ada_b, w_in, na_qnorm_g, na_knorm_g, na_rpb,
           ml_conv_w, ml_conv_b, ml_gate_b, ml_out_g, lru_conv_w, lru_conv_b, lru_wa, lru_ba, lru_wx, lru_bx,
           lru_lambda, hy_conv_w, hy_conv_b, hy_w1, hy_b1, hy_w2, hy_b2, hy_w3, hy_freq, hy_decay, hy_dskip,
           br_w, br_gate_b, w_o, router_w, router_b, exp_w_gate, exp_w_up, exp_w_down,
           sh_w_gate, sh_w_up, sh_w_down):
    B, S, D = x.shape
    assert ctx.shape[1] == CTX_LEN and D == D_MODEL
    n_ctx = B * CTX_LEN
    T = n_ctx + B * S
    assert n_ctx % ROW_TILE == 0 and S % ROW_TILE == 0 and T % ROUTE_TILE == 0
    tiles = dict(ctx_tiles=n_ctx // ROW_TILE, tiles_per_batch=S // ROW_TILE)
    pos = jnp.arange(S)
    rows, cols = pos // GRID_W, pos % GRID_W

    xa = jnp.concatenate([ctx.reshape(n_ctx, D), x.reshape(B * S, D)], axis=0)
    for l in range(DEPTH):
        need_ctx = l < DEPTH - 1
        cond = jnp.concatenate([c_ctx[None, :], c], axis=0)
        mod = (jax.nn.silu(cond) @ ada_w[l] + ada_b[l]).reshape(1 + B, 6, D)
        sh1sc1, g1, sh2sc2, g2 = mod[:, 0:2], mod[:, 2:3], mod[:, 3:5], mod[:, 5:6]

        w_l = w_in[l]
        w_main = jnp.concatenate([w_l[:, :IF_START], w_l[:, IF_START + IF_WIDTH:]], axis=1).astype(BF16)
        w_if = jnp.pad(w_l[:, IF_START:IF_START + IF_WIDTH], ((0, 0), (0, LANES - IF_WIDTH)))
        p = norm_mod_project(xa, norm1_g[l][None, :], sh1sc1, w_main, **tiles)
        p_if, _ = norm_mod_narrow(xa, norm1_g[l][None, :], sh1sc1, w_if, **tiles)
        p_if = p_if[:, :IF_WIDTH]

        def part(k, lat):
            a = p[:, MAIN_OFF[k]:MAIN_OFF[k + 1]]
            return a[n_ctx:].reshape(B, S, -1) if lat else a[:n_ctx].reshape(B, CTX_LEN, -1)
        pc_if, pl_if = p_if[:n_ctx].reshape(B, CTX_LEN, -1), p_if[n_ctx:].reshape(B, S, -1)

        def rows_of(y):
            y_c, y_l = y
            y_c = jnp.zeros((n_ctx, BRANCH_WIDTH), F32) if y_c is None else y_c.reshape(n_ctx, BRANCH_WIDTH)
            return jnp.concatenate([y_c, y_l.reshape(B * S, BRANCH_WIDTH)], axis=0)

        ya = neighbourhood_attention(p, na_rpb[l], na_qnorm_g[l], na_knorm_g[l], B, S)
        yb = rows_of(mlstm_branch(part(1, False), part(1, True), part(2, False), part(2, True), pc_if, pl_if,
                                  ml_conv_w[l], ml_conv_b[l], ml_gate_b[l], ml_out_g[l], rows, cols, need_ctx))
        yc = rglru(p, lru_conv_w[l], lru_conv_b[l], lru_wa[l], lru_ba[l], lru_wx[l], lru_bx[l], lru_lambda[l], B, S)
        yd = rows_of(hyena_branch(part(5, False), part(5, True), hy_conv_w[l], hy_conv_b[l], hy_w1[l], hy_b1[l],
                                  hy_w2[l], hy_b2[l], hy_w3[l], hy_freq[l], hy_decay[l], hy_dskip[l], need_ctx))
        xa = merge_branches([ya, yb, yc, yd], p, br_gate_b[l], br_w[l].astype(BF16), w_o[l].astype(BF16), xa, g1,
                            **tiles)

        logits, h2 = norm_mod_narrow(xa, norm2_g[l][None, :], sh2sc2, router_w[l], **tiles)
        idx, wts, rank, counts = route_tokens(logits, router_b[l])
        slot_of, slot_tok, block_exp, n_used = slot_layout(idx, rank, counts, T)
        h2_pad = jnp.concatenate([h2, jnp.zeros((1, D), BF16)], axis=0)
        ys = routed_experts(block_exp, n_used, h2_pad[slot_tok], exp_w_gate[l], exp_w_up[l], exp_w_down[l])
        routed = jnp.sum(ys[slot_of].astype(F32) * wts[:, :, None], axis=0)
        xa = ffn_out(h2, routed, xa, g2, sh_w_gate[l].astype(BF16), sh_w_up[l].astype(BF16),
                     sh_w_down[l].astype(BF16), **tiles)
    return xa[n_ctx:].reshape(B, S, D)
```

```python
import functools
import math

import numpy as np
import jax
import jax.numpy as jnp
from jax import lax
from jax.experimental import pallas as pl
from jax.experimental.pallas import tpu as pltpu

D_MODEL = 2048
DEPTH = 2
CTX_LEN = 256
GRID_W = 64
NORM_EPS = 1e-6
N_BRANCH = 4
BRANCH_WIDTH = D_MODEL // 4
HEAD_DIM = 128

NA_HEADS = BRANCH_WIDTH // HEAD_DIM
NA_WIN_ROWS = 8
NA_WIN_COLS = 16

ML_HEADS = BRANCH_WIDTH // HEAD_DIM
ML_WIDTH = BRANCH_WIDTH
ML_CHUNK = 128
ROPE_BASE = 10000.0

LRU_WIDTH = BRANCH_WIDTH
LRU_BLOCKS = 4
LRU_BW = LRU_WIDTH // LRU_BLOCKS
LRU_C = 8.0
LRU_CONV = 4

HY_WIDTH = BRANCH_WIDTH
HY_ORDER = 2
HY_EMB = 33
HY_N2 = 256
HY_K1_TILE = 2
HY_COL_TILE = 8192

N_EXPERTS = 128
TOP_K = 8
N_GROUPS = 8
GROUP_SIZE = N_EXPERTS // N_GROUPS
TOPK_GROUPS = 4
ROUTE_SCALE = 2.5

IN_SPLITS = (3 * BRANCH_WIDTH, 3 * ML_WIDTH, ML_WIDTH, 4 * ML_HEADS, LRU_WIDTH, LRU_WIDTH,
             (HY_ORDER + 1) * HY_WIDTH, N_BRANCH * D_MODEL)
IF_START = sum(IN_SPLITS[:3])
IF_WIDTH = IN_SPLITS[3]
MAIN_SPLITS = IN_SPLITS[:3] + IN_SPLITS[4:]
MAIN_OFF = np.concatenate([[0], np.cumsum(MAIN_SPLITS)]).tolist()
MAIN_WIDTH = MAIN_OFF[-1]
NA_OFF, LRU_X_OFF, LRU_G_OFF, HY_OFF, GATE_OFF = MAIN_OFF[0], MAIN_OFF[3], MAIN_OFF[4], MAIN_OFF[5], MAIN_OFF[6]

LANES = 128
SUBLANES = 8
ROW_TILE = 512
PROJ_COL_TILE = 1024
MERGE_COL_TILE = 512
MOE_ROW_TILE = 256
ROUTE_TILE = 512
SEQ_CHUNK = CTX_LEN
NA_G = 4
NEG = -1e30
VMEM_LIMIT = 56 * 1024 * 1024

F32 = jnp.float32
BF16 = jnp.bfloat16


def _cparams(sem):
    return pltpu.CompilerParams(dimension_semantics=sem, vmem_limit_bytes=VMEM_LIMIT)


def _seg_of_tile(i, ctx_tiles, tiles_per_batch):
    return jnp.where(i < ctx_tiles, 0, 1 + (i - ctx_tiles) // tiles_per_batch)


def _norm_mod(x, g, shift, scale):
    ms = jnp.mean(x * x, axis=-1, keepdims=True)
    return (x * lax.rsqrt(ms + NORM_EPS) * g) * (1.0 + scale) + shift


def _head_norm(x, g):
    return x * lax.rsqrt(jnp.mean(x * x, axis=-1, keepdims=True) + NORM_EPS) * g


def _proj_kernel(x_ref, g_ref, mod_ref, w_ref, o_ref, h_scr):
    @pl.when(pl.program_id(1) == 0)
    def _():
        h = _norm_mod(x_ref[...], g_ref[...], mod_ref[0, 0:1, :], mod_ref[0, 1:2, :])
        h_scr[...] = h.astype(BF16)
    o_ref[...] = jnp.dot(h_scr[...], w_ref[...], preferred_element_type=F32)


def norm_mod_project(x, g, mod, w, ctx_tiles, tiles_per_batch):
    T, D = x.shape
    N = w.shape[1]
    seg = functools.partial(_seg_of_tile, ctx_tiles=ctx_tiles, tiles_per_batch=tiles_per_batch)
    return pl.pallas_call(
        _proj_kernel,
        grid=(T // ROW_TILE, N // PROJ_COL_TILE),
        in_specs=[
            pl.BlockSpec((ROW_TILE, D), lambda i, j: (i, 0)),
            pl.BlockSpec((1, D), lambda i, j: (0, 0)),
            pl.BlockSpec((1, 2, D), lambda i, j: (seg(i), 0, 0)),
            pl.BlockSpec((D, PROJ_COL_TILE), lambda i, j: (0, j)),
        ],
        out_specs=pl.BlockSpec((ROW_TILE, PROJ_COL_TILE), lambda i, j: (i, j)),
        out_shape=jax.ShapeDtypeStruct((T, N), F32),
        scratch_shapes=[pltpu.VMEM((ROW_TILE, D), BF16)],
        compiler_params=_cparams(("arbitrary", "arbitrary")),
        name="norm_mod_project",
    )(x, g, mod, w)


def _narrow_kernel(x_ref, g_ref, mod_ref, w_ref, o_ref, h_ref):
    h = _norm_mod(x_ref[...], g_ref[...], mod_ref[0, 0:1, :], mod_ref[0, 1:2, :])
    h_ref[...] = h.astype(BF16)
    o_ref[...] = jnp.dot(h, w_ref[...], preferred_element_type=F32, precision=lax.Precision.HIGHEST)


def norm_mod_narrow(x, g, mod, w, ctx_tiles, tiles_per_batch):
    T, D = x.shape
    seg = functools.partial(_seg_of_tile, ctx_tiles=ctx_tiles, tiles_per_batch=tiles_per_batch)
    return pl.pallas_call(
        _narrow_kernel,
        grid=(T // ROW_TILE,),
        in_specs=[
            pl.BlockSpec((ROW_TILE, D), lambda i: (i, 0)),
            pl.BlockSpec((1, D), lambda i: (0, 0)),
            pl.BlockSpec((1, 2, D), lambda i: (seg(i), 0, 0)),
            pl.BlockSpec((D, LANES), lambda i: (0, 0)),
        ],
        out_specs=[pl.BlockSpec((ROW_TILE, LANES), lambda i: (i, 0)),
                   pl.BlockSpec((ROW_TILE, D), lambda i: (i, 0))],
        out_shape=[jax.ShapeDtypeStruct((T, LANES), F32), jax.ShapeDtypeStruct((T, D), BF16)],
        compiler_params=_cparams(("arbitrary",)),
        name="norm_mod_narrow",
    )(x, g, mod, w)


def _na_kernel(q_ref, k0_ref, k1_ref, k2_ref, v0_ref, v1_ref, v2_ref, kc_ref, vc_ref, bias_ref, qg_ref, kg_ref,
               o_ref, *, n_rows):
    s = pl.program_id(1)
    qb, kw = bias_ref.shape[1], bias_ref.shape[2]
    first_row = (s - 1) * NA_G
    q_row = first_row + lax.broadcasted_iota(jnp.int32, (qb, kw), 0) // GRID_W
    k_row = first_row - NA_G + lax.broadcasted_iota(jnp.int32, (qb, kw), 1) // GRID_W
    r0 = jnp.clip(q_row - NA_WIN_ROWS // 2, 0, n_rows - NA_WIN_ROWS)
    valid = (k_row >= r0) & (k_row < r0 + NA_WIN_ROWS) & (s > 0)
    nt = (((1,), (1,)), ((), ()))
    for h in range(NA_HEADS):
        hs = slice(h * HEAD_DIM, (h + 1) * HEAD_DIM)
        qn = (_head_norm(q_ref[:, hs], qg_ref[...]) * (HEAD_DIM ** -0.5)).astype(BF16)
        k_win = jnp.concatenate([k0_ref[:, hs], k1_ref[:, hs], k2_ref[:, hs]], axis=0)
        kn = _head_norm(k_win, kg_ref[...]).astype(BF16)
        s_lat = lax.dot_general(qn, kn, nt, preferred_element_type=F32)
        s_lat = jnp.where(valid, s_lat + bias_ref[h], NEG)
        kcn = _head_norm(kc_ref[:, hs], kg_ref[...]).astype(BF16)
        s_ctx = lax.dot_general(qn, kcn, nt, preferred_element_type=F32)
        m = jnp.maximum(jnp.max(s_lat, axis=-1, keepdims=True), jnp.max(s_ctx, axis=-1, keepdims=True))
        p_lat = jnp.exp(s_lat - m)
        p_ctx = jnp.exp(s_ctx - m)
        denom = jnp.sum(p_lat, axis=-1, keepdims=True) + jnp.sum(p_ctx, axis=-1, keepdims=True)
        v_win = jnp.concatenate([v0_ref[:, hs], v1_ref[:, hs], v2_ref[:, hs]], axis=0).astype(BF16)
        o = jnp.dot(p_lat.astype(BF16), v_win, preferred_element_type=F32)
        o = o + jnp.dot(p_ctx.astype(BF16), vc_ref[:, hs].astype(BF16), preferred_element_type=F32)
        o_ref[:, hs] = o / denom


def na_bias_table(rpb):
    qb, kw = NA_G * GRID_W, 3 * NA_G * GRID_W
    qg, qc = np.arange(qb)[:, None] // GRID_W, np.arange(qb)[:, None] % GRID_W
    kr, kc = np.arange(kw)[None, :] // GRID_W, np.arange(kw)[None, :] % GRID_W
    dr = kr - NA_G - qg
    c0 = np.clip(qc - NA_WIN_COLS // 2, 0, GRID_W - NA_WIN_COLS)
    ok = (kc >= c0) & (kc < c0 + NA_WIN_COLS) & (np.abs(dr) < NA_WIN_ROWS)
    ri = np.clip(dr + NA_WIN_ROWS - 1, 0, 2 * NA_WIN_ROWS - 2)
    ci = np.clip(kc - qc + NA_WIN_COLS - 1, 0, 2 * NA_WIN_COLS - 2)
    return jnp.where(ok[None], rpb[:, ri, ci], NEG)


def neighbourhood_attention(p, rpb, qg, kg, B, S):
    T = p.shape[0]
    qb = NA_G * GRID_W
    assert qb == CTX_LEN and NA_WIN_ROWS // 2 <= NA_G and NA_WIN_ROWS - NA_WIN_ROWS // 2 - 1 <= NA_G
    ctx_blocks, n_blk, n_rows = B * CTX_LEN // qb, S // qb, S // GRID_W
    c_q = NA_OFF // BRANCH_WIDTH
    c_k, c_v = c_q + 1, c_q + 2

    def q_map(b, s):
        return (jnp.where(s == 0, b, ctx_blocks + b * n_blk + s - 1), c_q)

    def win_spec(d, col):
        return pl.BlockSpec((qb, BRANCH_WIDTH),
                            lambda b, s: (ctx_blocks + b * n_blk + jnp.clip(s - 1 + d, 0, n_blk - 1), col))

    bias = na_bias_table(rpb)
    return pl.pallas_call(
        functools.partial(_na_kernel, n_rows=n_rows),
        grid=(B, 1 + n_blk),
        in_specs=[pl.BlockSpec((qb, BRANCH_WIDTH), q_map),
                  win_spec(-1, c_k), win_spec(0, c_k), win_spec(1, c_k),
                  win_spec(-1, c_v), win_spec(0, c_v), win_spec(1, c_v),
                  pl.BlockSpec((qb, BRANCH_WIDTH), lambda b, s: (b, c_k)),
                  pl.BlockSpec((qb, BRANCH_WIDTH), lambda b, s: (b, c_v)),
                  pl.BlockSpec(bias.shape, lambda b, s: (0, 0, 0)),
                  pl.BlockSpec((1, HEAD_DIM), lambda b, s: (0, 0)),
                  pl.BlockSpec((1, HEAD_DIM), lambda b, s: (0, 0))],
        out_specs=pl.BlockSpec((qb, BRANCH_WIDTH), lambda b, s: (q_map(b, s)[0], 0)),
        out_shape=jax.ShapeDtypeStruct((T, BRANCH_WIDTH), F32),
        compiler_params=_cparams(("arbitrary", "arbitrary")),
        name="neighbourhood_attention",
    )(p, p, p, p, p, p, p, p, p, bias, qg[None, :], kg[None, :])


def _chunk_edges(reverse):
    s = pl.program_id(1)
    n_lat = pl.num_programs(1) - 1
    lat_idx = (n_lat - s) if reverse else (s - 1)
    has_prev = jnp.where((s > 0) & (lat_idx > 0), 1.0, 0.0)
    has_next = jnp.where((s > 0) & (lat_idx < n_lat - 1), 1.0, 0.0)
    return has_prev, has_next


def _conv4(x_ref, prev_ref, next_ref, cw_ref, cb_ref, has_prev, has_next):
    x = x_ref[...]
    tc, w = x.shape
    row = lax.broadcasted_iota(jnp.int32, (tc, w), 0)
    before = prev_ref[SUBLANES - 1:SUBLANES, :] * has_prev
    after0 = next_ref[0:1, :] * has_next
    after1 = next_ref[1:2, :] * has_next
    xm1 = jnp.where(row == 0, before, pltpu.roll(x, 1, 0))
    xp1 = jnp.where(row == tc - 1, after0, pltpu.roll(x, tc - 1, 0))
    xp2 = jnp.where(row == tc - 2, after0, jnp.where(row == tc - 1, after1, pltpu.roll(x, tc - 2, 0)))
    return cb_ref[...] + xm1 * cw_ref[0:1, :] + x * cw_ref[1:2, :] + xp1 * cw_ref[2:3, :] + xp2 * cw_ref[3:4, :]


def _seq_specs(reverse, B, S, T, width):
    tc = SEQ_CHUNK
    n_lat, ctx_blocks, halo = S // tc, B, tc // SUBLANES

    def blk(b, s):
        lat = (n_lat - s) if reverse else (s - 1)
        return jnp.where(s == 0, b, ctx_blocks + b * n_lat + lat)

    def chunk(col):
        return pl.BlockSpec((tc, width), lambda b, s: (blk(b, s), col))

    def prev(col):
        return pl.BlockSpec((SUBLANES, width), lambda b, s: (jnp.maximum(blk(b, s) * halo - 1, 0), col))

    def nxt(col):
        return pl.BlockSpec((SUBLANES, width), lambda b, s: (jnp.minimum((blk(b, s) + 1) * halo, T // SUBLANES - 1), col))

    def full(shape):
        return pl.BlockSpec(shape, lambda b, s: (0,) * len(shape))
    return blk, chunk, prev, nxt, full


def _lru_kernel(x_ref, prev_ref, next_ref, cw_ref, cb_ref, wa_ref, ba_ref, wx_ref, bx_ref, sp_ref, *rest, reverse):
    if reverse:
        hf_ref, g_ref, o_ref, carry = rest
    else:
        o_ref, carry = rest
    s = pl.program_id(1)
    tc, w = x_ref.shape

    @pl.when(s == 0)
    def _():
        carry[...] = jnp.zeros_like(carry)

    has_prev, has_next = _chunk_edges(reverse)
    xc = _conv4(x_ref, prev_ref, next_ref, cw_ref, cb_ref, has_prev, has_next)
    row = lax.broadcasted_iota(jnp.int32, (tc, w), 0)

    def block_diag(w_ref):
        return jnp.concatenate(
            [jnp.dot(xc[:, n * LRU_BW:(n + 1) * LRU_BW], w_ref[n], preferred_element_type=F32,
                     precision=lax.Precision.HIGHEST) for n in range(LRU_BLOCKS)], axis=-1)
    r = jax.nn.sigmoid(block_diag(wa_ref) + ba_ref[...])
    i = jax.nn.sigmoid(block_diag(wx_ref) + bx_ref[...])
    log_a = -LRU_C * r * sp_ref[...]
    a = jnp.exp(log_a)
    u = jnp.sqrt(1.0 - jnp.exp(2.0 * log_a)) * (i * xc)

    step = 1
    while step < tc:
        if reverse:
            keep = row < tc - step
            shift = tc - step
        else:
            keep = row >= step
            shift = step
        a_sh = jnp.where(keep, pltpu.roll(a, shift, 0), 1.0)
        u_sh = jnp.where(keep, pltpu.roll(u, shift, 0), 0.0)
        u = a * u_sh + u
        a = a * a_sh
        step *= 2
    h = u + a * carry[...]
    carry[...] = h[0:1, :] if reverse else h[tc - 1:tc, :]
    if reverse:
        o_ref[...] = (hf_ref[...] + h) * jax.nn.gelu(g_ref[...])
    else:
        o_ref[...] = h


def rglru(p, conv_w, conv_b, wa, ba, wx, bx, lam, B, S):
    T = p.shape[0]
    tc, w = SEQ_CHUNK, LRU_WIDTH
    assert CTX_LEN == tc and S % tc == 0 and LRU_CONV == 4
    c_x, c_g = LRU_X_OFF // w, LRU_G_OFF // w
    sp = jax.nn.softplus(-lam)

    def call(reverse, extra_in, extra_cols):
        blk, chunk, prev, nxt, full = _seq_specs(reverse, B, S, T, w)
        d = 1 if reverse else 0
        return pl.pallas_call(
            functools.partial(_lru_kernel, reverse=reverse),
            grid=(B, 1 + S // tc),
            in_specs=[chunk(c_x), prev(c_x), nxt(c_x),
                      full((LRU_CONV, w)), full((1, w)), full((LRU_BLOCKS, LRU_BW, LRU_BW)), full((1, w)),
                      full((LRU_BLOCKS, LRU_BW, LRU_BW)), full((1, w)), full((1, w))] + [chunk(c) for c in extra_cols],
            out_specs=chunk(0),
            out_shape=jax.ShapeDtypeStruct((T, w), F32),
            scratch_shapes=[pltpu.VMEM((1, w), F32)],
            compiler_params=_cparams(("arbitrary", "arbitrary")),
            name="rglru_bwd" if reverse else "rglru_fwd",
        )(p, p, p, conv_w, conv_b[None, :], wa[d], ba[d][None, :], wx[d], bx[d][None, :], sp[d][None, :], *extra_in)

    h_fwd = call(False, [], [])
    return call(True, [h_fwd, p], [0, c_g])


def _log_sigmoid(x):
    return jnp.minimum(x, 0.0) - jnp.log(1.0 + jnp.exp(-jnp.abs(x)))


def _mlstm_kernel(q_ref, qp_ref, qn_ref, k_ref, kp_ref, kn_ref, v_ref, if_ref, cos_ref, sin_ref, cwq_ref, cbq_ref,
                  cwk_ref, cbk_ref, gb_ref, *rest, reverse):
    if reverse:
        hf_ref, og_ref, outg_ref, o_ref, c_scr, n_scr, m_scr = rest
    else:
        o_ref, c_scr, n_scr, m_scr = rest
    s = pl.program_id(1)
    tc = q_ref.shape[0]
    ck = ML_CHUNK
    d0 = (1 if reverse else 0) * 2 * ML_HEADS

    @pl.when(s == 0)
    def _():
        c_scr[...] = jnp.zeros_like(c_scr)
        n_scr[...] = jnp.zeros_like(n_scr)
        m_scr[...] = jnp.zeros_like(m_scr)

    has_prev, has_next = _chunk_edges(reverse)
    q_all = jax.nn.silu(_conv4(q_ref, qp_ref, qn_ref, cwq_ref, cbq_ref, has_prev, has_next))
    k_all = jax.nn.silu(_conv4(k_ref, kp_ref, kn_ref, cwk_ref, cbk_ref, has_prev, has_next))

    is_lat = s > 0
    cos = jnp.where(is_lat, cos_ref[...], 1.0)
    sin = jnp.where(is_lat, sin_ref[...], 0.0)
    lane = lax.broadcasted_iota(jnp.int32, (tc, HEAD_DIM), 1)
    low_quarter = (lane % (HEAD_DIM // 2)) < (HEAD_DIM // 4)

    def rope(xh):
        swapped = jnp.where(low_quarter, pltpu.roll(xh, HEAD_DIM - HEAD_DIM // 4, 1), pltpu.roll(xh, HEAD_DIM // 4, 1))
        return xh * cos + swapped * sin

    row = lax.broadcasted_iota(jnp.int32, (ck, ck), 0)
    col = lax.broadcasted_iota(jnp.int32, (ck, ck), 1)
    causal = (col >= row) if reverse else (col <= row)
    nt = (((1,), (1,)), ((), ()))
    tn = (((0,), (0,)), ((), ()))

    qs, ks = [], []
    for h in range(ML_HEADS):
        hs = slice(h * HEAD_DIM, (h + 1) * HEAD_DIM)
        qs.append(rope(q_all[:, hs]))
        ks.append(rope(k_all[:, hs]) * (HEAD_DIM ** -0.5))

    subs = range(tc // ck)
    for sub in (reversed(subs) if reverse else subs):
        rs = slice(sub * ck, (sub + 1) * ck)
        gates = if_ref[rs, :] + gb_ref[...]
        b_all = _log_sigmoid(gates)
        step = 1
        while step < ck:
            if reverse:
                b_all = b_all + jnp.where(row < ck - step, pltpu.roll(b_all, ck - step, 0), 0.0)
            else:
                b_all = b_all + jnp.where(row >= step, pltpu.roll(b_all, step, 0), 0.0)
            step *= 2
        b_all_t = b_all.T
        gates_t = gates.T
        last = 0 if reverse else ck - 1
        for h in range(ML_HEADS):
            hs = slice(h * HEAD_DIM, (h + 1) * HEAD_DIM)
            ci, cf = d0 + h, d0 + ML_HEADS + h
            qh, kh, vh = qs[h][rs], ks[h][rs], v_ref[rs, hs]
            b_col, b_row = b_all[:, cf:cf + 1], b_all_t[cf:cf + 1, :]
            i_col, i_row = gates[:, ci:ci + 1], gates_t[ci:ci + 1, :]
            b_last = b_all[last:last + 1, cf:cf + 1]
            m_prev = m_scr[h:h + 1, 0:1]
            d = jnp.where(causal, b_col - b_row + i_row, -jnp.inf)
            m_t = jnp.maximum(b_col + m_prev, jnp.max(d, axis=-1, keepdims=True))
            qb, kb, vb = qh.astype(BF16), kh.astype(BF16), vh.astype(BF16)
            w = jnp.exp(d - m_t) * lax.dot_general(qb, kb, nt, preferred_element_type=F32)
            carry_w = jnp.exp(b_col + m_prev - m_t)
            num = jnp.dot(w.astype(BF16), vb, preferred_element_type=F32)
            num = num + carry_w * jnp.dot(qb, c_scr[h].astype(BF16), preferred_element_type=F32)
            den = jnp.sum(w, axis=-1, keepdims=True) + carry_w * jnp.sum(qh * n_scr[h:h + 1, :], axis=-1, keepdims=True)
            hh = num / jnp.maximum(jnp.abs(den), jnp.exp(-m_t))
            g_col = b_last - b_col + i_col
            m_new = jnp.maximum(b_last + m_prev, jnp.max(g_col, axis=0, keepdims=True))
            keep = jnp.exp(b_last + m_prev - m_new)
            kw = kh * jnp.exp(g_col - m_new)
            c_scr[h] = keep * c_scr[h] + lax.dot_general(kw.astype(BF16), vb, tn, preferred_element_type=F32)
            n_scr[h:h + 1, :] = keep * n_scr[h:h + 1, :] + jnp.sum(kw, axis=0, keepdims=True)
            m_scr[h:h + 1, :] = jnp.broadcast_to(m_new, (1, LANES))
            if reverse:
                hsum = hf_ref[rs, hs] + hh
                o_ref[rs, hs] = _head_norm(hsum, outg_ref[:, hs]) * jax.nn.sigmoid(og_ref[rs, hs])
            else:
                o_ref[rs, hs] = hh


def rope_tables(S):
    quarter = HEAD_DIM // 4
    inv = ROPE_BASE ** (-jnp.arange(quarter, dtype=F32) / quarter)
    pos = jnp.arange(S)
    ang_r = (pos // GRID_W).astype(F32)[:, None] * inv
    ang_c = (pos % GRID_W).astype(F32)[:, None] * inv
    cos = jnp.concatenate([jnp.cos(ang_r), jnp.cos(ang_r), jnp.cos(ang_c), jnp.cos(ang_c)], axis=-1)
    sin = jnp.concatenate([-jnp.sin(ang_r), jnp.sin(ang_r), -jnp.sin(ang_c), jnp.sin(ang_c)], axis=-1)
    return cos, sin


def mlstm(p, p_if, conv_w, conv_b, gate_b, out_g, B, S):
    T = p.shape[0]
    tc, w = SEQ_CHUNK, ML_WIDTH
    assert CTX_LEN == tc and S % tc == 0 and tc % ML_CHUNK == 0 and conv_w.shape[0] == 4
    c_q = MAIN_OFF[1] // w
    c_k, c_v, c_o = c_q + 1, c_q + 2, MAIN_OFF[2] // w
    cos, sin = rope_tables(S)
    gb = jnp.pad(gate_b.reshape(1, -1), ((0, 0), (0, LANES - gate_b.size)))
    n_lat = S // tc

    def call(reverse, extra_in, extra_specs):
        blk, chunk, prev, nxt, full = _seq_specs(reverse, B, S, T, w)

        def lat_rows(b, s):
            return (jnp.clip((n_lat - s) if reverse else (s - 1), 0, n_lat - 1), 0)
        table = pl.BlockSpec((tc, HEAD_DIM), lat_rows)
        return pl.pallas_call(
            functools.partial(_mlstm_kernel, reverse=reverse),
            grid=(B, 1 + n_lat),
            in_specs=[chunk(c_q), prev(c_q), nxt(c_q), chunk(c_k), prev(c_k), nxt(c_k), chunk(c_v),
                      pl.BlockSpec((tc, LANES), lambda b, s: (blk(b, s), 0)), table, table,
                      full((4, w)), full((1, w)), full((4, w)), full((1, w)), full((1, LANES))] + extra_specs(chunk, full),
            out_specs=chunk(0),
            out_shape=jax.ShapeDtypeStruct((T, w), F32),
            scratch_shapes=[pltpu.VMEM((ML_HEADS, HEAD_DIM, HEAD_DIM), F32), pltpu.VMEM((SUBLANES, HEAD_DIM), F32),
                            pltpu.VMEM((SUBLANES, LANES), F32)],
            compiler_params=_cparams(("arbitrary", "arbitrary")),
            name="mlstm_bwd" if reverse else "mlstm_fwd",
        )(p, p, p, p, p, p, p, p_if, cos, sin, conv_w[:, :w], conv_b[None, :w], conv_w[:, w:], conv_b[None, w:], gb,
          *extra_in)

    h_fwd = call(False, [], lambda chunk, full: [])
    return call(True, [h_fwd, p, out_g[None, :]], lambda chunk, full: [chunk(0), chunk(c_o), full((1, w))])


def _merge_kernel(ya_ref, yb_ref, yc_ref, yd_ref, ga_ref, gb_ref, gc_ref, gd_ref, bias_ref, bw_ref, wo_ref,
                  x_ref, g1_ref, o_ref, acc_ref):
    n = pl.program_id(1)
    z = None
    for b, (y_ref, gate_ref) in enumerate(((ya_ref, ga_ref), (yb_ref, gb_ref), (yc_ref, gc_ref), (yd_ref, gd_ref))):
        t = jnp.dot(y_ref[...].astype(BF16), bw_ref[b], preferred_element_type=F32)
        t = jax.nn.sigmoid(gate_ref[...] + bias_ref[b:b + 1, :]) * t
        z = t if z is None else z + t
    part = jnp.dot(z.astype(BF16), wo_ref[...], preferred_element_type=F32)

    @pl.when(n == 0)
    def _():
        acc_ref[...] = part

    @pl.when(n > 0)
    def _():
        acc_ref[...] += part

    @pl.when(n == pl.num_programs(1) - 1)
    def _():
        o_ref[...] = x_ref[...] + g1_ref[0] * acc_ref[...]


def merge_branches(ys, p, bias, bw, wo, x, g1, ctx_tiles, tiles_per_batch):
    T, D = x.shape
    ct = MERGE_COL_TILE
    seg = functools.partial(_seg_of_tile, ctx_tiles=ctx_tiles, tiles_per_batch=tiles_per_batch)
    y_spec = pl.BlockSpec((ROW_TILE, BRANCH_WIDTH), lambda i, n: (i, 0))

    def gate_spec(b):
        base = (GATE_OFF + b * D) // ct
        return pl.BlockSpec((ROW_TILE, ct), lambda i, n: (i, base + n))

    return pl.pallas_call(
        _merge_kernel,
        grid=(T // ROW_TILE, D // ct),
        in_specs=[y_spec, y_spec, y_spec, y_spec,
                  gate_spec(0), gate_spec(1), gate_spec(2), gate_spec(3),
                  pl.BlockSpec((N_BRANCH, ct), lambda i, n: (0, n)),
                  pl.BlockSpec((N_BRANCH, BRANCH_WIDTH, ct), lambda i, n: (0, 0, n)),
                  pl.BlockSpec((ct, D), lambda i, n: (n, 0)),
                  pl.BlockSpec((ROW_TILE, D), lambda i, n: (i, 0)),
                  pl.BlockSpec((1, 1, D), lambda i, n: (seg(i), 0, 0))],
        out_specs=pl.BlockSpec((ROW_TILE, D), lambda i, n: (i, 0)),
        out_shape=jax.ShapeDtypeStruct((T, D), F32),
        scratch_shapes=[pltpu.VMEM((ROW_TILE, D), F32)],
        compiler_params=_cparams(("arbitrary", "arbitrary")),
        name="merge_branches",
    )(*ys, p, p, p, p, bias, bw, wo, x, g1)


def _route_kernel(lg_ref, rb_ref, idx_ref, w_ref, rank_ref, cnt_ref, run_scr):
    @pl.when(pl.program_id(0) == 0)
    def _():
        run_scr[...] = jnp.zeros_like(run_scr)

    tm = lg_ref.shape[0]
    E = N_EXPERTS
    scores = jax.nn.sigmoid(lg_ref[...].T)
    biased = scores + rb_ref[...]
    ninf = -jnp.inf

    gscore = []
    for g in range(N_GROUPS):
        blk = biased[g * GROUP_SIZE:(g + 1) * GROUP_SIZE, :]
        m1 = jnp.max(blk, axis=0, keepdims=True)
        n_top = jnp.sum(jnp.where(blk == m1, 1.0, 0.0), axis=0, keepdims=True)
        m2 = jnp.max(jnp.where(blk < m1, blk, ninf), axis=0, keepdims=True)
        gscore.append(m1 + jnp.where(n_top >= 2.0, m1, m2))
    parts = []
    for g in range(N_GROUPS):
        beaten = jnp.zeros_like(gscore[g])
        for o in range(N_GROUPS):
            if o != g:
                wins = (gscore[o] >= gscore[g]) if o < g else (gscore[o] > gscore[g])
                beaten = beaten + jnp.where(wins, 1.0, 0.0)
        keep = beaten < float(TOPK_GROUPS)
        parts.append(jnp.where(keep, biased[g * GROUP_SIZE:(g + 1) * GROUP_SIZE, :], ninf))
    cur = jnp.concatenate(parts, axis=0)

    eidx = lax.broadcasted_iota(jnp.int32, (E, tm), 0).astype(F32)
    picks, wts = [], []
    sel = jnp.zeros((E, tm), F32)
    for _ in range(TOP_K):
        m = jnp.max(cur, axis=0, keepdims=True)
        ik = jnp.min(jnp.where(cur == m, eidx, float(E)), axis=0, keepdims=True)
        hit = eidx == ik
        picks.append(ik)
        wts.append(jnp.sum(jnp.where(hit, scores, 0.0), axis=0, keepdims=True))
        sel = sel + jnp.where(hit, 1.0, 0.0)
        cur = jnp.where(hit, ninf, cur)
    total = wts[0]
    for k in range(1, TOP_K):
        total = total + wts[k]

    before = lax.broadcasted_iota(jnp.int32, (tm, tm), 0) < lax.broadcasted_iota(jnp.int32, (tm, tm), 1)
    prefix = jnp.dot(sel.astype(BF16), jnp.where(before, 1.0, 0.0).astype(BF16), preferred_element_type=F32)
    base = prefix + run_scr[:, 0:1]
    for k in range(TOP_K):
        idx_ref[k:k + 1, :] = picks[k].astype(jnp.int32)
        w_ref[k:k + 1, :] = wts[k] / total * ROUTE_SCALE
        rank_ref[k:k + 1, :] = jnp.sum(jnp.where(eidx == picks[k], base, 0.0), axis=0, keepdims=True).astype(jnp.int32)
    run_scr[...] = run_scr[...] + jnp.sum(sel, axis=1, keepdims=True)
    cnt_ref[...] = run_scr[...]


def route_tokens(logits, router_b):
    T, E = logits.shape
    tm = ROUTE_TILE
    per_k = pl.BlockSpec((TOP_K, tm), lambda i: (0, i))
    idx, wts, rank, cnt = pl.pallas_call(
        _route_kernel,
        grid=(T // tm,),
        in_specs=[pl.BlockSpec((tm, E), lambda i: (i, 0)), pl.BlockSpec((E, 1), lambda i: (0, 0))],
        out_specs=[per_k, per_k, per_k, pl.BlockSpec((E, LANES), lambda i: (0, 0))],
        out_shape=[jax.ShapeDtypeStruct((TOP_K, T), jnp.int32), jax.ShapeDtypeStruct((TOP_K, T), F32),
                   jax.ShapeDtypeStruct((TOP_K, T), jnp.int32), jax.ShapeDtypeStruct((E, LANES), F32)],
        scratch_shapes=[pltpu.VMEM((E, LANES), F32)],
        compiler_params=_cparams(("arbitrary",)),
        name="route_tokens",
    )(logits, router_b[:, None])
    return idx, wts, rank, cnt[:, 0].astype(jnp.int32)


def slot_layout(idx, rank, counts, T):
    E, tb = N_EXPERTS, MOE_ROW_TILE
    padded = (counts + tb - 1) // tb * tb
    pend = jnp.cumsum(padded)
    slot_of = (pend - padded)[idx] + rank
    n_blocks = -(-T * TOP_K // tb) + E
    tok = jnp.broadcast_to(jnp.arange(T, dtype=jnp.int32), (TOP_K, T))
    slot_tok = jnp.full((n_blocks * tb,), T, jnp.int32).at[slot_of.reshape(-1)].set(tok.reshape(-1))
    block_exp = jnp.minimum(jnp.searchsorted(pend, jnp.arange(n_blocks) * tb, side='right'), E - 1).astype(jnp.int32)
    n_used = (pend[-1] // tb).astype(jnp.int32).reshape(1)
    return slot_of, slot_tok, block_exp, n_used


def _expert_kernel(be_ref, nb_ref, xs_ref, wg_ref, wu_ref, wd_ref, o_ref, wg_s, wu_s, wd_s):
    i = pl.program_id(0)
    live = i < nb_ref[0]

    @pl.when(live & ((i == 0) | (be_ref[i] != be_ref[jnp.maximum(i - 1, 0)])))
    def _():
        wg_s[...] = wg_ref[0].astype(BF16)
        wu_s[...] = wu_ref[0].astype(BF16)
        wd_s[...] = wd_ref[0].astype(BF16)

    @pl.when(live)
    def _():
        xb = xs_ref[...]
        a = jax.nn.silu(jnp.dot(xb, wg_s[...], preferred_element_type=F32))
        a = a * jnp.dot(xb, wu_s[...], preferred_element_type=F32)
        o_ref[...] = jnp.dot(a.astype(BF16), wd_s[...], preferred_element_type=F32).astype(o_ref.dtype)

    @pl.when(i >= nb_ref[0])
    def _():
        o_ref[...] = jnp.zeros_like(o_ref)


def routed_experts(block_exp, n_used, xs, wg, wu, wd):
    cap, D = xs.shape
    H = wg.shape[-1]
    tb = MOE_ROW_TILE

    def row(i, be, nb):
        return (jnp.minimum(i, nb[0] - 1), 0)

    def wsel(i, be, nb):
        return (be[jnp.minimum(i, nb[0] - 1)], 0, 0)

    grid_spec = pltpu.PrefetchScalarGridSpec(
        num_scalar_prefetch=2,
        grid=(cap // tb,),
        in_specs=[pl.BlockSpec((tb, D), row),
                  pl.BlockSpec((1, D, H), wsel),
                  pl.BlockSpec((1, D, H), wsel),
                  pl.BlockSpec((1, H, D), wsel)],
        out_specs=pl.BlockSpec((tb, D), lambda i, be, nb: (i, 0)),
        scratch_shapes=[pltpu.VMEM((D, H), BF16), pltpu.VMEM((D, H), BF16), pltpu.VMEM((H, D), BF16)],
    )
    return pl.pallas_call(
        _expert_kernel,
        grid_spec=grid_spec,
        out_shape=jax.ShapeDtypeStruct((cap, D), BF16),
        compiler_params=_cparams(("arbitrary",)),
        name="routed_experts",
    )(block_exp, n_used, xs, wg, wu, wd)


def _ffn_out_kernel(h_ref, r_ref, x_ref, g2_ref, sg_ref, su_ref, sd_ref, o_ref):
    h = h_ref[...]
    a = jax.nn.silu(jnp.dot(h, sg_ref[...], preferred_element_type=F32))
    a = a * jnp.dot(h, su_ref[...], preferred_element_type=F32)
    shared = jnp.dot(a.astype(BF16), sd_ref[...], preferred_element_type=F32)
    o_ref[...] = x_ref[...] + g2_ref[0] * (r_ref[...] + shared)


def ffn_out(h, routed, x, g2, sg, su, sd, ctx_tiles, tiles_per_batch):
    T, D = x.shape
    H = sg.shape[-1]
    seg = functools.partial(_seg_of_tile, ctx_tiles=ctx_tiles, tiles_per_batch=tiles_per_batch)
    row = pl.BlockSpec((ROW_TILE, D), lambda i: (i, 0))
    return pl.pallas_call(
        _ffn_out_kernel,
        grid=(T // ROW_TILE,),
        in_specs=[row, row, row,
                  pl.BlockSpec((1, 1, D), lambda i: (seg(i), 0, 0)),
                  pl.BlockSpec((D, H), lambda i: (0, 0)),
                  pl.BlockSpec((D, H), lambda i: (0, 0)),
                  pl.BlockSpec((H, D), lambda i: (0, 0))],
        out_specs=row,
        out_shape=jax.ShapeDtypeStruct((T, D), F32),
        compiler_params=_cparams(("arbitrary",)),
        name="ffn_out",
    )(h, routed, x, g2, sg, su, sd)


def rms_norm(x, g):
    xf = x.astype(F32)
    y = xf * lax.rsqrt(jnp.mean(xf * xf, axis=-1, keepdims=True) + NORM_EPS)
    return (y * g.astype(F32)).astype(x.dtype)


def centred_dwconv(x, w, b):
    K, L = w.shape[0], x.shape[1]
    lo = (K - 1) // 2
    xp = jnp.pad(x, ((0, 0), (lo, K - 1 - lo), (0, 0)))
    y = b
    for j in range(K):
        y = y + xp[:, j:j + L] * w[j]
    return y


def to_heads(a, n_heads):
    B, L, W = a.shape
    return a.reshape(B, L, n_heads, W // n_heads).transpose(0, 2, 1, 3)


def from_heads(a):
    B, H, L, Dh = a.shape
    return a.transpose(0, 2, 1, 3).reshape(B, L, H * Dh)


def rope_2d(x, rows, cols):
    dh = x.shape[-1]
    half, quarter = dh // 2, dh // 4
    inv = ROPE_BASE ** (-jnp.arange(quarter, dtype=F32) / quarter)

    def rot(xa, pos):
        ang = pos.astype(F32)[:, None] * inv
        cos, sin = jnp.cos(ang).astype(xa.dtype), jnp.sin(ang).astype(xa.dtype)
        x1, x2 = xa[..., :quarter], xa[..., quarter:]
        return jnp.concatenate([x1 * cos - x2 * sin, x1 * sin + x2 * cos], axis=-1)
    return jnp.concatenate([rot(x[..., :half], rows), rot(x[..., half:], cols)], axis=-1)


def mlstm_scan(q, k, v, ig, fg, state):
    B, H, L, Dh = q.shape
    nc = L // ML_CHUNK

    def chunks(a):
        return jnp.moveaxis(a.reshape(B, H, nc, ML_CHUNK, *a.shape[3:]), 2, 0)
    lower = jnp.tril(jnp.ones((ML_CHUNK, ML_CHUNK), dtype=bool))

    def step(carry, xs):
        C, n, m = carry
        qc, kc, vc, ic, fc = xs
        b = jnp.cumsum(jax.nn.log_sigmoid(fc), axis=-1)
        d = jnp.where(lower, b[..., :, None] - b[..., None, :] + ic[..., None, :], -jnp.inf)
        m_t = jnp.maximum(b + m[..., None], jnp.max(d, axis=-1))
        w = jnp.exp(d - m_t[..., None]) * jnp.einsum('bhtd,bhsd->bhts', qc, kc)
        carry_w = jnp.exp(b + m[..., None] - m_t)
        num = jnp.einsum('bhts,bhsd->bhtd', w, vc) + carry_w[..., None] * jnp.einsum('bhtd,bhde->bhte', qc, C)
        den = jnp.sum(w, axis=-1) + carry_w * jnp.einsum('bhtd,bhd->bht', qc, n)
        h = num / jnp.maximum(jnp.abs(den), jnp.exp(-m_t))[..., None]
        g = b[..., -1:] - b + ic
        m_new = jnp.maximum(b[..., -1] + m, jnp.max(g, axis=-1))
        keep = jnp.exp(b[..., -1] + m - m_new)
        wg = jnp.exp(g - m_new[..., None])
        C = keep[..., None, None] * C + jnp.einsum('bhs,bhsd,bhse->bhde', wg, kc, vc)
        n = keep[..., None] * n + jnp.einsum('bhs,bhsd->bhd', wg, kc)
        return (C, n, m_new), h
    state, hs = lax.scan(step, state, (chunks(q), chunks(k), chunks(v), chunks(ig), chunks(fg)))
    return jnp.moveaxis(hs, 0, 2).reshape(B, H, L, Dh), state


def mlstm_branch(pc_qkv, pl_qkv, pc_o, pl_o, pc_if, pl_if, conv_w, conv_b, gate_b, out_g, rows, cols, need_ctx):
    def prep(p_qkv, p_if, rotary):
        q, k, v = jnp.split(p_qkv, 3, axis=-1)
        qk = jax.nn.silu(centred_dwconv(jnp.concatenate([q, k], axis=-1), conv_w, conv_b))
        q, k = jnp.split(qk, 2, axis=-1)
        q = to_heads(q, ML_HEADS).astype(F32)
        k = to_heads(k, ML_HEADS).astype(F32)
        v = to_heads(v, ML_HEADS).astype(F32)
        if rotary:
            q, k = rope_2d(q, rows, cols), rope_2d(k, rows, cols)
        k = k * (HEAD_DIM ** -0.5)
        B, L, _ = p_if.shape
        g = p_if.astype(F32).reshape(B, L, 2, 2, ML_HEADS) + gate_b
        return q, k, v, g.transpose(2, 3, 0, 4, 1)
    qc, kc, vc, gc = prep(pc_qkv, pc_if, False)
    ql, kl, vl, gl = prep(pl_qkv, pl_if, True)
    B = ql.shape[0]
    zero = (jnp.zeros((B, ML_HEADS, HEAD_DIM, HEAD_DIM), F32),
            jnp.zeros((B, ML_HEADS, HEAD_DIM), F32), jnp.zeros((B, ML_HEADS), F32))
    flip = lambda a: jnp.flip(a, axis=2)
    hc_f, st_f = mlstm_scan(qc, kc, vc, gc[0, 0], gc[0, 1], zero)
    hl_f, _ = mlstm_scan(ql, kl, vl, gl[0, 0], gl[0, 1], st_f)
    hc_b, st_b = mlstm_scan(flip(qc), flip(kc), flip(vc), flip(gc[1, 0]), flip(gc[1, 1]), zero)
    hl_b, _ = mlstm_scan(flip(ql), flip(kl), flip(vl), flip(gl[1, 0]), flip(gl[1, 1]), st_b)

    def readout(h, o):
        h = from_heads(rms_norm(h, out_g.reshape(ML_HEADS, 1, HEAD_DIM)))
        return (h * jax.nn.sigmoid(o.astype(F32))).astype(o.dtype)
    yl = readout(hl_f + flip(hl_b), pl_o)
    yc = readout(hc_f + flip(hc_b), pc_o) if need_ctx else None
    return yc, yl


def hyena_filters(L, w1, b1, w2, b2, w3, freq, decay):
    n = jnp.arange(L, dtype=F32)
    t = n / max(L - 1, 1)
    bands = (HY_EMB - 1) // 2
    f = jnp.linspace(1e-4, bands - 1, bands, dtype=F32)
    ang = (2.0 * math.pi / L) * n[:, None] * f
    z = jnp.concatenate([t[:, None], jnp.cos(ang), -jnp.sin(ang)], axis=-1)
    a = jnp.sin(freq * (z @ w1 + b1))
    a = jnp.sin(freq * (a @ w2 + b2))
    h = ((a @ w3) * jnp.exp(-t[:, None] * jnp.abs(decay))).astype(F32)
    h = h.reshape(L, HY_ORDER, 2, HY_WIDTH)
    l1 = jnp.sum(jnp.abs(h[:, :, 0]), axis=0) + jnp.sum(jnp.abs(h[1:, :, 1]), axis=0)
    return h / l1[None, :, None]


def long_conv(u, hf, hb, d):
    B, L, W = u.shape
    taps = jnp.concatenate([hf, jnp.zeros((1, W), hf.dtype), hb[:0:-1]], axis=0)
    y = jnp.fft.irfft(jnp.fft.rfft(u, n=2 * L, axis=1) * jnp.fft.rfft(taps, axis=0)[None], n=2 * L, axis=1)[:, :L]
    return y + u * d


def hyena_branch(pc, pl_, conv_w, conv_b, w1, b1, w2, b2, w3, freq, decay, dskip, need_ctx):
    def run(p):
        L = p.shape[1]
        u = centred_dwconv(p, conv_w, conv_b).astype(F32)
        parts = jnp.split(u, HY_ORDER + 1, axis=-1)
        filt = hyena_filters(L, w1, b1, w2, b2, w3, freq, decay)
        z = parts[0]
        for o in range(HY_ORDER):
            z = parts[o + 1] * long_conv(z, filt[:, o, 0], filt[:, o, 1], dskip[o])
        return z.astype(p.dtype)
    yl = run(pl_)
    yc = run(pc) if need_ctx else None
    return yc, yl


def _dft_tables(L):
    n = 2 * L
    n1 = n // HY_N2
    assert n == n1 * HY_N2 and n1 % HY_K1_TILE == 0
    a1 = 2.0 * np.pi * np.outer(np.arange(n1), np.arange(n1)) / n1
    a2 = 2.0 * np.pi * np.outer(np.arange(HY_N2), np.arange(HY_N2)) / HY_N2
    at = 2.0 * np.pi * np.outer(np.arange(n1), np.arange(HY_N2)) / n

    def pair(a, dt):
        return jnp.asarray(np.cos(a), dt), jnp.asarray(-np.sin(a), dt)
    f1r, f1i = pair(a1, BF16)
    f2r, f2i = pair(a2, BF16)
    twr, twi = pair(at, F32)
    return f1r, f1i, f2r, f2i, twr[:, :, None], twi[:, :, None]


def _hy_conv_kernel(x_ref, prev_ref, next_ref, cw_ref, cb_ref, o0_ref, o1_ref, o2_ref):
    j = pl.program_id(1)
    has_prev = jnp.where(j > 0, 1.0, 0.0)
    has_next = jnp.where(j < pl.num_programs(1) - 1, 1.0, 0.0)
    x = x_ref[...]
    tc, w = x.shape
    row = lax.broadcasted_iota(jnp.int32, (tc, w), 0)
    xm1 = jnp.where(row == 0, prev_ref[SUBLANES - 1:SUBLANES, :] * has_prev, pltpu.roll(x, 1, 0))
    xp1 = jnp.where(row == tc - 1, next_ref[0:1, :] * has_next, pltpu.roll(x, tc - 1, 0))
    u = cb_ref[...] + xm1 * cw_ref[0:1, :] + x * cw_ref[1:2, :] + xp1 * cw_ref[2:3, :]
    for o, o_ref in enumerate((o0_ref, o1_ref, o2_ref)):
        o_ref[...] = u[:, o * HY_WIDTH:(o + 1) * HY_WIDTH]


def hy_short_conv(p, conv_w, conv_b, B, S):
    T = p.shape[0]
    tc, w = ROW_TILE, (HY_ORDER + 1) * HY_WIDTH
    assert conv_w.shape[0] == 3 and HY_ORDER == 2 and HY_OFF % w == 0
    col, n_blk, first = HY_OFF // w, S // tc, (T - B * S) // tc
    halo = tc // SUBLANES

    def blk(b, j):
        return first + b * n_blk + j
    out = pl.BlockSpec((tc, HY_WIDTH), lambda b, j: (b * n_blk + j, 0))
    return pl.pallas_call(
        _hy_conv_kernel,
        grid=(B, n_blk),
        in_specs=[pl.BlockSpec((tc, w), lambda b, j: (blk(b, j), col)),
                  pl.BlockSpec((SUBLANES, w), lambda b, j: (blk(b, j) * halo - 1, col)),
                  pl.BlockSpec((SUBLANES, w), lambda b, j: (jnp.minimum((blk(b, j) + 1) * halo, T // SUBLANES - 1), col)),
                  pl.BlockSpec((3, w), lambda b, j: (0, 0)), pl.BlockSpec((1, w), lambda b, j: (0, 0))],
        out_specs=[out, out, out],
        out_shape=[jax.ShapeDtypeStruct((B * S, HY_WIDTH), F32)] * 3,
        compiler_params=_cparams(("arbitrary", "arbitrary")),
        name="hy_short_conv",
    )(p, p, p, conv_w, conv_b[None, :])


def _hy_rows_kernel(x_ref, fr_ref, fi_ref, ar_ref, ai_ref):
    xb = x_ref[0].astype(BF16)
    ar_ref[0] = jnp.dot(fr_ref[...], xb, preferred_element_type=F32)
    ai_ref[0] = jnp.dot(fi_ref[...], xb, preferred_element_type=F32)


def hy_dft_rows(x, fr, fi):
    G, K, cols = x.shape
    n1 = fr.shape[0]
    ct = min(HY_COL_TILE, cols)
    out = pl.BlockSpec((1, n1, ct), lambda g, j: (g, 0, j))
    return pl.pallas_call(
        _hy_rows_kernel,
        grid=(G, cols // ct),
        in_specs=[pl.BlockSpec((1, K, ct), lambda g, j: (g, 0, j)),
                  pl.BlockSpec((n1, K), lambda g, j: (0, 0)), pl.BlockSpec((n1, K), lambda g, j: (0, 0))],
        out_specs=[out, out],
        out_shape=[jax.ShapeDtypeStruct((G, n1, cols), F32)] * 2,
        compiler_params=_cparams(("arbitrary", "arbitrary")),
        name="hy_dft_rows",
    )(x, fr[:, :K], fi[:, :K])


def _hy_mid_kernel(ar_ref, ai_ref, twr_ref, twi_ref, f2r_ref, f2i_ref, *rest, conv):
    if conv:
        hr_ref, hi_ref, gr_ref, gi_ref = rest
    else:
        gr_ref, gi_ref = rest
    f2r, f2i = f2r_ref[...], f2i_ref[...]

    def mm(a, b):
        return jnp.dot(a, b, preferred_element_type=F32)
    for j in range(ar_ref.shape[1]):
        ar, ai = ar_ref[0, j], ai_ref[0, j]
        twr, twi = twr_ref[j], twi_ref[j]
        pr = (ar * twr - ai * twi).astype(BF16)
        pi = (ar * twi + ai * twr).astype(BF16)
        xr = mm(f2r, pr) - mm(f2i, pi)
        xi = mm(f2r, pi) + mm(f2i, pr)
        if not conv:
            gr_ref[0, j] = xr
            gi_ref[0, j] = xi
            continue
        hr, hi = hr_ref[0, j], hi_ref[0, j]
        yr = (xr * hr - xi * hi).astype(BF16)
        yi = (xr * hi + xi * hr).astype(BF16)
        gr = mm(f2r, yr) + mm(f2i, yi)
        gi = mm(f2r, yi) - mm(f2i, yr)
        gr_ref[0, j] = gr * twr + gi * twi
        gi_ref[0, j] = gi * twr - gr * twi


def hy_mid(ar, ai, tables, spectrum=None, order=0):
    G, n1 = ar.shape[0], ar.shape[1]
    C = ar.shape[-1]
    _, _, f2r, f2i, twr, twi = tables
    kt = HY_K1_TILE
    blk = pl.BlockSpec((1, kt, HY_N2, C), lambda g, i: (g, i, 0, 0))
    tw = pl.BlockSpec((kt, HY_N2, 1), lambda g, i: (i, 0, 0))
    mat = pl.BlockSpec((HY_N2, HY_N2), lambda g, i: (0, 0))
    ins, specs = [ar, ai, twr, twi, f2r, f2i], [blk, blk, tw, tw, mat, mat]
    if spectrum is not None:
        hspec = pl.BlockSpec((1, kt, HY_N2, C), lambda g, i: (order, i, 0, 0))
        ins, specs = ins + list(spectrum), specs + [hspec, hspec]
    return pl.pallas_call(
        functools.partial(_hy_mid_kernel, conv=spectrum is not None),
        grid=(G, n1 // kt),
        in_specs=specs,
        out_specs=[blk, blk],
        out_shape=[jax.ShapeDtypeStruct(ar.shape, F32)] * 2,
        compiler_params=_cparams(("arbitrary", "arbitrary")),
        name="hy_mid_conv" if spectrum is not None else "hy_mid_filter",
    )(*ins)


def _hy_out_kernel(gr_ref, gi_ref, fr_ref, fi_ref, z_ref, xm_ref, d_ref, o_ref, *, scale):
    y = jnp.dot(fr_ref[...], gr_ref[0].astype(BF16), preferred_element_type=F32)
    y = y + jnp.dot(fi_ref[...], gi_ref[0].astype(BF16), preferred_element_type=F32)
    o_ref[0] = xm_ref[0] * (y * scale + z_ref[0] * d_ref[...])


def hy_out(gr, gi, fr, fi, z, xmul, d_cols):
    G, n1, cols = gr.shape
    rows = z.shape[1]
    ct = min(HY_COL_TILE // 2, cols)
    spec = pl.BlockSpec((1, n1, ct), lambda g, j: (g, 0, j))
    half = pl.BlockSpec((1, rows, ct), lambda g, j: (g, 0, j))
    mat = pl.BlockSpec((rows, n1), lambda g, j: (0, 0))
    return pl.pallas_call(
        functools.partial(_hy_out_kernel, scale=1.0 / (n1 * HY_N2)),
        grid=(G, cols // ct),
        in_specs=[spec, spec, mat, mat, half, half, pl.BlockSpec((1, ct), lambda g, j: (0, j))],
        out_specs=half,
        out_shape=jax.ShapeDtypeStruct(z.shape, F32),
        compiler_params=_cparams(("arbitrary", "arbitrary")),
        name="hy_out",
    )(gr, gi, fr[:rows], fi[:rows], z, xmul, d_cols)


def hyena_latents(p, conv_w, conv_b, filt, dskip, B, S):
    C = HY_WIDTH
    tables = _dft_tables(S)
    f1r, f1i = tables[0], tables[1]
    n1 = f1r.shape[0]
    cols = HY_N2 * C
    taps = jnp.concatenate([filt[:, :, 0], jnp.zeros((1, HY_ORDER, C), F32), filt[:0:-1, :, 1]], axis=0)
    taps = taps.transpose(1, 0, 2).reshape(HY_ORDER, n1, cols)
    tr, ti = hy_dft_rows(taps, f1r, f1i)
    spectrum = hy_mid(tr.reshape(HY_ORDER, n1, HY_N2, C), ti.reshape(HY_ORDER, n1, HY_N2, C), tables)

    parts = hy_short_conv(p, conv_w, conv_b, B, S)
    rows = S // HY_N2
    z = parts[0].reshape(B, rows, cols)
    for o in range(HY_ORDER):
        ar, ai = hy_dft_rows(z, f1r, f1i)
        gr, gi = hy_mid(ar.reshape(B, n1, HY_N2, C), ai.reshape(B, n1, HY_N2, C), tables, spectrum, o)
        z = hy_out(gr.reshape(B, n1, cols), gi.reshape(B, n1, cols), f1r, f1i, z,
                   parts[o + 1].reshape(B, rows, cols), jnp.tile(dskip[o], HY_N2)[None, :])
    return z.reshape(B * S, C)


def hyena_context(pc, conv_w, conv_b, filt, dskip):
    L = pc.shape[1]
    u = centred_dwconv(pc, conv_w, conv_b)
    parts = jnp.split(u, HY_ORDER + 1, axis=-1)
    z = parts[0]
    for o in range(HY_ORDER):
        z = parts[o + 1] * long_conv(z, filt[:, o, 0], filt[:, o, 1], dskip[o])
    return z


def kernel(x, c, ctx, c_ctx, norm1_g, norm2_g, ada_w, ada_b, w_in, na_qnorm_g, na_knorm_g, na_rpb,
           ml_conv_w, ml_conv_b, ml_gate_b, ml_out_g, lru_conv_w, lru_conv_b, lru_wa, lru_ba, lru_wx, lru_bx,
           lru_lambda, hy_conv_w, hy_conv_b, hy_w1, hy_b1, hy_w2, hy_b2, hy_w3, hy_freq, hy_decay, hy_dskip,
           br_w, br_gate_b, w_o, router_w, router_b, exp_w_gate, exp_w_up, exp_w_down,
           sh_w_gate, sh_w_up, sh_w_down):
    B, S, D = x.shape
    assert ctx.shape[1] == CTX_LEN and D == D_MODEL
    n_ctx = B * CTX_LEN
    T = n_ctx + B * S
    assert n_ctx % ROW_TILE == 0 and S % ROW_TILE == 0 and T % ROUTE_TILE == 0
    tiles = dict(ctx_tiles=n_ctx // ROW_TILE, tiles_per_batch=S // ROW_TILE)
    pos = jnp.arange(S)
    rows, cols = pos // GRID_W, pos % GRID_W

    xa = jnp.concatenate([ctx.reshape(n_ctx, D), x.reshape(B * S, D)], axis=0)
    for l in range(DEPTH):
        need_ctx = l < DEPTH - 1
        cond = jnp.concatenate([c_ctx[None, :], c], axis=0)
        mod = (jax.nn.silu(cond) @ ada_w[l] + ada_b[l]).reshape(1 + B, 6, D)
        sh1sc1, g1, sh2sc2, g2 = mod[:, 0:2], mod[:, 2:3], mod[:, 3:5], mod[:, 5:6]

        w_l = w_in[l]
        w_main = jnp.concatenate([w_l[:, :IF_START], w_l[:, IF_START + IF_WIDTH:]], axis=1).astype(BF16)
        w_if = jnp.pad(w_l[:, IF_START:IF_START + IF_WIDTH], ((0, 0), (0, LANES - IF_WIDTH)))
        p = norm_mod_project(xa, norm1_g[l][None, :], sh1sc1, w_main, **tiles)
        p_if, _ = norm_mod_narrow(xa, norm1_g[l][None, :], sh1sc1, w_if, **tiles)

        ya = neighbourhood_attention(p, na_rpb[l], na_qnorm_g[l], na_knorm_g[l], B, S)
        yb = mlstm(p, p_if, ml_conv_w[l], ml_conv_b[l], ml_gate_b[l], ml_out_g[l], B, S)
        yc = rglru(p, lru_conv_w[l], lru_conv_b[l], lru_wa[l], lru_ba[l], lru_wx[l], lru_bx[l], lru_lambda[l], B, S)
        hy_par = (hy_w1[l], hy_b1[l], hy_w2[l], hy_b2[l], hy_w3[l], hy_freq[l], hy_decay[l])
        yd_lat = hyena_latents(p, hy_conv_w[l], hy_conv_b[l], hyena_filters(S, *hy_par), hy_dskip[l], B, S)
        if need_ctx:
            pc = p[:n_ctx, HY_OFF:GATE_OFF].reshape(B, CTX_LEN, -1)
            yd_ctx = hyena_context(pc, hy_conv_w[l], hy_conv_b[l], hyena_filters(CTX_LEN, *hy_par), hy_dskip[l])
            yd_ctx = yd_ctx.reshape(n_ctx, HY_WIDTH)
        else:
            yd_ctx = jnp.zeros((n_ctx, HY_WIDTH), F32)
        yd = jnp.concatenate([yd_ctx, yd_lat], axis=0)
        xa = merge_branches([ya, yb, yc, yd], p, br_gate_b[l], br_w[l].astype(BF16), w_o[l].astype(BF16), xa, g1,
                            **tiles)

        logits, h2 = norm_mod_narrow(xa, norm2_g[l][None, :], sh2sc2, router_w[l], **tiles)
        idx, wts, rank, counts = route_tokens(logits, router_b[l])
        slot_of, slot_tok, block_exp, n_used = slot_layout(idx, rank, counts, T)
        h2_pad = jnp.concatenate([h2, jnp.zeros((1, D), BF16)], axis=0)
        ys = routed_experts(block_exp, n_used, h2_pad[slot_tok], exp_w_gate[l], exp_w_up[l], exp_w_down[l])
        routed = jnp.sum(ys[slot_of].astype(F32) * wts[:, :, None], axis=0)
        xa = ffn_out(h2, routed, xa, g2, sh_w_gate[l].astype(BF16), sh_w_up[l].astype(BF16),
                     sh_w_down[l].astype(BF16), **tiles)
    return xa[n_ctx:].reshape(B, S, D)
```

```python
import functools
import math

import numpy as np
import jax
import jax.numpy as jnp
from jax import lax
from jax.experimental import pallas as pl
from jax.experimental.pallas import tpu as pltpu

D_MODEL = 2048
DEPTH = 2
CTX_LEN = 256
GRID_W = 64
NORM_EPS = 1e-6
N_BRANCH = 4
BRANCH_WIDTH = D_MODEL // 4
HEAD_DIM = 128

NA_HEADS = BRANCH_WIDTH // HEAD_DIM
NA_WIN_ROWS = 8
NA_WIN_COLS = 16

ML_HEADS = BRANCH_WIDTH // HEAD_DIM
ML_WIDTH = BRANCH_WIDTH
ML_CHUNK = 128
ROPE_BASE = 10000.0

LRU_WIDTH = BRANCH_WIDTH
LRU_BLOCKS = 4
LRU_BW = LRU_WIDTH // LRU_BLOCKS
LRU_C = 8.0
LRU_CONV = 4

HY_WIDTH = BRANCH_WIDTH
HY_ORDER = 2
HY_EMB = 33
HY_N2 = 256
HY_K1_TILE = 2
HY_COL_TILE = 8192

N_EXPERTS = 128
TOP_K = 8
N_GROUPS = 8
GROUP_SIZE = N_EXPERTS // N_GROUPS
TOPK_GROUPS = 4
ROUTE_SCALE = 2.5

IN_SPLITS = (3 * BRANCH_WIDTH, 3 * ML_WIDTH, ML_WIDTH, 4 * ML_HEADS, LRU_WIDTH, LRU_WIDTH,
             (HY_ORDER + 1) * HY_WIDTH, N_BRANCH * D_MODEL)
IF_START = sum(IN_SPLITS[:3])
IF_WIDTH = IN_SPLITS[3]
MAIN_SPLITS = IN_SPLITS[:3] + IN_SPLITS[4:]
MAIN_OFF = np.concatenate([[0], np.cumsum(MAIN_SPLITS)]).tolist()
MAIN_WIDTH = MAIN_OFF[-1]
NA_OFF, LRU_X_OFF, LRU_G_OFF, HY_OFF, GATE_OFF = MAIN_OFF[0], MAIN_OFF[3], MAIN_OFF[4], MAIN_OFF[5], MAIN_OFF[6]

LANES = 128
SUBLANES = 8
ROW_TILE = 512
PROJ_COL_TILE = 1024
MERGE_COL_TILE = 512
MOE_ROW_TILE = 512
FFN_TILE = 256
ROUTE_TILE = 512
SEQ_CHUNK = CTX_LEN
NA_G = 4
NEG = -1e30
VMEM_LIMIT = 56 * 1024 * 1024

F32 = jnp.float32
BF16 = jnp.bfloat16


def _cparams(sem):
    return pltpu.CompilerParams(dimension_semantics=sem, vmem_limit_bytes=VMEM_LIMIT)


def _seg_of_tile(i, ctx_tiles, tiles_per_batch):
    return jnp.where(i < ctx_tiles, 0, 1 + (i - ctx_tiles) // tiles_per_batch)


def _norm_mod(x, g, shift, scale):
    ms = jnp.mean(x * x, axis=-1, keepdims=True)
    return (x * lax.rsqrt(ms + NORM_EPS) * g) * (1.0 + scale) + shift


def _head_norm(x, g):
    return x * lax.rsqrt(jnp.mean(x * x, axis=-1, keepdims=True) + NORM_EPS) * g


def _proj_kernel(x_ref, g_ref, mod_ref, w_ref, o_ref, h_scr):
    @pl.when(pl.program_id(1) == 0)
    def _():
        h = _norm_mod(x_ref[...], g_ref[...], mod_ref[0, 0:1, :], mod_ref[0, 1:2, :])
        h_scr[...] = h.astype(BF16)
    o_ref[...] = jnp.dot(h_scr[...], w_ref[...], preferred_element_type=F32)


def norm_mod_project(x, g, mod, w, ctx_tiles, tiles_per_batch):
    T, D = x.shape
    N = w.shape[1]
    seg = functools.partial(_seg_of_tile, ctx_tiles=ctx_tiles, tiles_per_batch=tiles_per_batch)
    return pl.pallas_call(
        _proj_kernel,
        grid=(T // ROW_TILE, N // PROJ_COL_TILE),
        in_specs=[
            pl.BlockSpec((ROW_TILE, D), lambda i, j: (i, 0)),
            pl.BlockSpec((1, D), lambda i, j: (0, 0)),
            pl.BlockSpec((1, 2, D), lambda i, j: (seg(i), 0, 0)),
            pl.BlockSpec((D, PROJ_COL_TILE), lambda i, j: (0, j)),
        ],
        out_specs=pl.BlockSpec((ROW_TILE, PROJ_COL_TILE), lambda i, j: (i, j)),
        out_shape=jax.ShapeDtypeStruct((T, N), F32),
        scratch_shapes=[pltpu.VMEM((ROW_TILE, D), BF16)],
        compiler_params=_cparams(("arbitrary", "arbitrary")),
        name="norm_mod_project",
    )(x, g, mod, w)


def _narrow_kernel(x_ref, g_ref, mod_ref, w_ref, o_ref, h_ref):
    h = _norm_mod(x_ref[...], g_ref[...], mod_ref[0, 0:1, :], mod_ref[0, 1:2, :])
    h_ref[...] = h.astype(BF16)
    o_ref[...] = jnp.dot(h, w_ref[...], preferred_element_type=F32, precision=lax.Precision.HIGHEST)


def norm_mod_narrow(x, g, mod, w, ctx_tiles, tiles_per_batch):
    T, D = x.shape
    seg = functools.partial(_seg_of_tile, ctx_tiles=ctx_tiles, tiles_per_batch=tiles_per_batch)
    return pl.pallas_call(
        _narrow_kernel,
        grid=(T // ROW_TILE,),
        in_specs=[
            pl.BlockSpec((ROW_TILE, D), lambda i: (i, 0)),
            pl.BlockSpec((1, D), lambda i: (0, 0)),
            pl.BlockSpec((1, 2, D), lambda i: (seg(i), 0, 0)),
            pl.BlockSpec((D, LANES), lambda i: (0, 0)),
        ],
        out_specs=[pl.BlockSpec((ROW_TILE, LANES), lambda i: (i, 0)),
                   pl.BlockSpec((ROW_TILE, D), lambda i: (i, 0))],
        out_shape=[jax.ShapeDtypeStruct((T, LANES), F32), jax.ShapeDtypeStruct((T, D), BF16)],
        compiler_params=_cparams(("arbitrary",)),
        name="norm_mod_narrow",
    )(x, g, mod, w)


def _na_kernel(q_ref, k0_ref, k1_ref, k2_ref, v0_ref, v1_ref, v2_ref, kc_ref, vc_ref, bias_ref, qg_ref, kg_ref,
               o_ref, *, n_rows):
    s = pl.program_id(1)
    qb, kw = bias_ref.shape[1], bias_ref.shape[2]
    first_row = (s - 1) * NA_G
    q_row = first_row + lax.broadcasted_iota(jnp.int32, (qb, kw), 0) // GRID_W
    k_row = first_row - NA_G + lax.broadcasted_iota(jnp.int32, (qb, kw), 1) // GRID_W
    r0 = jnp.clip(q_row - NA_WIN_ROWS // 2, 0, n_rows - NA_WIN_ROWS)
    valid = (k_row >= r0) & (k_row < r0 + NA_WIN_ROWS) & (s > 0)
    nt = (((1,), (1,)), ((), ()))
    for h in range(NA_HEADS):
        hs = slice(h * HEAD_DIM, (h + 1) * HEAD_DIM)
        qn = (_head_norm(q_ref[:, hs], qg_ref[...]) * (HEAD_DIM ** -0.5)).astype(BF16)
        k_win = jnp.concatenate([k0_ref[:, hs], k1_ref[:, hs], k2_ref[:, hs]], axis=0)
        kn = _head_norm(k_win, kg_ref[...]).astype(BF16)
        s_lat = lax.dot_general(qn, kn, nt, preferred_element_type=F32)
        s_lat = jnp.where(valid, s_lat + bias_ref[h], NEG)
        kcn = _head_norm(kc_ref[:, hs], kg_ref[...]).astype(BF16)
        s_ctx = lax.dot_general(qn, kcn, nt, preferred_element_type=F32)
        m = jnp.maximum(jnp.max(s_lat, axis=-1, keepdims=True), jnp.max(s_ctx, axis=-1, keepdims=True))
        p_lat = jnp.exp(s_lat - m)
        p_ctx = jnp.exp(s_ctx - m)
        denom = jnp.sum(p_lat, axis=-1, keepdims=True) + jnp.sum(p_ctx, axis=-1, keepdims=True)
        v_win = jnp.concatenate([v0_ref[:, hs], v1_ref[:, hs], v2_ref[:, hs]], axis=0).astype(BF16)
        o = jnp.dot(p_lat.astype(BF16), v_win, preferred_element_type=F32)
        o = o + jnp.dot(p_ctx.astype(BF16), vc_ref[:, hs].astype(BF16), preferred_element_type=F32)
        o_ref[:, hs] = o / denom


def na_bias_table(rpb):
    qb, kw = NA_G * GRID_W, 3 * NA_G * GRID_W
    qg, qc = np.arange(qb)[:, None] // GRID_W, np.arange(qb)[:, None] % GRID_W
    kr, kc = np.arange(kw)[None, :] // GRID_W, np.arange(kw)[None, :] % GRID_W
    dr = kr - NA_G - qg
    c0 = np.clip(qc - NA_WIN_COLS // 2, 0, GRID_W - NA_WIN_COLS)
    ok = (kc >= c0) & (kc < c0 + NA_WIN_COLS) & (np.abs(dr) < NA_WIN_ROWS)
    ri = np.clip(dr + NA_WIN_ROWS - 1, 0, 2 * NA_WIN_ROWS - 2)
    ci = np.clip(kc - qc + NA_WIN_COLS - 1, 0, 2 * NA_WIN_COLS - 2)
    return jnp.where(ok[None], rpb[:, ri, ci], NEG)


def neighbourhood_attention(p, rpb, qg, kg, B, S):
    T = p.shape[0]
    qb = NA_G * GRID_W
    assert qb == CTX_LEN and NA_WIN_ROWS // 2 <= NA_G and NA_WIN_ROWS - NA_WIN_ROWS // 2 - 1 <= NA_G
    ctx_blocks, n_blk, n_rows = B * CTX_LEN // qb, S // qb, S // GRID_W
    c_q = NA_OFF // BRANCH_WIDTH
    c_k, c_v = c_q + 1, c_q + 2

    def q_map(b, s):
        return (jnp.where(s == 0, b, ctx_blocks + b * n_blk + s - 1), c_q)

    def win_spec(d, col):
        return pl.BlockSpec((qb, BRANCH_WIDTH),
                            lambda b, s: (ctx_blocks + b * n_blk + jnp.clip(s - 1 + d, 0, n_blk - 1), col))

    bias = na_bias_table(rpb)
    return pl.pallas_call(
        functools.partial(_na_kernel, n_rows=n_rows),
        grid=(B, 1 + n_blk),
        in_specs=[pl.BlockSpec((qb, BRANCH_WIDTH), q_map),
                  win_spec(-1, c_k), win_spec(0, c_k), win_spec(1, c_k),
                  win_spec(-1, c_v), win_spec(0, c_v), win_spec(1, c_v),
                  pl.BlockSpec((qb, BRANCH_WIDTH), lambda b, s: (b, c_k)),
                  pl.BlockSpec((qb, BRANCH_WIDTH), lambda b, s: (b, c_v)),
                  pl.BlockSpec(bias.shape, lambda b, s: (0, 0, 0)),
                  pl.BlockSpec((1, HEAD_DIM), lambda b, s: (0, 0)),
                  pl.BlockSpec((1, HEAD_DIM), lambda b, s: (0, 0))],
        out_specs=pl.BlockSpec((qb, BRANCH_WIDTH), lambda b, s: (q_map(b, s)[0], 0)),
        out_shape=jax.ShapeDtypeStruct((T, BRANCH_WIDTH), F32),
        compiler_params=_cparams(("arbitrary", "arbitrary")),
        name="neighbourhood_attention",
    )(p, p, p, p, p, p, p, p, p, bias, qg[None, :], kg[None, :])


def _chunk_edges(reverse):
    s = pl.program_id(1)
    n_lat = pl.num_programs(1) - 1
    lat_idx = (n_lat - s) if reverse else (s - 1)
    has_prev = jnp.where((s > 0) & (lat_idx > 0), 1.0, 0.0)
    has_next = jnp.where((s > 0) & (lat_idx < n_lat - 1), 1.0, 0.0)
    return has_prev, has_next


def _conv4(x_ref, prev_ref, next_ref, cw_ref, cb_ref, has_prev, has_next):
    x = x_ref[...]
    tc, w = x.shape
    row = lax.broadcasted_iota(jnp.int32, (tc, w), 0)
    before = prev_ref[SUBLANES - 1:SUBLANES, :] * has_prev
    after0 = next_ref[0:1, :] * has_next
    after1 = next_ref[1:2, :] * has_next
    xm1 = jnp.where(row == 0, before, pltpu.roll(x, 1, 0))
    xp1 = jnp.where(row == tc - 1, after0, pltpu.roll(x, tc - 1, 0))
    xp2 = jnp.where(row == tc - 2, after0, jnp.where(row == tc - 1, after1, pltpu.roll(x, tc - 2, 0)))
    return cb_ref[...] + xm1 * cw_ref[0:1, :] + x * cw_ref[1:2, :] + xp1 * cw_ref[2:3, :] + xp2 * cw_ref[3:4, :]


def _seq_specs(reverse, B, S, T, width):
    tc = SEQ_CHUNK
    n_lat, ctx_blocks, halo = S // tc, B, tc // SUBLANES

    def blk(b, s):
        lat = (n_lat - s) if reverse else (s - 1)
        return jnp.where(s == 0, b, ctx_blocks + b * n_lat + lat)

    def chunk(col):
        return pl.BlockSpec((tc, width), lambda b, s: (blk(b, s), col))

    def prev(col):
        return pl.BlockSpec((SUBLANES, width), lambda b, s: (jnp.maximum(blk(b, s) * halo - 1, 0), col))

    def nxt(col):
        return pl.BlockSpec((SUBLANES, width), lambda b, s: (jnp.minimum((blk(b, s) + 1) * halo, T // SUBLANES - 1), col))

    def full(shape):
        return pl.BlockSpec(shape, lambda b, s: (0,) * len(shape))
    return blk, chunk, prev, nxt, full


def _lru_kernel(x_ref, prev_ref, next_ref, cw_ref, cb_ref, wa_ref, ba_ref, wx_ref, bx_ref, sp_ref, *rest, reverse):
    if reverse:
        hf_ref, g_ref, o_ref, carry = rest
    else:
        o_ref, carry = rest
    s = pl.program_id(1)
    tc, w = x_ref.shape

    @pl.when(s == 0)
    def _():
        carry[...] = jnp.zeros_like(carry)

    has_prev, has_next = _chunk_edges(reverse)
    xc = _conv4(x_ref, prev_ref, next_ref, cw_ref, cb_ref, has_prev, has_next)
    row = lax.broadcasted_iota(jnp.int32, (tc, w), 0)

    def block_diag(w_ref):
        return jnp.concatenate(
            [jnp.dot(xc[:, n * LRU_BW:(n + 1) * LRU_BW], w_ref[n], preferred_element_type=F32,
                     precision=lax.Precision.HIGHEST) for n in range(LRU_BLOCKS)], axis=-1)
    r = jax.nn.sigmoid(block_diag(wa_ref) + ba_ref[...])
    i = jax.nn.sigmoid(block_diag(wx_ref) + bx_ref[...])
    log_a = -LRU_C * r * sp_ref[...]
    a = jnp.exp(log_a)
    u = jnp.sqrt(1.0 - jnp.exp(2.0 * log_a)) * (i * xc)

    step = 1
    while step < tc:
        if reverse:
            keep = row < tc - step
            shift = tc - step
        else:
            keep = row >= step
            shift = step
        a_sh = jnp.where(keep, pltpu.roll(a, shift, 0), 1.0)
        u_sh = jnp.where(keep, pltpu.roll(u, shift, 0), 0.0)
        u = a * u_sh + u
        a = a * a_sh
        step *= 2
    h = u + a * carry[...]
    carry[...] = h[0:1, :] if reverse else h[tc - 1:tc, :]
    if reverse:
        o_ref[...] = (hf_ref[...] + h) * jax.nn.gelu(g_ref[...])
    else:
        o_ref[...] = h


def rglru(p, conv_w, conv_b, wa, ba, wx, bx, lam, B, S):
    T = p.shape[0]
    tc, w = SEQ_CHUNK, LRU_WIDTH
    assert CTX_LEN == tc and S % tc == 0 and LRU_CONV == 4
    c_x, c_g = LRU_X_OFF // w, LRU_G_OFF // w
    sp = jax.nn.softplus(-lam)

    def call(reverse, extra_in, extra_cols):
        blk, chunk, prev, nxt, full = _seq_specs(reverse, B, S, T, w)
        d = 1 if reverse else 0
        return pl.pallas_call(
            functools.partial(_lru_kernel, reverse=reverse),
            grid=(B, 1 + S // tc),
            in_specs=[chunk(c_x), prev(c_x), nxt(c_x),
                      full((LRU_CONV, w)), full((1, w)), full((LRU_BLOCKS, LRU_BW, LRU_BW)), full((1, w)),
                      full((LRU_BLOCKS, LRU_BW, LRU_BW)), full((1, w)), full((1, w))] + [chunk(c) for c in extra_cols],
            out_specs=chunk(0),
            out_shape=jax.ShapeDtypeStruct((T, w), F32),
            scratch_shapes=[pltpu.VMEM((1, w), F32)],
            compiler_params=_cparams(("arbitrary", "arbitrary")),
            name="rglru_bwd" if reverse else "rglru_fwd",
        )(p, p, p, conv_w, conv_b[None, :], wa[d], ba[d][None, :], wx[d], bx[d][None, :], sp[d][None, :], *extra_in)

    h_fwd = call(False, [], [])
    return call(True, [h_fwd, p], [0, c_g])


def _log_sigmoid(x):
    return jnp.minimum(x, 0.0) - jnp.log(1.0 + jnp.exp(-jnp.abs(x)))


def _mlstm_kernel(q_ref, qp_ref, qn_ref, k_ref, kp_ref, kn_ref, v_ref, if_ref, cos_ref, sin_ref, cwq_ref, cbq_ref,
                  cwk_ref, cbk_ref, gb_ref, *rest, reverse):
    if reverse:
        hf_ref, og_ref, outg_ref, o_ref, c_scr, n_scr, m_scr = rest
    else:
        o_ref, c_scr, n_scr, m_scr = rest
    s = pl.program_id(1)
    tc = q_ref.shape[0]
    ck = ML_CHUNK
    d0 = (1 if reverse else 0) * 2 * ML_HEADS

    @pl.when(s == 0)
    def _():
        c_scr[...] = jnp.zeros_like(c_scr)
        n_scr[...] = jnp.zeros_like(n_scr)
        m_scr[...] = jnp.zeros_like(m_scr)

    has_prev, has_next = _chunk_edges(reverse)
    q_all = jax.nn.silu(_conv4(q_ref, qp_ref, qn_ref, cwq_ref, cbq_ref, has_prev, has_next))
    k_all = jax.nn.silu(_conv4(k_ref, kp_ref, kn_ref, cwk_ref, cbk_ref, has_prev, has_next))

    is_lat = s > 0
    cos = jnp.where(is_lat, cos_ref[...], 1.0)
    sin = jnp.where(is_lat, sin_ref[...], 0.0)
    lane = lax.broadcasted_iota(jnp.int32, (tc, HEAD_DIM), 1)
    low_quarter = (lane % (HEAD_DIM // 2)) < (HEAD_DIM // 4)

    def rope(xh):
        swapped = jnp.where(low_quarter, pltpu.roll(xh, HEAD_DIM - HEAD_DIM // 4, 1), pltpu.roll(xh, HEAD_DIM // 4, 1))
        return xh * cos + swapped * sin

    row = lax.broadcasted_iota(jnp.int32, (ck, ck), 0)
    col = lax.broadcasted_iota(jnp.int32, (ck, ck), 1)
    causal = (col >= row) if reverse else (col <= row)
    nt = (((1,), (1,)), ((), ()))
    tn = (((0,), (0,)), ((), ()))

    qs, ks = [], []
    for h in range(ML_HEADS):
        hs = slice(h * HEAD_DIM, (h + 1) * HEAD_DIM)
        qs.append(rope(q_all[:, hs]))
        ks.append(rope(k_all[:, hs]) * (HEAD_DIM ** -0.5))

    subs = range(tc // ck)
    for sub in (reversed(subs) if reverse else subs):
        rs = slice(sub * ck, (sub + 1) * ck)
        gates = if_ref[rs, :] + gb_ref[...]
        b_all = _log_sigmoid(gates)
        step = 1
        while step < ck:
            if reverse:
                b_all = b_all + jnp.where(row < ck - step, pltpu.roll(b_all, ck - step, 0), 0.0)
            else:
                b_all = b_all + jnp.where(row >= step, pltpu.roll(b_all, step, 0), 0.0)
            step *= 2
        b_all_t = b_all.T
        gates_t = gates.T
        last = 0 if reverse else ck - 1
        for h in range(ML_HEADS):
            hs = slice(h * HEAD_DIM, (h + 1) * HEAD_DIM)
            ci, cf = d0 + h, d0 + ML_HEADS + h
            qh, kh, vh = qs[h][rs], ks[h][rs], v_ref[rs, hs]
            b_col, b_row = b_all[:, cf:cf + 1], b_all_t[cf:cf + 1, :]
            i_col, i_row = gates[:, ci:ci + 1], gates_t[ci:ci + 1, :]
            b_last = b_all[last:last + 1, cf:cf + 1]
            m_prev = m_scr[h:h + 1, 0:1]
            d = jnp.where(causal, b_col - b_row + i_row, -jnp.inf)
            m_t = jnp.maximum(b_col + m_prev, jnp.max(d, axis=-1, keepdims=True))
            qb, kb, vb = qh.astype(BF16), kh.astype(BF16), vh.astype(BF16)
            w = jnp.exp(d - m_t) * lax.dot_general(qb, kb, nt, preferred_element_type=F32)
            carry_w = jnp.exp(b_col + m_prev - m_t)
            num = jnp.dot(w.astype(BF16), vb, preferred_element_type=F32)
            num = num + carry_w * jnp.dot(qb, c_scr[h].astype(BF16), preferred_element_type=F32)
            den = jnp.sum(w, axis=-1, keepdims=True) + carry_w * jnp.sum(qh * n_scr[h:h + 1, :], axis=-1, keepdims=True)
            hh = num / jnp.maximum(jnp.abs(den), jnp.exp(-m_t))
            g_col = b_last - b_col + i_col
            m_new = jnp.maximum(b_last + m_prev, jnp.max(g_col, axis=0, keepdims=True))
            keep = jnp.exp(b_last + m_prev - m_new)
            kw = kh * jnp.exp(g_col - m_new)
            c_scr[h] = keep * c_scr[h] + lax.dot_general(kw.astype(BF16), vb, tn, preferred_element_type=F32)
            n_scr[h:h + 1, :] = keep * n_scr[h:h + 1, :] + jnp.sum(kw, axis=0, keepdims=True)
            m_scr[h:h + 1, :] = jnp.broadcast_to(m_new, (1, LANES))
            if reverse:
                hsum = hf_ref[rs, hs] + hh
                o_ref[rs, hs] = _head_norm(hsum, outg_ref[:, hs]) * jax.nn.sigmoid(og_ref[rs, hs])
            else:
                o_ref[rs, hs] = hh


def rope_tables(S):
    quarter = HEAD_DIM // 4
    inv = ROPE_BASE ** (-jnp.arange(quarter, dtype=F32) / quarter)
    pos = jnp.arange(S)
    ang_r = (pos // GRID_W).astype(F32)[:, None] * inv
    ang_c = (pos % GRID_W).astype(F32)[:, None] * inv
    cos = jnp.concatenate([jnp.cos(ang_r), jnp.cos(ang_r), jnp.cos(ang_c), jnp.cos(ang_c)], axis=-1)
    sin = jnp.concatenate([-jnp.sin(ang_r), jnp.sin(ang_r), -jnp.sin(ang_c), jnp.sin(ang_c)], axis=-1)
    return cos, sin


def mlstm(p, p_if, conv_w, conv_b, gate_b, out_g, B, S):
    T = p.shape[0]
    tc, w = SEQ_CHUNK, ML_WIDTH
    assert CTX_LEN == tc and S % tc == 0 and tc % ML_CHUNK == 0 and conv_w.shape[0] == 4
    c_q = MAIN_OFF[1] // w
    c_k, c_v, c_o = c_q + 1, c_q + 2, MAIN_OFF[2] // w
    cos, sin = rope_tables(S)
    gb = jnp.pad(gate_b.reshape(1, -1), ((0, 0), (0, LANES - gate_b.size)))
    n_lat = S // tc

    def call(reverse, extra_in, extra_specs):
        blk, chunk, prev, nxt, full = _seq_specs(reverse, B, S, T, w)

        def lat_rows(b, s):
            return (jnp.clip((n_lat - s) if reverse else (s - 1), 0, n_lat - 1), 0)
        table = pl.BlockSpec((tc, HEAD_DIM), lat_rows)
        return pl.pallas_call(
            functools.partial(_mlstm_kernel, reverse=reverse),
            grid=(B, 1 + n_lat),
            in_specs=[chunk(c_q), prev(c_q), nxt(c_q), chunk(c_k), prev(c_k), nxt(c_k), chunk(c_v),
                      pl.BlockSpec((tc, LANES), lambda b, s: (blk(b, s), 0)), table, table,
                      full((4, w)), full((1, w)), full((4, w)), full((1, w)), full((1, LANES))] + extra_specs(chunk, full),
            out_specs=chunk(0),
            out_shape=jax.ShapeDtypeStruct((T, w), F32),
            scratch_shapes=[pltpu.VMEM((ML_HEADS, HEAD_DIM, HEAD_DIM), F32), pltpu.VMEM((SUBLANES, HEAD_DIM), F32),
                            pltpu.VMEM((SUBLANES, LANES), F32)],
            compiler_params=_cparams(("arbitrary", "arbitrary")),
            name="mlstm_bwd" if reverse else "mlstm_fwd",
        )(p, p, p, p, p, p, p, p_if, cos, sin, conv_w[:, :w], conv_b[None, :w], conv_w[:, w:], conv_b[None, w:], gb,
          *extra_in)

    h_fwd = call(False, [], lambda chunk, full: [])
    return call(True, [h_fwd, p, out_g[None, :]], lambda chunk, full: [chunk(0), chunk(c_o), full((1, w))])


def _merge_kernel(ya_ref, yb_ref, yc_ref, yd_ref, ga_ref, gb_ref, gc_ref, gd_ref, bias_ref, bw_ref, wo_ref,
                  x_ref, g1_ref, o_ref, acc_ref):
    n = pl.program_id(1)
    z = None
    for b, (y_ref, gate_ref) in enumerate(((ya_ref, ga_ref), (yb_ref, gb_ref), (yc_ref, gc_ref), (yd_ref, gd_ref))):
        t = jnp.dot(y_ref[...].astype(BF16), bw_ref[b], preferred_element_type=F32)
        t = jax.nn.sigmoid(gate_ref[...] + bias_ref[b:b + 1, :]) * t
        z = t if z is None else z + t
    part = jnp.dot(z.astype(BF16), wo_ref[...], preferred_element_type=F32)

    @pl.when(n == 0)
    def _():
        acc_ref[...] = part

    @pl.when(n > 0)
    def _():
        acc_ref[...] += part

    @pl.when(n == pl.num_programs(1) - 1)
    def _():
        o_ref[...] = x_ref[...] + g1_ref[0] * acc_ref[...]


def merge_branches(ys, p, bias, bw, wo, x, g1, ctx_tiles, tiles_per_batch):
    T, D = x.shape
    ct = MERGE_COL_TILE
    seg = functools.partial(_seg_of_tile, ctx_tiles=ctx_tiles, tiles_per_batch=tiles_per_batch)
    y_spec = pl.BlockSpec((ROW_TILE, BRANCH_WIDTH), lambda i, n: (i, 0))

    def gate_spec(b):
        base = (GATE_OFF + b * D) // ct
        return pl.BlockSpec((ROW_TILE, ct), lambda i, n: (i, base + n))

    return pl.pallas_call(
        _merge_kernel,
        grid=(T // ROW_TILE, D // ct),
        in_specs=[y_spec, y_spec, y_spec, y_spec,
                  gate_spec(0), gate_spec(1), gate_spec(2), gate_spec(3),
                  pl.BlockSpec((N_BRANCH, ct), lambda i, n: (0, n)),
                  pl.BlockSpec((N_BRANCH, BRANCH_WIDTH, ct), lambda i, n: (0, 0, n)),
                  pl.BlockSpec((ct, D), lambda i, n: (n, 0)),
                  pl.BlockSpec((ROW_TILE, D), lambda i, n: (i, 0)),
                  pl.BlockSpec((1, 1, D), lambda i, n: (seg(i), 0, 0))],
        out_specs=pl.BlockSpec((ROW_TILE, D), lambda i, n: (i, 0)),
        out_shape=jax.ShapeDtypeStruct((T, D), F32),
        scratch_shapes=[pltpu.VMEM((ROW_TILE, D), F32)],
        compiler_params=_cparams(("arbitrary", "arbitrary")),
        name="merge_branches",
    )(*ys, p, p, p, p, bias, bw, wo, x, g1)


def _route_kernel(lg_ref, rb_ref, idx_ref, w_ref, rank_ref, cnt_ref, run_scr):
    @pl.when(pl.program_id(0) == 0)
    def _():
        run_scr[...] = jnp.zeros_like(run_scr)

    tm = lg_ref.shape[0]
    E = N_EXPERTS
    scores = jax.nn.sigmoid(lg_ref[...].T)
    biased = scores + rb_ref[...]
    ninf = -jnp.inf

    gscore = []
    for g in range(N_GROUPS):
        blk = biased[g * GROUP_SIZE:(g + 1) * GROUP_SIZE, :]
        m1 = jnp.max(blk, axis=0, keepdims=True)
        n_top = jnp.sum(jnp.where(blk == m1, 1.0, 0.0), axis=0, keepdims=True)
        m2 = jnp.max(jnp.where(blk < m1, blk, ninf), axis=0, keepdims=True)
        gscore.append(m1 + jnp.where(n_top >= 2.0, m1, m2))
    parts = []
    for g in range(N_GROUPS):
        beaten = jnp.zeros_like(gscore[g])
        for o in range(N_GROUPS):
            if o != g:
                wins = (gscore[o] >= gscore[g]) if o < g else (gscore[o] > gscore[g])
                beaten = beaten + jnp.where(wins, 1.0, 0.0)
        keep = beaten < float(TOPK_GROUPS)
        parts.append(jnp.where(keep, biased[g * GROUP_SIZE:(g + 1) * GROUP_SIZE, :], ninf))
    cur = jnp.concatenate(parts, axis=0)

    eidx = lax.broadcasted_iota(jnp.int32, (E, tm), 0).astype(F32)
    picks, wts = [], []
    sel = jnp.zeros((E, tm), F32)
    for _ in range(TOP_K):
        m = jnp.max(cur, axis=0, keepdims=True)
        ik = jnp.min(jnp.where(cur == m, eidx, float(E)), axis=0, keepdims=True)
        hit = eidx == ik
        picks.append(ik)
        wts.append(jnp.sum(jnp.where(hit, scores, 0.0), axis=0, keepdims=True))
        sel = sel + jnp.where(hit, 1.0, 0.0)
        cur = jnp.where(hit, ninf, cur)
    total = wts[0]
    for k in range(1, TOP_K):
        total = total + wts[k]

    before = lax.broadcasted_iota(jnp.int32, (tm, tm), 0) < lax.broadcasted_iota(jnp.int32, (tm, tm), 1)
    prefix = jnp.dot(sel.astype(BF16), jnp.where(before, 1.0, 0.0).astype(BF16), preferred_element_type=F32)
    base = prefix + run_scr[:, 0:1]
    for k in range(TOP_K):
        idx_ref[k:k + 1, :] = picks[k].astype(jnp.int32)
        w_ref[k:k + 1, :] = wts[k] / total * ROUTE_SCALE
        rank_ref[k:k + 1, :] = jnp.sum(jnp.where(eidx == picks[k], base, 0.0), axis=0, keepdims=True).astype(jnp.int32)
    run_scr[...] = run_scr[...] + jnp.sum(sel, axis=1, keepdims=True)
    cnt_ref[...] = run_scr[...]


def route_tokens(logits, router_b):
    T, E = logits.shape
    tm = ROUTE_TILE
    per_k = pl.BlockSpec((TOP_K, tm), lambda i: (0, i))
    idx, wts, rank, cnt = pl.pallas_call(
        _route_kernel,
        grid=(T // tm,),
        in_specs=[pl.BlockSpec((tm, E), lambda i: (i, 0)), pl.BlockSpec((E, 1), lambda i: (0, 0))],
        out_specs=[per_k, per_k, per_k, pl.BlockSpec((E, LANES), lambda i: (0, 0))],
        out_shape=[jax.ShapeDtypeStruct((TOP_K, T), jnp.int32), jax.ShapeDtypeStruct((TOP_K, T), F32),
                   jax.ShapeDtypeStruct((TOP_K, T), jnp.int32), jax.ShapeDtypeStruct((E, LANES), F32)],
        scratch_shapes=[pltpu.VMEM((E, LANES), F32)],
        compiler_params=_cparams(("arbitrary",)),
        name="route_tokens",
    )(logits, router_b[:, None])
    return idx, wts, rank, cnt[:, 0].astype(jnp.int32)


def slot_layout(idx, rank, counts, T):
    E, tb = N_EXPERTS, MOE_ROW_TILE
    padded = (counts + tb - 1) // tb * tb
    pend = jnp.cumsum(padded)
    slot_of = (pend - padded)[idx] + rank
    n_blocks = -(-T * TOP_K // tb) + E
    tok = jnp.broadcast_to(jnp.arange(T, dtype=jnp.int32), (TOP_K, T))
    slot_tok = jnp.full((n_blocks * tb,), T, jnp.int32).at[slot_of.reshape(-1)].set(tok.reshape(-1))
    block_exp = jnp.minimum(jnp.searchsorted(pend, jnp.arange(n_blocks) * tb, side='right'), E - 1).astype(jnp.int32)
    n_used = (pend[-1] // tb).astype(jnp.int32).reshape(1)
    return slot_of, slot_tok, block_exp, n_used


def _expert_kernel(be_ref, nb_ref, xs_ref, wg_ref, wu_ref, wd_ref, o_ref, wg_s, wu_s, wd_s):
    i = pl.program_id(0)
    live = i < nb_ref[0]

    @pl.when(live & ((i == 0) | (be_ref[i] != be_ref[jnp.maximum(i - 1, 0)])))
    def _():
        wg_s[...] = wg_ref[0].astype(BF16)
        wu_s[...] = wu_ref[0].astype(BF16)
        wd_s[...] = wd_ref[0].astype(BF16)

    @pl.when(live)
    def _():
        xb = xs_ref[...]
        a = jax.nn.silu(jnp.dot(xb, wg_s[...], preferred_element_type=F32))
        a = a * jnp.dot(xb, wu_s[...], preferred_element_type=F32)
        o_ref[...] = jnp.dot(a.astype(BF16), wd_s[...], preferred_element_type=F32).astype(o_ref.dtype)

    @pl.when(i >= nb_ref[0])
    def _():
        o_ref[...] = jnp.zeros_like(o_ref)


def routed_experts(block_exp, n_used, xs, wg, wu, wd):
    cap, D = xs.shape
    H = wg.shape[-1]
    tb = MOE_ROW_TILE

    def row(i, be, nb):
        return (jnp.minimum(i, nb[0] - 1), 0)

    def wsel(i, be, nb):
        return (be[jnp.minimum(i, nb[0] - 1)], 0, 0)

    grid_spec = pltpu.PrefetchScalarGridSpec(
        num_scalar_prefetch=2,
        grid=(cap // tb,),
        in_specs=[pl.BlockSpec((tb, D), row),
                  pl.BlockSpec((1, D, H), wsel),
                  pl.BlockSpec((1, D, H), wsel),
                  pl.BlockSpec((1, H, D), wsel)],
        out_specs=pl.BlockSpec((tb, D), lambda i, be, nb: (i, 0)),
        scratch_shapes=[pltpu.VMEM((D, H), BF16), pltpu.VMEM((D, H), BF16), pltpu.VMEM((H, D), BF16)],
    )
    return pl.pallas_call(
        _expert_kernel,
        grid_spec=grid_spec,
        out_shape=jax.ShapeDtypeStruct((cap, D), BF16),
        compiler_params=_cparams(("arbitrary",)),
        name="routed_experts",
    )(block_exp, n_used, xs, wg, wu, wd)


def _ffn_out_kernel(h_ref, ys_ref, wt_ref, x_ref, g2_ref, sg_ref, su_ref, sd_ref, o_ref):
    h = h_ref[...]
    a = jax.nn.silu(jnp.dot(h, sg_ref[...], preferred_element_type=F32))
    a = a * jnp.dot(h, su_ref[...], preferred_element_type=F32)
    acc = jnp.dot(a.astype(BF16), sd_ref[...], preferred_element_type=F32)
    wt = wt_ref[...]
    for k in range(TOP_K):
        acc = acc + ys_ref[k].astype(F32) * wt[:, k:k + 1]
    o_ref[...] = x_ref[...] + g2_ref[0] * acc


def ffn_out(h, ys_tok, wts, x, g2, sg, su, sd, n_ctx, S):
    T, D = x.shape
    H = sg.shape[-1]
    tm = FFN_TILE
    seg = functools.partial(_seg_of_tile, ctx_tiles=n_ctx // tm, tiles_per_batch=S // tm)
    row = pl.BlockSpec((tm, D), lambda i: (i, 0))
    return pl.pallas_call(
        _ffn_out_kernel,
        grid=(T // tm,),
        in_specs=[row,
                  pl.BlockSpec((TOP_K, tm, D), lambda i: (0, i, 0)),
                  pl.BlockSpec((tm, TOP_K), lambda i: (i, 0)),
                  row,
                  pl.BlockSpec((1, 1, D), lambda i: (seg(i), 0, 0)),
                  pl.BlockSpec((D, H), lambda i: (0, 0)),
                  pl.BlockSpec((D, H), lambda i: (0, 0)),
                  pl.BlockSpec((H, D), lambda i: (0, 0))],
        out_specs=row,
        out_shape=jax.ShapeDtypeStruct((T, D), F32),
        compiler_params=_cparams(("arbitrary",)),
        name="ffn_out",
    )(h, ys_tok, wts, x, g2, sg, su, sd)


def rms_norm(x, g):
    xf = x.astype(F32)
    y = xf * lax.rsqrt(jnp.mean(xf * xf, axis=-1, keepdims=True) + NORM_EPS)
    return (y * g.astype(F32)).astype(x.dtype)


def centred_dwconv(x, w, b):
    K, L = w.shape[0], x.shape[1]
    lo = (K - 1) // 2
    xp = jnp.pad(x, ((0, 0), (lo, K - 1 - lo), (0, 0)))
    y = b
    for j in range(K):
        y = y + xp[:, j:j + L] * w[j]
    return y


def to_heads(a, n_heads):
    B, L, W = a.shape
    return a.reshape(B, L, n_heads, W // n_heads).transpose(0, 2, 1, 3)


def from_heads(a):
    B, H, L, Dh = a.shape
    return a.transpose(0, 2, 1, 3).reshape(B, L, H * Dh)


def rope_2d(x, rows, cols):
    dh = x.shape[-1]
    half, quarter = dh // 2, dh // 4
    inv = ROPE_BASE ** (-jnp.arange(quarter, dtype=F32) / quarter)

    def rot(xa, pos):
        ang = pos.astype(F32)[:, None] * inv
        cos, sin = jnp.cos(ang).astype(xa.dtype), jnp.sin(ang).astype(xa.dtype)
        x1, x2 = xa[..., :quarter], xa[..., quarter:]
        return jnp.concatenate([x1 * cos - x2 * sin, x1 * sin + x2 * cos], axis=-1)
    return jnp.concatenate([rot(x[..., :half], rows), rot(x[..., half:], cols)], axis=-1)


def mlstm_scan(q, k, v, ig, fg, state):
    B, H, L, Dh = q.shape
    nc = L // ML_CHUNK

    def chunks(a):
        return jnp.moveaxis(a.reshape(B, H, nc, ML_CHUNK, *a.shape[3:]), 2, 0)
    lower = jnp.tril(jnp.ones((ML_CHUNK, ML_CHUNK), dtype=bool))

    def step(carry, xs):
        C, n, m = carry
        qc, kc, vc, ic, fc = xs
        b = jnp.cumsum(jax.nn.log_sigmoid(fc), axis=-1)
        d = jnp.where(lower, b[..., :, None] - b[..., None, :] + ic[..., None, :], -jnp.inf)
        m_t = jnp.maximum(b + m[..., None], jnp.max(d, axis=-1))
        w = jnp.exp(d - m_t[..., None]) * jnp.einsum('bhtd,bhsd->bhts', qc, kc)
        carry_w = jnp.exp(b + m[..., None] - m_t)
        num = jnp.einsum('bhts,bhsd->bhtd', w, vc) + carry_w[..., None] * jnp.einsum('bhtd,bhde->bhte', qc, C)
        den = jnp.sum(w, axis=-1) + carry_w * jnp.einsum('bhtd,bhd->bht', qc, n)
        h = num / jnp.maximum(jnp.abs(den), jnp.exp(-m_t))[..., None]
        g = b[..., -1:] - b + ic
        m_new = jnp.maximum(b[..., -1] + m, jnp.max(g, axis=-1))
        keep = jnp.exp(b[..., -1] + m - m_new)
        wg = jnp.exp(g - m_new[..., None])
        C = keep[..., None, None] * C + jnp.einsum('bhs,bhsd,bhse->bhde', wg, kc, vc)
        n = keep[..., None] * n + jnp.einsum('bhs,bhsd->bhd', wg, kc)
        return (C, n, m_new), h
    state, hs = lax.scan(step, state, (chunks(q), chunks(k), chunks(v), chunks(ig), chunks(fg)))
    return jnp.moveaxis(hs, 0, 2).reshape(B, H, L, Dh), state


def mlstm_branch(pc_qkv, pl_qkv, pc_o, pl_o, pc_if, pl_if, conv_w, conv_b, gate_b, out_g, rows, cols, need_ctx):
    def prep(p_qkv, p_if, rotary):
        q, k, v = jnp.split(p_qkv, 3, axis=-1)
        qk = jax.nn.silu(centred_dwconv(jnp.concatenate([q, k], axis=-1), conv_w, conv_b))
        q, k = jnp.split(qk, 2, axis=-1)
        q = to_heads(q, ML_HEADS).astype(F32)
        k = to_heads(k, ML_HEADS).astype(F32)
        v = to_heads(v, ML_HEADS).astype(F32)
        if rotary:
            q, k = rope_2d(q, rows, cols), rope_2d(k, rows, cols)
        k = k * (HEAD_DIM ** -0.5)
        B, L, _ = p_if.shape
        g = p_if.astype(F32).reshape(B, L, 2, 2, ML_HEADS) + gate_b
        return q, k, v, g.transpose(2, 3, 0, 4, 1)
    qc, kc, vc, gc = prep(pc_qkv, pc_if, False)
    ql, kl, vl, gl = prep(pl_qkv, pl_if, True)
    B = ql.shape[0]
    zero = (jnp.zeros((B, ML_HEADS, HEAD_DIM, HEAD_DIM), F32),
            jnp.zeros((B, ML_HEADS, HEAD_DIM), F32), jnp.zeros((B, ML_HEADS), F32))
    flip = lambda a: jnp.flip(a, axis=2)
    hc_f, st_f = mlstm_scan(qc, kc, vc, gc[0, 0], gc[0, 1], zero)
    hl_f, _ = mlstm_scan(ql, kl, vl, gl[0, 0], gl[0, 1], st_f)
    hc_b, st_b = mlstm_scan(flip(qc), flip(kc), flip(vc), flip(gc[1, 0]), flip(gc[1, 1]), zero)
    hl_b, _ = mlstm_scan(flip(ql), flip(kl), flip(vl), flip(gl[1, 0]), flip(gl[1, 1]), st_b)

    def readout(h, o):
        h = from_heads(rms_norm(h, out_g.reshape(ML_HEADS, 1, HEAD_DIM)))
        return (h * jax.nn.sigmoid(o.astype(F32))).astype(o.dtype)
    yl = readout(hl_f + flip(hl_b), pl_o)
    yc = readout(hc_f + flip(hc_b), pc_o) if need_ctx else None
    return yc, yl


def hyena_filters(L, w1, b1, w2, b2, w3, freq, decay):
    n = jnp.arange(L, dtype=F32)
    t = n / max(L - 1, 1)
    bands = (HY_EMB - 1) // 2
    f = jnp.linspace(1e-4, bands - 1, bands, dtype=F32)
    ang = (2.0 * math.pi / L) * n[:, None] * f
    z = jnp.concatenate([t[:, None], jnp.cos(ang), -jnp.sin(ang)], axis=-1)
    a = jnp.sin(freq * (z @ w1 + b1))
    a = jnp.sin(freq * (a @ w2 + b2))
    h = ((a @ w3) * jnp.exp(-t[:, None] * jnp.abs(decay))).astype(F32)
    h = h.reshape(L, HY_ORDER, 2, HY_WIDTH)
    l1 = jnp.sum(jnp.abs(h[:, :, 0]), axis=0) + jnp.sum(jnp.abs(h[1:, :, 1]), axis=0)
    return h / l1[None, :, None]


def long_conv(u, hf, hb, d):
    B, L, W = u.shape
    taps = jnp.concatenate([hf, jnp.zeros((1, W), hf.dtype), hb[:0:-1]], axis=0)
    y = jnp.fft.irfft(jnp.fft.rfft(u, n=2 * L, axis=1) * jnp.fft.rfft(taps, axis=0)[None], n=2 * L, axis=1)[:, :L]
    return y + u * d


def hyena_branch(pc, pl_, conv_w, conv_b, w1, b1, w2, b2, w3, freq, decay, dskip, need_ctx):
    def run(p):
        L = p.shape[1]
        u = centred_dwconv(p, conv_w, conv_b).astype(F32)
        parts = jnp.split(u, HY_ORDER + 1, axis=-1)
        filt = hyena_filters(L, w1, b1, w2, b2, w3, freq, decay)
        z = parts[0]
        for o in range(HY_ORDER):
            z = parts[o + 1] * long_conv(z, filt[:, o, 0], filt[:, o, 1], dskip[o])
        return z.astype(p.dtype)
    yl = run(pl_)
    yc = run(pc) if need_ctx else None
    return yc, yl


def _dft_tables(L):
    n = 2 * L
    n1 = n // HY_N2
    assert n == n1 * HY_N2 and n1 % HY_K1_TILE == 0
    a1 = 2.0 * np.pi * np.outer(np.arange(n1), np.arange(n1)) / n1
    a2 = 2.0 * np.pi * np.outer(np.arange(HY_N2), np.arange(HY_N2)) / HY_N2
    at = 2.0 * np.pi * np.outer(np.arange(n1), np.arange(HY_N2)) / n

    def pair(a, dt):
        return jnp.asarray(np.cos(a), dt), jnp.asarray(-np.sin(a), dt)
    f1r, f1i = pair(a1, BF16)
    f2r, f2i = pair(a2, BF16)
    twr, twi = pair(at, F32)
    return f1r, f1i, f2r, f2i, twr[:, :, None], twi[:, :, None]


def _hy_conv_kernel(x_ref, prev_ref, next_ref, cw_ref, cb_ref, o0_ref, o1_ref, o2_ref):
    j = pl.program_id(1)
    has_prev = jnp.where(j > 0, 1.0, 0.0)
    has_next = jnp.where(j < pl.num_programs(1) - 1, 1.0, 0.0)
    x = x_ref[...]
    tc, w = x.shape
    row = lax.broadcasted_iota(jnp.int32, (tc, w), 0)
    xm1 = jnp.where(row == 0, prev_ref[SUBLANES - 1:SUBLANES, :] * has_prev, pltpu.roll(x, 1, 0))
    xp1 = jnp.where(row == tc - 1, next_ref[0:1, :] * has_next, pltpu.roll(x, tc - 1, 0))
    u = cb_ref[...] + xm1 * cw_ref[0:1, :] + x * cw_ref[1:2, :] + xp1 * cw_ref[2:3, :]
    for o, o_ref in enumerate((o0_ref, o1_ref, o2_ref)):
        o_ref[...] = u[:, o * HY_WIDTH:(o + 1) * HY_WIDTH]


def hy_short_conv(p, conv_w, conv_b, B, S):
    T = p.shape[0]
    tc, w = ROW_TILE, (HY_ORDER + 1) * HY_WIDTH
    assert conv_w.shape[0] == 3 and HY_ORDER == 2 and HY_OFF % w == 0
    col, n_blk, first = HY_OFF // w, S // tc, (T - B * S) // tc
    halo = tc // SUBLANES

    def blk(b, j):
        return first + b * n_blk + j
    out = pl.BlockSpec((tc, HY_WIDTH), lambda b, j: (b * n_blk + j, 0))
    return pl.pallas_call(
        _hy_conv_kernel,
        grid=(B, n_blk),
        in_specs=[pl.BlockSpec((tc, w), lambda b, j: (blk(b, j), col)),
                  pl.BlockSpec((SUBLANES, w), lambda b, j: (blk(b, j) * halo - 1, col)),
                  pl.BlockSpec((SUBLANES, w), lambda b, j: (jnp.minimum((blk(b, j) + 1) * halo, T // SUBLANES - 1), col)),
                  pl.BlockSpec((3, w), lambda b, j: (0, 0)), pl.BlockSpec((1, w), lambda b, j: (0, 0))],
        out_specs=[out, out, out],
        out_shape=[jax.ShapeDtypeStruct((B * S, HY_WIDTH), F32)] * 3,
        compiler_params=_cparams(("arbitrary", "arbitrary")),
        name="hy_short_conv",
    )(p, p, p, conv_w, conv_b[None, :])


def _hy_rows_kernel(x_ref, fr_ref, fi_ref, ar_ref, ai_ref):
    xb = x_ref[0].astype(BF16)
    ar_ref[0] = jnp.dot(fr_ref[...], xb, preferred_element_type=F32)
    ai_ref[0] = jnp.dot(fi_ref[...], xb, preferred_element_type=F32)


def hy_dft_rows(x, fr, fi):
    G, K, cols = x.shape
    n1 = fr.shape[0]
    ct = min(HY_COL_TILE, cols)
    out = pl.BlockSpec((1, n1, ct), lambda g, j: (g, 0, j))
    return pl.pallas_call(
        _hy_rows_kernel,
        grid=(G, cols // ct),
        in_specs=[pl.BlockSpec((1, K, ct), lambda g, j: (g, 0, j)),
                  pl.BlockSpec((n1, K), lambda g, j: (0, 0)), pl.BlockSpec((n1, K), lambda g, j: (0, 0))],
        out_specs=[out, out],
        out_shape=[jax.ShapeDtypeStruct((G, n1, cols), F32)] * 2,
        compiler_params=_cparams(("arbitrary", "arbitrary")),
        name="hy_dft_rows",
    )(x, fr[:, :K], fi[:, :K])


def _hy_mid_kernel(ar_ref, ai_ref, twr_ref, twi_ref, f2r_ref, f2i_ref, *rest, conv):
    if conv:
        hr_ref, hi_ref, gr_ref, gi_ref = rest
    else:
        gr_ref, gi_ref = rest
    f2r, f2i = f2r_ref[...], f2i_ref[...]

    def mm(a, b):
        return jnp.dot(a, b, preferred_element_type=F32)
    for j in range(ar_ref.shape[1]):
        ar, ai = ar_ref[0, j], ai_ref[0, j]
        twr, twi = twr_ref[j], twi_ref[j]
        pr = (ar * twr - ai * twi).astype(BF16)
        pi = (ar * twi + ai * twr).astype(BF16)
        xr = mm(f2r, pr) - mm(f2i, pi)
        xi = mm(f2r, pi) + mm(f2i, pr)
        if not conv:
            gr_ref[0, j] = xr
            gi_ref[0, j] = xi
            continue
        hr, hi = hr_ref[0, j], hi_ref[0, j]
        yr = (xr * hr - xi * hi).astype(BF16)
        yi = (xr * hi + xi * hr).astype(BF16)
        gr = mm(f2r, yr) + mm(f2i, yi)
        gi = mm(f2r, yi) - mm(f2i, yr)
        gr_ref[0, j] = gr * twr + gi * twi
        gi_ref[0, j] = gi * twr - gr * twi


def hy_mid(ar, ai, tables, spectrum=None, order=0):
    G, n1 = ar.shape[0], ar.shape[1]
    C = ar.shape[-1]
    _, _, f2r, f2i, twr, twi = tables
    kt = HY_K1_TILE
    blk = pl.BlockSpec((1, kt, HY_N2, C), lambda g, i: (g, i, 0, 0))
    tw = pl.BlockSpec((kt, HY_N2, 1), lambda g, i: (i, 0, 0))
    mat = pl.BlockSpec((HY_N2, HY_N2), lambda g, i: (0, 0))
    ins, specs = [ar, ai, twr, twi, f2r, f2i], [blk, blk, tw, tw, mat, mat]
    if spectrum is not None:
        hspec = pl.BlockSpec((1, kt, HY_N2, C), lambda g, i: (order, i, 0, 0))
        ins, specs = ins + list(spectrum), specs + [hspec, hspec]
    return pl.pallas_call(
        functools.partial(_hy_mid_kernel, conv=spectrum is not None),
        grid=(G, n1 // kt),
        in_specs=specs,
        out_specs=[blk, blk],
        out_shape=[jax.ShapeDtypeStruct(ar.shape, F32)] * 2,
        compiler_params=_cparams(("arbitrary", "arbitrary")),
        name="hy_mid_conv" if spectrum is not None else "hy_mid_filter",
    )(*ins)


def _hy_out_kernel(gr_ref, gi_ref, fr_ref, fi_ref, z_ref, xm_ref, d_ref, o_ref, *, scale):
    y = jnp.dot(fr_ref[...], gr_ref[0].astype(BF16), preferred_element_type=F32)
    y = y + jnp.dot(fi_ref[...], gi_ref[0].astype(BF16), preferred_element_type=F32)
    o_ref[0] = xm_ref[0] * (y * scale + z_ref[0] * d_ref[...])


def hy_out(gr, gi, fr, fi, z, xmul, d_cols):
    G, n1, cols = gr.shape
    rows = z.shape[1]
    ct = min(HY_COL_TILE // 2, cols)
    spec = pl.BlockSpec((1, n1, ct), lambda g, j: (g, 0, j))
    half = pl.BlockSpec((1, rows, ct), lambda g, j: (g, 0, j))
    mat = pl.BlockSpec((rows, n1), lambda g, j: (0, 0))
    return pl.pallas_call(
        functools.partial(_hy_out_kernel, scale=1.0 / (n1 * HY_N2)),
        grid=(G, cols // ct),
        in_specs=[spec, spec, mat, mat, half, half, pl.BlockSpec((1, ct), lambda g, j: (0, j))],
        out_specs=half,
        out_shape=jax.ShapeDtypeStruct(z.shape, F32),
        compiler_params=_cparams(("arbitrary", "arbitrary")),
        name="hy_out",
    )(gr, gi, fr[:rows], fi[:rows], z, xmul, d_cols)


def hyena_taps(L, w1, b1, w2, b2, w3, freq, decay):
    n = jnp.arange(L, dtype=F32)
    t = n / max(L - 1, 1)
    bands = (HY_EMB - 1) // 2
    f = jnp.linspace(1e-4, bands - 1, bands, dtype=F32)
    ang = (2.0 * math.pi / L) * n[:, None] * f
    z = jnp.concatenate([t[:, None], jnp.cos(ang), -jnp.sin(ang)], axis=-1)
    a = jnp.sin(freq * (z @ w1 + b1))
    a = jnp.sin(freq * (a @ w2 + b2))
    w3r = w3.reshape(-1, HY_ORDER, 2, HY_WIDTH)
    dec = jnp.abs(decay).reshape(HY_ORDER, 2, HY_WIDTH)

    def side(a_, t_, d):
        return jnp.einsum('lf,foc->olc', a_, w3r[:, :, d]) * jnp.exp(-t_[None, :, None] * dec[:, d][:, None, :])
    hf = side(a, t, 0)
    hb = side(a[::-1], t[::-1], 1)[:, :L - 1]
    l1 = jnp.sum(jnp.abs(hf), axis=1) + jnp.sum(jnp.abs(hb), axis=1)
    taps = jnp.concatenate([hf, jnp.zeros((HY_ORDER, 1, HY_WIDTH), F32), hb], axis=1)
    return taps / l1[:, None, :]


def hyena_latents(p, conv_w, conv_b, taps, dskip, B, S):
    C = HY_WIDTH
    tables = _dft_tables(S)
    f1r, f1i = tables[0], tables[1]
    n1 = f1r.shape[0]
    cols = HY_N2 * C
    taps = taps.reshape(HY_ORDER, n1, cols)
    tr, ti = hy_dft_rows(taps, f1r, f1i)
    spectrum = hy_mid(tr.reshape(HY_ORDER, n1, HY_N2, C), ti.reshape(HY_ORDER, n1, HY_N2, C), tables)

    parts = hy_short_conv(p, conv_w, conv_b, B, S)
    rows = S // HY_N2
    z = parts[0].reshape(B, rows, cols)
    for o in range(HY_ORDER):
        ar, ai = hy_dft_rows(z, f1r, f1i)
        gr, gi = hy_mid(ar.reshape(B, n1, HY_N2, C), ai.reshape(B, n1, HY_N2, C), tables, spectrum, o)
        z = hy_out(gr.reshape(B, n1, cols), gi.reshape(B, n1, cols), f1r, f1i, z,
                   parts[o + 1].reshape(B, rows, cols), jnp.tile(dskip[o], HY_N2)[None, :])
    return z.reshape(B * S, C)


def hyena_context(pc, conv_w, conv_b, filt, dskip):
    L = pc.shape[1]
    u = centred_dwconv(pc, conv_w, conv_b)
    parts = jnp.split(u, HY_ORDER + 1, axis=-1)
    z = parts[0]
    for o in range(HY_ORDER):
        z = parts[o + 1] * long_conv(z, filt[:, o, 0], filt[:, o, 1], dskip[o])
    return z


def kernel(x, c, ctx, c_ctx, norm1_g, norm2_g, ada_w, ada_b, w_in, na_qnorm_g, na_knorm_g, na_rpb,
           ml_conv_w, ml_conv_b, ml_gate_b, ml_out_g, lru_conv_w, lru_conv_b, lru_wa, lru_ba, lru_wx, lru_bx,
           lru_lambda, hy_conv_w, hy_conv_b, hy_w1, hy_b1, hy_w2, hy_b2, hy_w3, hy_freq, hy_decay, hy_dskip,
           br_w, br_gate_b, w_o, router_w, router_b, exp_w_gate, exp_w_up, exp_w_down,
           sh_w_gate, sh_w_up, sh_w_down):
    B, S, D = x.shape
    assert ctx.shape[1] == CTX_LEN and D == D_MODEL
    n_ctx = B * CTX_LEN
    T = n_ctx + B * S
    assert n_ctx % ROW_TILE == 0 and S % ROW_TILE == 0 and T % ROUTE_TILE == 0
    tiles = dict(ctx_tiles=n_ctx // ROW_TILE, tiles_per_batch=S // ROW_TILE)
    pos = jnp.arange(S)
    rows, cols = pos // GRID_W, pos % GRID_W

    xa = jnp.concatenate([ctx.reshape(n_ctx, D), x.reshape(B * S, D)], axis=0)
    for l in range(DEPTH):
        need_ctx = l < DEPTH - 1
        cond = jnp.concatenate([c_ctx[None, :], c], axis=0)
        mod = (jax.nn.silu(cond) @ ada_w[l] + ada_b[l]).reshape(1 + B, 6, D)
        sh1sc1, g1, sh2sc2, g2 = mod[:, 0:2], mod[:, 2:3], mod[:, 3:5], mod[:, 5:6]

        w_l = w_in[l]
        w_main = jnp.concatenate([w_l[:, :IF_START], w_l[:, IF_START + IF_WIDTH:]], axis=1).astype(BF16)
        w_if = jnp.pad(w_l[:, IF_START:IF_START + IF_WIDTH], ((0, 0), (0, LANES - IF_WIDTH)))
        p = norm_mod_project(xa, norm1_g[l][None, :], sh1sc1, w_main, **tiles)
        p_if, _ = norm_mod_narrow(xa, norm1_g[l][None, :], sh1sc1, w_if, **tiles)

        ya = neighbourhood_attention(p, na_rpb[l], na_qnorm_g[l], na_knorm_g[l], B, S)
        yb = mlstm(p, p_if, ml_conv_w[l], ml_conv_b[l], ml_gate_b[l], ml_out_g[l], B, S)
        yc = rglru(p, lru_conv_w[l], lru_conv_b[l], lru_wa[l], lru_ba[l], lru_wx[l], lru_bx[l], lru_lambda[l], B, S)
        hy_par = (hy_w1[l], hy_b1[l], hy_w2[l], hy_b2[l], hy_w3[l], hy_freq[l], hy_decay[l])
        yd_lat = hyena_latents(p, hy_conv_w[l], hy_conv_b[l], hyena_taps(S, *hy_par), hy_dskip[l], B, S)
        if need_ctx:
            pc = p[:n_ctx, HY_OFF:GATE_OFF].reshape(B, CTX_LEN, -1)
            yd_ctx = hyena_context(pc, hy_conv_w[l], hy_conv_b[l], hyena_filters(CTX_LEN, *hy_par), hy_dskip[l])
            yd_ctx = yd_ctx.reshape(n_ctx, HY_WIDTH)
        else:
            yd_ctx = jnp.zeros((n_ctx, HY_WIDTH), F32)
        yd = jnp.concatenate([yd_ctx, yd_lat], axis=0)
        xa = merge_branches([ya, yb, yc, yd], p, br_gate_b[l], br_w[l].astype(BF16), w_o[l].astype(BF16), xa, g1,
                            **tiles)

        logits, h2 = norm_mod_narrow(xa, norm2_g[l][None, :], sh2sc2, router_w[l], **tiles)
        idx, wts, rank, counts = route_tokens(logits, router_b[l])
        slot_of, slot_tok, block_exp, n_used = slot_layout(idx, rank, counts, T)
        h2_pad = jnp.concatenate([h2, jnp.zeros((1, D), BF16)], axis=0)
        ys = routed_experts(block_exp, n_used, h2_pad[slot_tok], exp_w_gate[l], exp_w_up[l], exp_w_down[l])
        ys_tok = ys[slot_of.reshape(-1)].reshape(TOP_K, T, D)
        xa = ffn_out(h2, ys_tok, wts.T, xa, g2, sh_w_gate[l].astype(BF16), sh_w_up[l].astype(BF16),
                     sh_w_down[l].astype(BF16), n_ctx, S)
    return xa[n_ctx:].reshape(B, S, D)
```

```python
import functools
import math

import numpy as np
import jax
import jax.numpy as jnp
from jax import lax
from jax.experimental import pallas as pl
from jax.experimental.pallas import tpu as pltpu

D_MODEL = 2048
DEPTH = 2
CTX_LEN = 256
GRID_W = 64
NORM_EPS = 1e-6
N_BRANCH = 4
BRANCH_WIDTH = D_MODEL // 4
HEAD_DIM = 128

NA_HEADS = BRANCH_WIDTH // HEAD_DIM
NA_WIN_ROWS = 8
NA_WIN_COLS = 16

ML_HEADS = BRANCH_WIDTH // HEAD_DIM
ML_WIDTH = BRANCH_WIDTH
ML_CHUNK = 128
ROPE_BASE = 10000.0

LRU_WIDTH = BRANCH_WIDTH
LRU_BLOCKS = 4
LRU_BW = LRU_WIDTH // LRU_BLOCKS
LRU_C = 8.0
LRU_CONV = 4

HY_WIDTH = BRANCH_WIDTH
HY_ORDER = 2
HY_EMB = 33
HY_N2 = 256
HY_K1_TILE = 2
HY_COL_TILE = 8192

N_EXPERTS = 128
TOP_K = 8
N_GROUPS = 8
GROUP_SIZE = N_EXPERTS // N_GROUPS
TOPK_GROUPS = 4
ROUTE_SCALE = 2.5

IN_SPLITS = (3 * BRANCH_WIDTH, 3 * ML_WIDTH, ML_WIDTH, 4 * ML_HEADS, LRU_WIDTH, LRU_WIDTH,
             (HY_ORDER + 1) * HY_WIDTH, N_BRANCH * D_MODEL)
IF_START = sum(IN_SPLITS[:3])
IF_WIDTH = IN_SPLITS[3]
MAIN_SPLITS = IN_SPLITS[:3] + IN_SPLITS[4:]
MAIN_OFF = np.concatenate([[0], np.cumsum(MAIN_SPLITS)]).tolist()
MAIN_WIDTH = MAIN_OFF[-1]
NA_OFF, LRU_X_OFF, LRU_G_OFF, HY_OFF, GATE_OFF = MAIN_OFF[0], MAIN_OFF[3], MAIN_OFF[4], MAIN_OFF[5], MAIN_OFF[6]

LANES = 128
SUBLANES = 8
ROW_TILE = 512
PROJ_COL_TILE = 1024
MERGE_COL_TILE = 512
MOE_ROW_TILE = 512
FFN_TILE = 256
ROUTE_TILE = 512
SEQ_CHUNK = CTX_LEN
NA_G = 4
NEG = -1e30
VMEM_LIMIT = 56 * 1024 * 1024

F32 = jnp.float32
BF16 = jnp.bfloat16


def _cparams(sem):
    return pltpu.CompilerParams(dimension_semantics=sem, vmem_limit_bytes=VMEM_LIMIT)


def _seg_of_tile(i, ctx_tiles, tiles_per_batch):
    return jnp.where(i < ctx_tiles, 0, 1 + (i - ctx_tiles) // tiles_per_batch)


def _norm_mod(x, g, shift, scale):
    ms = jnp.mean(x * x, axis=-1, keepdims=True)
    return (x * lax.rsqrt(ms + NORM_EPS) * g) * (1.0 + scale) + shift


def _head_norm(x, g):
    return x * lax.rsqrt(jnp.mean(x * x, axis=-1, keepdims=True) + NORM_EPS) * g


def _proj_kernel(x_ref, g_ref, mod_ref, w_ref, o_ref, h_scr):
    @pl.when(pl.program_id(1) == 0)
    def _():
        h = _norm_mod(x_ref[...], g_ref[...], mod_ref[0, 0:1, :], mod_ref[0, 1:2, :])
        h_scr[...] = h.astype(BF16)
    o_ref[...] = jnp.dot(h_scr[...], w_ref[...], preferred_element_type=F32)


def norm_mod_project(x, g, mod, w, ctx_tiles, tiles_per_batch):
    T, D = x.shape
    N = w.shape[1]
    seg = functools.partial(_seg_of_tile, ctx_tiles=ctx_tiles, tiles_per_batch=tiles_per_batch)
    return pl.pallas_call(
        _proj_kernel,
        grid=(T // ROW_TILE, N // PROJ_COL_TILE),
        in_specs=[
            pl.BlockSpec((ROW_TILE, D), lambda i, j: (i, 0)),
            pl.BlockSpec((1, D), lambda i, j: (0, 0)),
            pl.BlockSpec((1, 2, D), lambda i, j: (seg(i), 0, 0)),
            pl.BlockSpec((D, PROJ_COL_TILE), lambda i, j: (0, j)),
        ],
        out_specs=pl.BlockSpec((ROW_TILE, PROJ_COL_TILE), lambda i, j: (i, j)),
        out_shape=jax.ShapeDtypeStruct((T, N), F32),
        scratch_shapes=[pltpu.VMEM((ROW_TILE, D), BF16)],
        compiler_params=_cparams(("arbitrary", "arbitrary")),
        name="norm_mod_project",
    )(x, g, mod, w)


def _narrow_kernel(x_ref, g_ref, mod_ref, w_ref, o_ref, h_ref):
    h = _norm_mod(x_ref[...], g_ref[...], mod_ref[0, 0:1, :], mod_ref[0, 1:2, :])
    h_ref[...] = h.astype(BF16)
    o_ref[...] = jnp.dot(h, w_ref[...], preferred_element_type=F32, precision=lax.Precision.HIGHEST)


def norm_mod_narrow(x, g, mod, w, ctx_tiles, tiles_per_batch):
    T, D = x.shape
    seg = functools.partial(_seg_of_tile, ctx_tiles=ctx_tiles, tiles_per_batch=tiles_per_batch)
    return pl.pallas_call(
        _narrow_kernel,
        grid=(T // ROW_TILE,),
        in_specs=[
            pl.BlockSpec((ROW_TILE, D), lambda i: (i, 0)),
            pl.BlockSpec((1, D), lambda i: (0, 0)),
            pl.BlockSpec((1, 2, D), lambda i: (seg(i), 0, 0)),
            pl.BlockSpec((D, LANES), lambda i: (0, 0)),
        ],
        out_specs=[pl.BlockSpec((ROW_TILE, LANES), lambda i: (i, 0)),
                   pl.BlockSpec((ROW_TILE, D), lambda i: (i, 0))],
        out_shape=[jax.ShapeDtypeStruct((T, LANES), F32), jax.ShapeDtypeStruct((T, D), BF16)],
        compiler_params=_cparams(("arbitrary",)),
        name="norm_mod_narrow",
    )(x, g, mod, w)


def _na_kernel(q_ref, k0_ref, k1_ref, k2_ref, v0_ref, v1_ref, v2_ref, kc_ref, vc_ref, bias_ref, qg_ref, kg_ref,
               o_ref, *, n_rows):
    s = pl.program_id(1)
    qb, kw = bias_ref.shape[1], bias_ref.shape[2]
    first_row = (s - 1) * NA_G
    q_row = first_row + lax.broadcasted_iota(jnp.int32, (qb, kw), 0) // GRID_W
    k_row = first_row - NA_G + lax.broadcasted_iota(jnp.int32, (qb, kw), 1) // GRID_W
    r0 = jnp.clip(q_row - NA_WIN_ROWS // 2, 0, n_rows - NA_WIN_ROWS)
    valid = (k_row >= r0) & (k_row < r0 + NA_WIN_ROWS) & (s > 0)
    nt = (((1,), (1,)), ((), ()))
    for h in range(NA_HEADS):
        hs = slice(h * HEAD_DIM, (h + 1) * HEAD_DIM)
        qn = (_head_norm(q_ref[:, hs], qg_ref[...]) * (HEAD_DIM ** -0.5)).astype(BF16)
        k_win = jnp.concatenate([k0_ref[:, hs], k1_ref[:, hs], k2_ref[:, hs]], axis=0)
        kn = _head_norm(k_win, kg_ref[...]).astype(BF16)
        s_lat = lax.dot_general(qn, kn, nt, preferred_element_type=F32)
        s_lat = jnp.where(valid, s_lat + bias_ref[h], NEG)
        kcn = _head_norm(kc_ref[:, hs], kg_ref[...]).astype(BF16)
        s_ctx = lax.dot_general(qn, kcn, nt, preferred_element_type=F32)
        m = jnp.maximum(jnp.max(s_lat, axis=-1, keepdims=True), jnp.max(s_ctx, axis=-1, keepdims=True))
        p_lat = jnp.exp(s_lat - m)
        p_ctx = jnp.exp(s_ctx - m)
        denom = jnp.sum(p_lat, axis=-1, keepdims=True) + jnp.sum(p_ctx, axis=-1, keepdims=True)
        v_win = jnp.concatenate([v0_ref[:, hs], v1_ref[:, hs], v2_ref[:, hs]], axis=0).astype(BF16)
        o = jnp.dot(p_lat.astype(BF16), v_win, preferred_element_type=F32)
        o = o + jnp.dot(p_ctx.astype(BF16), vc_ref[:, hs].astype(BF16), preferred_element_type=F32)
        o_ref[:, hs] = o / denom


def na_bias_table(rpb):
    W, k_rows = GRID_W, 3 * NA_G
    qb, kw = NA_G * W, k_rows * W
    qc = np.arange(qb)[:, None] % W
    kc = np.arange(kw)[None, :] % W
    c0 = np.clip(qc - NA_WIN_COLS // 2, 0, W - NA_WIN_COLS)
    ok = (kc >= c0) & (kc < c0 + NA_WIN_COLS)
    lo = W - NA_WIN_COLS
    padded = jnp.pad(rpb, ((0, 0), (0, 0), (lo, lo)))
    by_col = jnp.stack([padded[:, :, W - 1 - c:2 * W - 1 - c] for c in range(W)], axis=2)
    assert NA_WIN_ROWS - 1 - NA_G - (NA_G - 1) >= 0 and k_rows - 1 - NA_G + NA_WIN_ROWS - 1 <= 2 * NA_WIN_ROWS - 2
    per_g = [jnp.stack([by_col[:, kr - NA_G - g + NA_WIN_ROWS - 1] for kr in range(k_rows)], axis=2)
             for g in range(NA_G)]
    table = jnp.stack(per_g, axis=1).reshape(rpb.shape[0], qb, kw)
    return jnp.where(ok[None], table, NEG)


def neighbourhood_attention(p, rpb, qg, kg, B, S):
    T = p.shape[0]
    qb = NA_G * GRID_W
    assert qb == CTX_LEN and NA_WIN_ROWS // 2 <= NA_G and NA_WIN_ROWS - NA_WIN_ROWS // 2 - 1 <= NA_G
    ctx_blocks, n_blk, n_rows = B * CTX_LEN // qb, S // qb, S // GRID_W
    c_q = NA_OFF // BRANCH_WIDTH
    c_k, c_v = c_q + 1, c_q + 2

    def q_map(b, s):
        return (jnp.where(s == 0, b, ctx_blocks + b * n_blk + s - 1), c_q)

    def win_spec(d, col):
        return pl.BlockSpec((qb, BRANCH_WIDTH),
                            lambda b, s: (ctx_blocks + b * n_blk + jnp.clip(s - 1 + d, 0, n_blk - 1), col))

    bias = na_bias_table(rpb)
    return pl.pallas_call(
        functools.partial(_na_kernel, n_rows=n_rows),
        grid=(B, 1 + n_blk),
        in_specs=[pl.BlockSpec((qb, BRANCH_WIDTH), q_map),
                  win_spec(-1, c_k), win_spec(0, c_k), win_spec(1, c_k),
                  win_spec(-1, c_v), win_spec(0, c_v), win_spec(1, c_v),
                  pl.BlockSpec((qb, BRANCH_WIDTH), lambda b, s: (b, c_k)),
                  pl.BlockSpec((qb, BRANCH_WIDTH), lambda b, s: (b, c_v)),
                  pl.BlockSpec(bias.shape, lambda b, s: (0, 0, 0)),
                  pl.BlockSpec((1, HEAD_DIM), lambda b, s: (0, 0)),
                  pl.BlockSpec((1, HEAD_DIM), lambda b, s: (0, 0))],
        out_specs=pl.BlockSpec((qb, BRANCH_WIDTH), lambda b, s: (q_map(b, s)[0], 0)),
        out_shape=jax.ShapeDtypeStruct((T, BRANCH_WIDTH), F32),
        compiler_params=_cparams(("arbitrary", "arbitrary")),
        name="neighbourhood_attention",
    )(p, p, p, p, p, p, p, p, p, bias, qg[None, :], kg[None, :])


def _chunk_edges(reverse):
    s = pl.program_id(1)
    n_lat = pl.num_programs(1) - 1
    lat_idx = (n_lat - s) if reverse else (s - 1)
    has_prev = jnp.where((s > 0) & (lat_idx > 0), 1.0, 0.0)
    has_next = jnp.where((s > 0) & (lat_idx < n_lat - 1), 1.0, 0.0)
    return has_prev, has_next


def _conv4(x_ref, prev_ref, next_ref, cw_ref, cb_ref, has_prev, has_next):
    x = x_ref[...]
    tc, w = x.shape
    row = lax.broadcasted_iota(jnp.int32, (tc, w), 0)
    before = prev_ref[SUBLANES - 1:SUBLANES, :] * has_prev
    after0 = next_ref[0:1, :] * has_next
    after1 = next_ref[1:2, :] * has_next
    xm1 = jnp.where(row == 0, before, pltpu.roll(x, 1, 0))
    xp1 = jnp.where(row == tc - 1, after0, pltpu.roll(x, tc - 1, 0))
    xp2 = jnp.where(row == tc - 2, after0, jnp.where(row == tc - 1, after1, pltpu.roll(x, tc - 2, 0)))
    return cb_ref[...] + xm1 * cw_ref[0:1, :] + x * cw_ref[1:2, :] + xp1 * cw_ref[2:3, :] + xp2 * cw_ref[3:4, :]


def _seq_specs(reverse, B, S, T, width):
    tc = SEQ_CHUNK
    n_lat, ctx_blocks, halo = S // tc, B, tc // SUBLANES

    def blk(b, s):
        lat = (n_lat - s) if reverse else (s - 1)
        return jnp.where(s == 0, b, ctx_blocks + b * n_lat + lat)

    def chunk(col):
        return pl.BlockSpec((tc, width), lambda b, s: (blk(b, s), col))

    def prev(col):
        return pl.BlockSpec((SUBLANES, width), lambda b, s: (jnp.maximum(blk(b, s) * halo - 1, 0), col))

    def nxt(col):
        return pl.BlockSpec((SUBLANES, width), lambda b, s: (jnp.minimum((blk(b, s) + 1) * halo, T // SUBLANES - 1), col))

    def full(shape):
        return pl.BlockSpec(shape, lambda b, s: (0,) * len(shape))
    return blk, chunk, prev, nxt, full


def _lru_kernel(x_ref, prev_ref, next_ref, cw_ref, cb_ref, wa_ref, ba_ref, wx_ref, bx_ref, sp_ref, *rest, reverse):
    if reverse:
        hf_ref, g_ref, o_ref, carry = rest
    else:
        o_ref, carry = rest
    s = pl.program_id(1)
    tc, w = x_ref.shape

    @pl.when(s == 0)
    def _():
        carry[...] = jnp.zeros_like(carry)

    has_prev, has_next = _chunk_edges(reverse)
    xc = _conv4(x_ref, prev_ref, next_ref, cw_ref, cb_ref, has_prev, has_next)
    row = lax.broadcasted_iota(jnp.int32, (tc, w), 0)

    def block_diag(w_ref):
        return jnp.concatenate(
            [jnp.dot(xc[:, n * LRU_BW:(n + 1) * LRU_BW], w_ref[n], preferred_element_type=F32,
                     precision=lax.Precision.HIGHEST) for n in range(LRU_BLOCKS)], axis=-1)
    r = jax.nn.sigmoid(block_diag(wa_ref) + ba_ref[...])
    i = jax.nn.sigmoid(block_diag(wx_ref) + bx_ref[...])
    log_a = -LRU_C * r * sp_ref[...]
    a = jnp.exp(log_a)
    u = jnp.sqrt(1.0 - jnp.exp(2.0 * log_a)) * (i * xc)

    step = 1
    while step < tc:
        if reverse:
            keep = row < tc - step
            shift = tc - step
        else:
            keep = row >= step
            shift = step
        a_sh = jnp.where(keep, pltpu.roll(a, shift, 0), 1.0)
        u_sh = jnp.where(keep, pltpu.roll(u, shift, 0), 0.0)
        u = a * u_sh + u
        a = a * a_sh
        step *= 2
    h = u + a * carry[...]
    carry[...] = h[0:1, :] if reverse else h[tc - 1:tc, :]
    if reverse:
        o_ref[...] = (hf_ref[...] + h) * jax.nn.gelu(g_ref[...])
    else:
        o_ref[...] = h


def rglru(p, conv_w, conv_b, wa, ba, wx, bx, lam, B, S):
    T = p.shape[0]
    tc, w = SEQ_CHUNK, LRU_WIDTH
    assert CTX_LEN == tc and S % tc == 0 and LRU_CONV == 4
    c_x, c_g = LRU_X_OFF // w, LRU_G_OFF // w
    sp = jax.nn.softplus(-lam)

    def call(reverse, extra_in, extra_cols):
        blk, chunk, prev, nxt, full = _seq_specs(reverse, B, S, T, w)
        d = 1 if reverse else 0
        return pl.pallas_call(
            functools.partial(_lru_kernel, reverse=reverse),
            grid=(B, 1 + S // tc),
            in_specs=[chunk(c_x), prev(c_x), nxt(c_x),
                      full((LRU_CONV, w)), full((1, w)), full((LRU_BLOCKS, LRU_BW, LRU_BW)), full((1, w)),
                      full((LRU_BLOCKS, LRU_BW, LRU_BW)), full((1, w)), full((1, w))] + [chunk(c) for c in extra_cols],
            out_specs=chunk(0),
            out_shape=jax.ShapeDtypeStruct((T, w), F32),
            scratch_shapes=[pltpu.VMEM((1, w), F32)],
            compiler_params=_cparams(("arbitrary", "arbitrary")),
            name="rglru_bwd" if reverse else "rglru_fwd",
        )(p, p, p, conv_w, conv_b[None, :], wa[d], ba[d][None, :], wx[d], bx[d][None, :], sp[d][None, :], *extra_in)

    h_fwd = call(False, [], [])
    return call(True, [h_fwd, p], [0, c_g])


def _log_sigmoid(x):
    return jnp.minimum(x, 0.0) - jnp.log(1.0 + jnp.exp(-jnp.abs(x)))


def _mlstm_kernel(q_ref, qp_ref, qn_ref, k_ref, kp_ref, kn_ref, v_ref, if_ref, cos_ref, sin_ref, cwq_ref, cbq_ref,
                  cwk_ref, cbk_ref, gb_ref, *rest, reverse):
    if reverse:
        hf_ref, og_ref, outg_ref, o_ref, c_scr, n_scr, m_scr = rest
    else:
        o_ref, c_scr, n_scr, m_scr = rest
    s = pl.program_id(1)
    tc = q_ref.shape[0]
    ck = ML_CHUNK
    d0 = (1 if reverse else 0) * 2 * ML_HEADS

    @pl.when(s == 0)
    def _():
        c_scr[...] = jnp.zeros_like(c_scr)
        n_scr[...] = jnp.zeros_like(n_scr)
        m_scr[...] = jnp.zeros_like(m_scr)

    has_prev, has_next = _chunk_edges(reverse)
    q_all = jax.nn.silu(_conv4(q_ref, qp_ref, qn_ref, cwq_ref, cbq_ref, has_prev, has_next))
    k_all = jax.nn.silu(_conv4(k_ref, kp_ref, kn_ref, cwk_ref, cbk_ref, has_prev, has_next))

    is_lat = s > 0
    cos = jnp.where(is_lat, cos_ref[...], 1.0)
    sin = jnp.where(is_lat, sin_ref[...], 0.0)
    lane = lax.broadcasted_iota(jnp.int32, (tc, HEAD_DIM), 1)
    low_quarter = (lane % (HEAD_DIM // 2)) < (HEAD_DIM // 4)

    def rope(xh):
        swapped = jnp.where(low_quarter, pltpu.roll(xh, HEAD_DIM - HEAD_DIM // 4, 1), pltpu.roll(xh, HEAD_DIM // 4, 1))
        return xh * cos + swapped * sin

    row = lax.broadcasted_iota(jnp.int32, (ck, ck), 0)
    col = lax.broadcasted_iota(jnp.int32, (ck, ck), 1)
    causal = (col >= row) if reverse else (col <= row)
    nt = (((1,), (1,)), ((), ()))
    tn = (((0,), (0,)), ((), ()))

    qs, ks = [], []
    for h in range(ML_HEADS):
        hs = slice(h * HEAD_DIM, (h + 1) * HEAD_DIM)
        qs.append(rope(q_all[:, hs]))
        ks.append(rope(k_all[:, hs]) * (HEAD_DIM ** -0.5))

    subs = range(tc // ck)
    for sub in (reversed(subs) if reverse else subs):
        rs = slice(sub * ck, (sub + 1) * ck)
        gates = if_ref[rs, :] + gb_ref[...]
        b_all = _log_sigmoid(gates)
        step = 1
        while step < ck:
            if reverse:
                b_all = b_all + jnp.where(row < ck - step, pltpu.roll(b_all, ck - step, 0), 0.0)
            else:
                b_all = b_all + jnp.where(row >= step, pltpu.roll(b_all, step, 0), 0.0)
            step *= 2
        b_all_t = b_all.T
        gates_t = gates.T
        last = 0 if reverse else ck - 1
        for h in range(ML_HEADS):
            hs = slice(h * HEAD_DIM, (h + 1) * HEAD_DIM)
            ci, cf = d0 + h, d0 + ML_HEADS + h
            qh, kh, vh = qs[h][rs], ks[h][rs], v_ref[rs, hs]
            b_col, b_row = b_all[:, cf:cf + 1], b_all_t[cf:cf + 1, :]
            i_col, i_row = gates[:, ci:ci + 1], gates_t[ci:ci + 1, :]
            b_last = b_all[last:last + 1, cf:cf + 1]
            m_prev = m_scr[h:h + 1, 0:1]
            d = jnp.where(causal, b_col - b_row + i_row, -jnp.inf)
            m_t = jnp.maximum(b_col + m_prev, jnp.max(d, axis=-1, keepdims=True))
            qb, kb, vb = qh.astype(BF16), kh.astype(BF16), vh.astype(BF16)
            w = jnp.exp(d - m_t) * lax.dot_general(qb, kb, nt, preferred_element_type=F32)
            carry_w = jnp.exp(b_col + m_prev - m_t)
            num = jnp.dot(w.astype(BF16), vb, preferred_element_type=F32)
            num = num + carry_w * jnp.dot(qb, c_scr[h].astype(BF16), preferred_element_type=F32)
            den = jnp.sum(w, axis=-1, keepdims=True) + carry_w * jnp.sum(qh * n_scr[h:h + 1, :], axis=-1, keepdims=True)
            hh = num / jnp.maximum(jnp.abs(den), jnp.exp(-m_t))
            g_col = b_last - b_col + i_col
            m_new = jnp.maximum(b_last + m_prev, jnp.max(g_col, axis=0, keepdims=True))
            keep = jnp.exp(b_last + m_prev - m_new)
            kw = kh * jnp.exp(g_col - m_new)
            c_scr[h] = keep * c_scr[h] + lax.dot_general(kw.astype(BF16), vb, tn, preferred_element_type=F32)
            n_scr[h:h + 1, :] = keep * n_scr[h:h + 1, :] + jnp.sum(kw, axis=0, keepdims=True)
            m_scr[h:h + 1, :] = jnp.broadcast_to(m_new, (1, LANES))
            if reverse:
                hsum = hf_ref[rs, hs] + hh
                o_ref[rs, hs] = _head_norm(hsum, outg_ref[:, hs]) * jax.nn.sigmoid(og_ref[rs, hs])
            else:
                o_ref[rs, hs] = hh


def rope_tables(S):
    quarter = HEAD_DIM // 4
    inv = ROPE_BASE ** (-jnp.arange(quarter, dtype=F32) / quarter)
    pos = jnp.arange(S)
    ang_r = (pos // GRID_W).astype(F32)[:, None] * inv
    ang_c = (pos % GRID_W).astype(F32)[:, None] * inv
    cos = jnp.concatenate([jnp.cos(ang_r), jnp.cos(ang_r), jnp.cos(ang_c), jnp.cos(ang_c)], axis=-1)
    sin = jnp.concatenate([-jnp.sin(ang_r), jnp.sin(ang_r), -jnp.sin(ang_c), jnp.sin(ang_c)], axis=-1)
    return cos, sin


def mlstm(p, p_if, conv_w, conv_b, gate_b, out_g, B, S):
    T = p.shape[0]
    tc, w = SEQ_CHUNK, ML_WIDTH
    assert CTX_LEN == tc and S % tc == 0 and tc % ML_CHUNK == 0 and conv_w.shape[0] == 4
    c_q = MAIN_OFF[1] // w
    c_k, c_v, c_o = c_q + 1, c_q + 2, MAIN_OFF[2] // w
    cos, sin = rope_tables(S)
    gb = jnp.pad(gate_b.reshape(1, -1), ((0, 0), (0, LANES - gate_b.size)))
    n_lat = S // tc

    def call(reverse, extra_in, extra_specs):
        blk, chunk, prev, nxt, full = _seq_specs(reverse, B, S, T, w)

        def lat_rows(b, s):
            return (jnp.clip((n_lat - s) if reverse else (s - 1), 0, n_lat - 1), 0)
        table = pl.BlockSpec((tc, HEAD_DIM), lat_rows)
        return pl.pallas_call(
            functools.partial(_mlstm_kernel, reverse=reverse),
            grid=(B, 1 + n_lat),
            in_specs=[chunk(c_q), prev(c_q), nxt(c_q), chunk(c_k), prev(c_k), nxt(c_k), chunk(c_v),
                      pl.BlockSpec((tc, LANES), lambda b, s: (blk(b, s), 0)), table, table,
                      full((4, w)), full((1, w)), full((4, w)), full((1, w)), full((1, LANES))] + extra_specs(chunk, full),
            out_specs=chunk(0),
            out_shape=jax.ShapeDtypeStruct((T, w), F32),
            scratch_shapes=[pltpu.VMEM((ML_HEADS, HEAD_DIM, HEAD_DIM), F32), pltpu.VMEM((SUBLANES, HEAD_DIM), F32),
                            pltpu.VMEM((SUBLANES, LANES), F32)],
            compiler_params=_cparams(("arbitrary", "arbitrary")),
            name="mlstm_bwd" if reverse else "mlstm_fwd",
        )(p, p, p, p, p, p, p, p_if, cos, sin, conv_w[:, :w], conv_b[None, :w], conv_w[:, w:], conv_b[None, w:], gb,
          *extra_in)

    h_fwd = call(False, [], lambda chunk, full: [])
    return call(True, [h_fwd, p, out_g[None, :]], lambda chunk, full: [chunk(0), chunk(c_o), full((1, w))])


def _merge_kernel(ya_ref, yb_ref, yc_ref, yd_ref, ga_ref, gb_ref, gc_ref, gd_ref, bias_ref, bw_ref, wo_ref,
                  x_ref, g1_ref, o_ref, acc_ref):
    n = pl.program_id(1)
    z = None
    for b, (y_ref, gate_ref) in enumerate(((ya_ref, ga_ref), (yb_ref, gb_ref), (yc_ref, gc_ref), (yd_ref, gd_ref))):
        t = jnp.dot(y_ref[...].astype(BF16), bw_ref[b], preferred_element_type=F32)
        t = jax.nn.sigmoid(gate_ref[...] + bias_ref[b:b + 1, :]) * t
        z = t if z is None else z + t
    part = jnp.dot(z.astype(BF16), wo_ref[...], preferred_element_type=F32)

    @pl.when(n == 0)
    def _():
        acc_ref[...] = part

    @pl.when(n > 0)
    def _():
        acc_ref[...] += part

    @pl.when(n == pl.num_programs(1) - 1)
    def _():
        o_ref[...] = x_ref[...] + g1_ref[0] * acc_ref[...]


def merge_branches(ys, p, bias, bw, wo, x, g1, ctx_tiles, tiles_per_batch):
    T, D = x.shape
    ct = MERGE_COL_TILE
    seg = functools.partial(_seg_of_tile, ctx_tiles=ctx_tiles, tiles_per_batch=tiles_per_batch)
    y_spec = pl.BlockSpec((ROW_TILE, BRANCH_WIDTH), lambda i, n: (i, 0))

    def gate_spec(b):
        base = (GATE_OFF + b * D) // ct
        return pl.BlockSpec((ROW_TILE, ct), lambda i, n: (i, base + n))

    return pl.pallas_call(
        _merge_kernel,
        grid=(T // ROW_TILE, D // ct),
        in_specs=[y_spec, y_spec, y_spec, y_spec,
                  gate_spec(0), gate_spec(1), gate_spec(2), gate_spec(3),
                  pl.BlockSpec((N_BRANCH, ct), lambda i, n: (0, n)),
                  pl.BlockSpec((N_BRANCH, BRANCH_WIDTH, ct), lambda i, n: (0, 0, n)),
                  pl.BlockSpec((ct, D), lambda i, n: (n, 0)),
                  pl.BlockSpec((ROW_TILE, D), lambda i, n: (i, 0)),
                  pl.BlockSpec((1, 1, D), lambda i, n: (seg(i), 0, 0))],
        out_specs=pl.BlockSpec((ROW_TILE, D), lambda i, n: (i, 0)),
        out_shape=jax.ShapeDtypeStruct((T, D), F32),
        scratch_shapes=[pltpu.VMEM((ROW_TILE, D), F32)],
        compiler_params=_cparams(("arbitrary", "arbitrary")),
        name="merge_branches",
    )(*ys, p, p, p, p, bias, bw, wo, x, g1)


def _route_kernel(lg_ref, rb_ref, idx_ref, w_ref, rank_ref, cnt_ref, run_scr):
    @pl.when(pl.program_id(0) == 0)
    def _():
        run_scr[...] = jnp.zeros_like(run_scr)

    tm = lg_ref.shape[0]
    E = N_EXPERTS
    scores = jax.nn.sigmoid(lg_ref[...].T)
    biased = scores + rb_ref[...]
    ninf = -jnp.inf

    gscore = []
    for g in range(N_GROUPS):
        blk = biased[g * GROUP_SIZE:(g + 1) * GROUP_SIZE, :]
        m1 = jnp.max(blk, axis=0, keepdims=True)
        n_top = jnp.sum(jnp.where(blk == m1, 1.0, 0.0), axis=0, keepdims=True)
        m2 = jnp.max(jnp.where(blk < m1, blk, ninf), axis=0, keepdims=True)
        gscore.append(m1 + jnp.where(n_top >= 2.0, m1, m2))
    parts = []
    for g in range(N_GROUPS):
        beaten = jnp.zeros_like(gscore[g])
        for o in range(N_GROUPS):
            if o != g:
                wins = (gscore[o] >= gscore[g]) if o < g else (gscore[o] > gscore[g])
                beaten = beaten + jnp.where(wins, 1.0, 0.0)
        keep = beaten < float(TOPK_GROUPS)
        parts.append(jnp.where(keep, biased[g * GROUP_SIZE:(g + 1) * GROUP_SIZE, :], ninf))
    cur = jnp.concatenate(parts, axis=0)

    eidx = lax.broadcasted_iota(jnp.int32, (E, tm), 0).astype(F32)
    picks, wts = [], []
    sel = jnp.zeros((E, tm), F32)
    for _ in range(TOP_K):
        m = jnp.max(cur, axis=0, keepdims=True)
        ik = jnp.min(jnp.where(cur == m, eidx, float(E)), axis=0, keepdims=True)
        hit = eidx == ik
        picks.append(ik)
        wts.append(jnp.sum(jnp.where(hit, scores, 0.0), axis=0, keepdims=True))
        sel = sel + jnp.where(hit, 1.0, 0.0)
        cur = jnp.where(hit, ninf, cur)
    total = wts[0]
    for k in range(1, TOP_K):
        total = total + wts[k]

    before = lax.broadcasted_iota(jnp.int32, (tm, tm), 0) < lax.broadcasted_iota(jnp.int32, (tm, tm), 1)
    prefix = jnp.dot(sel.astype(BF16), jnp.where(before, 1.0, 0.0).astype(BF16), preferred_element_type=F32)
    base = prefix + run_scr[:, 0:1]
    for k in range(TOP_K):
        idx_ref[k:k + 1, :] = picks[k].astype(jnp.int32)
        w_ref[k:k + 1, :] = wts[k] / total * ROUTE_SCALE
        rank_ref[k:k + 1, :] = jnp.sum(jnp.where(eidx == picks[k], base, 0.0), axis=0, keepdims=True).astype(jnp.int32)
    run_scr[...] = run_scr[...] + jnp.sum(sel, axis=1, keepdims=True)
    cnt_ref[...] = run_scr[...]


def route_tokens(logits, router_b):
    T, E = logits.shape
    tm = ROUTE_TILE
    per_k = pl.BlockSpec((TOP_K, tm), lambda i: (0, i))
    idx, wts, rank, cnt = pl.pallas_call(
        _route_kernel,
        grid=(T // tm,),
        in_specs=[pl.BlockSpec((tm, E), lambda i: (i, 0)), pl.BlockSpec((E, 1), lambda i: (0, 0))],
        out_specs=[per_k, per_k, per_k, pl.BlockSpec((E, LANES), lambda i: (0, 0))],
        out_shape=[jax.ShapeDtypeStruct((TOP_K, T), jnp.int32), jax.ShapeDtypeStruct((TOP_K, T), F32),
                   jax.ShapeDtypeStruct((TOP_K, T), jnp.int32), jax.ShapeDtypeStruct((E, LANES), F32)],
        scratch_shapes=[pltpu.VMEM((E, LANES), F32)],
        compiler_params=_cparams(("arbitrary",)),
        name="route_tokens",
    )(logits, router_b[:, None])
    return idx, wts, rank, cnt[:, 0].astype(jnp.int32)


def slot_layout(idx, rank, counts, T):
    E, tb = N_EXPERTS, MOE_ROW_TILE
    padded = (counts + tb - 1) // tb * tb
    pend = jnp.cumsum(padded)
    first = pend - padded
    experts = jnp.arange(E, dtype=jnp.int32)
    slot_of = jnp.sum(jnp.where(idx[:, :, None] == experts, first, 0), axis=-1) + rank
    n_blocks = -(-T * TOP_K // tb) + E
    tok = jnp.broadcast_to(jnp.arange(T, dtype=jnp.int32), (TOP_K, T))
    slot_tok = jnp.full((n_blocks * tb,), T, jnp.int32).at[slot_of.reshape(-1)].set(tok.reshape(-1))
    block_exp = jnp.minimum(jnp.searchsorted(pend, jnp.arange(n_blocks) * tb, side='right'), E - 1).astype(jnp.int32)
    n_used = (pend[-1] // tb).astype(jnp.int32).reshape(1)
    return slot_of, slot_tok, block_exp, n_used


def _expert_kernel(be_ref, nb_ref, xs_ref, wg_ref, wu_ref, wd_ref, o_ref, wg_s, wu_s, wd_s):
    i = pl.program_id(0)
    live = i < nb_ref[0]

    @pl.when(live & ((i == 0) | (be_ref[i] != be_ref[jnp.maximum(i - 1, 0)])))
    def _():
        wg_s[...] = wg_ref[0, 0].astype(BF16)
        wu_s[...] = wu_ref[0, 0].astype(BF16)
        wd_s[...] = wd_ref[0, 0].astype(BF16)

    @pl.when(live)
    def _():
        xb = xs_ref[...]
        a = jax.nn.silu(jnp.dot(xb, wg_s[...], preferred_element_type=F32))
        a = a * jnp.dot(xb, wu_s[...], preferred_element_type=F32)
        o_ref[...] = jnp.dot(a.astype(BF16), wd_s[...], preferred_element_type=F32).astype(o_ref.dtype)

    @pl.when(i >= nb_ref[0])
    def _():
        o_ref[...] = jnp.zeros_like(o_ref)


def routed_experts(block_exp, n_used, xs, wg, wu, wd, layer):
    cap, D = xs.shape
    H = wg.shape[-1]
    tb = MOE_ROW_TILE

    def row(i, be, nb):
        return (jnp.minimum(i, nb[0] - 1), 0)

    def wsel(i, be, nb):
        return (layer, be[jnp.minimum(i, nb[0] - 1)], 0, 0)

    grid_spec = pltpu.PrefetchScalarGridSpec(
        num_scalar_prefetch=2,
        grid=(cap // tb,),
        in_specs=[pl.BlockSpec((tb, D), row),
                  pl.BlockSpec((1, 1, D, H), wsel),
                  pl.BlockSpec((1, 1, D, H), wsel),
                  pl.BlockSpec((1, 1, H, D), wsel)],
        out_specs=pl.BlockSpec((tb, D), lambda i, be, nb: (i, 0)),
        scratch_shapes=[pltpu.VMEM((D, H), BF16), pltpu.VMEM((D, H), BF16), pltpu.VMEM((H, D), BF16)],
    )
    return pl.pallas_call(
        _expert_kernel,
        grid_spec=grid_spec,
        out_shape=jax.ShapeDtypeStruct((cap, D), BF16),
        compiler_params=_cparams(("arbitrary",)),
        name="routed_experts",
    )(block_exp, n_used, xs, wg, wu, wd)


def _ffn_out_kernel(h_ref, ys_ref, wt_ref, x_ref, g2_ref, sg_ref, su_ref, sd_ref, o_ref):
    h = h_ref[...]
    a = jax.nn.silu(jnp.dot(h, sg_ref[...], preferred_element_type=F32))
    a = a * jnp.dot(h, su_ref[...], preferred_element_type=F32)
    acc = jnp.dot(a.astype(BF16), sd_ref[...], preferred_element_type=F32)
    wt = wt_ref[...]
    for k in range(TOP_K):
        acc = acc + ys_ref[k].astype(F32) * wt[:, k:k + 1]
    o_ref[...] = x_ref[...] + g2_ref[0] * acc


def ffn_out(h, ys_tok, wts, x, g2, sg, su, sd, n_ctx, S):
    T, D = x.shape
    H = sg.shape[-1]
    tm = FFN_TILE
    seg = functools.partial(_seg_of_tile, ctx_tiles=n_ctx // tm, tiles_per_batch=S // tm)
    row = pl.BlockSpec((tm, D), lambda i: (i, 0))
    return pl.pallas_call(
        _ffn_out_kernel,
        grid=(T // tm,),
        in_specs=[row,
                  pl.BlockSpec((TOP_K, tm, D), lambda i: (0, i, 0)),
                  pl.BlockSpec((tm, TOP_K), lambda i: (i, 0)),
                  row,
                  pl.BlockSpec((1, 1, D), lambda i: (seg(i), 0, 0)),
                  pl.BlockSpec((D, H), lambda i: (0, 0)),
                  pl.BlockSpec((D, H), lambda i: (0, 0)),
                  pl.BlockSpec((H, D), lambda i: (0, 0))],
        out_specs=row,
        out_shape=jax.ShapeDtypeStruct((T, D), F32),
        compiler_params=_cparams(("arbitrary",)),
        name="ffn_out",
    )(h, ys_tok, wts, x, g2, sg, su, sd)


def rms_norm(x, g):
    xf = x.astype(F32)
    y = xf * lax.rsqrt(jnp.mean(xf * xf, axis=-1, keepdims=True) + NORM_EPS)
    return (y * g.astype(F32)).astype(x.dtype)


def centred_dwconv(x, w, b):
    K, L = w.shape[0], x.shape[1]
    lo = (K - 1) // 2
    xp = jnp.pad(x, ((0, 0), (lo, K - 1 - lo), (0, 0)))
    y = b
    for j in range(K):
        y = y + xp[:, j:j + L] * w[j]
    return y


def to_heads(a, n_heads):
    B, L, W = a.shape
    return a.reshape(B, L, n_heads, W // n_heads).transpose(0, 2, 1, 3)


def from_heads(a):
    B, H, L, Dh = a.shape
    return a.transpose(0, 2, 1, 3).reshape(B, L, H * Dh)


def rope_2d(x, rows, cols):
    dh = x.shape[-1]
    half, quarter = dh // 2, dh // 4
    inv = ROPE_BASE ** (-jnp.arange(quarter, dtype=F32) / quarter)

    def rot(xa, pos):
        ang = pos.astype(F32)[:, None] * inv
        cos, sin = jnp.cos(ang).astype(xa.dtype), jnp.sin(ang).astype(xa.dtype)
        x1, x2 = xa[..., :quarter], xa[..., quarter:]
        return jnp.concatenate([x1 * cos - x2 * sin, x1 * sin + x2 * cos], axis=-1)
    return jnp.concatenate([rot(x[..., :half], rows), rot(x[..., half:], cols)], axis=-1)


def mlstm_scan(q, k, v, ig, fg, state):
    B, H, L, Dh = q.shape
    nc = L // ML_CHUNK

    def chunks(a):
        return jnp.moveaxis(a.reshape(B, H, nc, ML_CHUNK, *a.shape[3:]), 2, 0)
    lower = jnp.tril(jnp.ones((ML_CHUNK, ML_CHUNK), dtype=bool))

    def step(carry, xs):
        C, n, m = carry
        qc, kc, vc, ic, fc = xs
        b = jnp.cumsum(jax.nn.log_sigmoid(fc), axis=-1)
        d = jnp.where(lower, b[..., :, None] - b[..., None, :] + ic[..., None, :], -jnp.inf)
        m_t = jnp.maximum(b + m[..., None], jnp.max(d, axis=-1))
        w = jnp.exp(d - m_t[..., None]) * jnp.einsum('bhtd,bhsd->bhts', qc, kc)
        carry_w = jnp.exp(b + m[..., None] - m_t)
        num = jnp.einsum('bhts,bhsd->bhtd', w, vc) + carry_w[..., None] * jnp.einsum('bhtd,bhde->bhte', qc, C)
        den = jnp.sum(w, axis=-1) + carry_w * jnp.einsum('bhtd,bhd->bht', qc, n)
        h = num / jnp.maximum(jnp.abs(den), jnp.exp(-m_t))[..., None]
        g = b[..., -1:] - b + ic
        m_new = jnp.maximum(b[..., -1] + m, jnp.max(g, axis=-1))
        keep = jnp.exp(b[..., -1] + m - m_new)
        wg = jnp.exp(g - m_new[..., None])
        C = keep[..., None, None] * C + jnp.einsum('bhs,bhsd,bhse->bhde', wg, kc, vc)
        n = keep[..., None] * n + jnp.einsum('bhs,bhsd->bhd', wg, kc)
        return (C, n, m_new), h
    state, hs = lax.scan(step, state, (chunks(q), chunks(k), chunks(v), chunks(ig), chunks(fg)))
    return jnp.moveaxis(hs, 0, 2).reshape(B, H, L, Dh), state


def mlstm_branch(pc_qkv, pl_qkv, pc_o, pl_o, pc_if, pl_if, conv_w, conv_b, gate_b, out_g, rows, cols, need_ctx):
    def prep(p_qkv, p_if, rotary):
        q, k, v = jnp.split(p_qkv, 3, axis=-1)
        qk = jax.nn.silu(centred_dwconv(jnp.concatenate([q, k], axis=-1), conv_w, conv_b))
        q, k = jnp.split(qk, 2, axis=-1)
        q = to_heads(q, ML_HEADS).astype(F32)
        k = to_heads(k, ML_HEADS).astype(F32)
        v = to_heads(v, ML_HEADS).astype(F32)
        if rotary:
            q, k = rope_2d(q, rows, cols), rope_2d(k, rows, cols)
        k = k * (HEAD_DIM ** -0.5)
        B, L, _ = p_if.shape
        g = p_if.astype(F32).reshape(B, L, 2, 2, ML_HEADS) + gate_b
        return q, k, v, g.transpose(2, 3, 0, 4, 1)
    qc, kc, vc, gc = prep(pc_qkv, pc_if, False)
    ql, kl, vl, gl = prep(pl_qkv, pl_if, True)
    B = ql.shape[0]
    zero = (jnp.zeros((B, ML_HEADS, HEAD_DIM, HEAD_DIM), F32),
            jnp.zeros((B, ML_HEADS, HEAD_DIM), F32), jnp.zeros((B, ML_HEADS), F32))
    flip = lambda a: jnp.flip(a, axis=2)
    hc_f, st_f = mlstm_scan(qc, kc, vc, gc[0, 0], gc[0, 1], zero)
    hl_f, _ = mlstm_scan(ql, kl, vl, gl[0, 0], gl[0, 1], st_f)
    hc_b, st_b = mlstm_scan(flip(qc), flip(kc), flip(vc), flip(gc[1, 0]), flip(gc[1, 1]), zero)
    hl_b, _ = mlstm_scan(flip(ql), flip(kl), flip(vl), flip(gl[1, 0]), flip(gl[1, 1]), st_b)

    def readout(h, o):
        h = from_heads(rms_norm(h, out_g.reshape(ML_HEADS, 1, HEAD_DIM)))
        return (h * jax.nn.sigmoid(o.astype(F32))).astype(o.dtype)
    yl = readout(hl_f + flip(hl_b), pl_o)
    yc = readout(hc_f + flip(hc_b), pc_o) if need_ctx else None
    return yc, yl


def hyena_filters(L, w1, b1, w2, b2, w3, freq, decay):
    n = jnp.arange(L, dtype=F32)
    t = n / max(L - 1, 1)
    bands = (HY_EMB - 1) // 2
    f = jnp.linspace(1e-4, bands - 1, bands, dtype=F32)
    ang = (2.0 * math.pi / L) * n[:, None] * f
    z = jnp.concatenate([t[:, None], jnp.cos(ang), -jnp.sin(ang)], axis=-1)
    a = jnp.sin(freq * (z @ w1 + b1))
    a = jnp.sin(freq * (a @ w2 + b2))
    h = ((a @ w3) * jnp.exp(-t[:, None] * jnp.abs(decay))).astype(F32)
    h = h.reshape(L, HY_ORDER, 2, HY_WIDTH)
    l1 = jnp.sum(jnp.abs(h[:, :, 0]), axis=0) + jnp.sum(jnp.abs(h[1:, :, 1]), axis=0)
    return h / l1[None, :, None]


def long_conv(u, hf, hb, d):
    B, L, W = u.shape
    taps = jnp.concatenate([hf, jnp.zeros((1, W), hf.dtype), hb[:0:-1]], axis=0)
    y = jnp.fft.irfft(jnp.fft.rfft(u, n=2 * L, axis=1) * jnp.fft.rfft(taps, axis=0)[None], n=2 * L, axis=1)[:, :L]
    return y + u * d


def hyena_branch(pc, pl_, conv_w, conv_b, w1, b1, w2, b2, w3, freq, decay, dskip, need_ctx):
    def run(p):
        L = p.shape[1]
        u = centred_dwconv(p, conv_w, conv_b).astype(F32)
        parts = jnp.split(u, HY_ORDER + 1, axis=-1)
        filt = hyena_filters(L, w1, b1, w2, b2, w3, freq, decay)
        z = parts[0]
        for o in range(HY_ORDER):
            z = parts[o + 1] * long_conv(z, filt[:, o, 0], filt[:, o, 1], dskip[o])
        return z.astype(p.dtype)
    yl = run(pl_)
    yc = run(pc) if need_ctx else None
    return yc, yl


def _dft_tables(L):
    n = 2 * L
    n1 = n // HY_N2
    assert n == n1 * HY_N2 and n1 % HY_K1_TILE == 0
    a1 = 2.0 * np.pi * np.outer(np.arange(n1), np.arange(n1)) / n1
    a2 = 2.0 * np.pi * np.outer(np.arange(HY_N2), np.arange(HY_N2)) / HY_N2
    at = 2.0 * np.pi * np.outer(np.arange(n1), np.arange(HY_N2)) / n

    def pair(a, dt):
        return jnp.asarray(np.cos(a), dt), jnp.asarray(-np.sin(a), dt)
    f1r, f1i = pair(a1, BF16)
    f2r, f2i = pair(a2, BF16)
    twr, twi = pair(at, F32)
    return f1r, f1i, f2r, f2i, twr[:, :, None], twi[:, :, None]


def _hy_conv_kernel(x_ref, prev_ref, next_ref, cw_ref, cb_ref, o0_ref, o1_ref, o2_ref):
    j = pl.program_id(1)
    has_prev = jnp.where(j > 0, 1.0, 0.0)
    has_next = jnp.where(j < pl.num_programs(1) - 1, 1.0, 0.0)
    x = x_ref[...]
    tc, w = x.shape
    row = lax.broadcasted_iota(jnp.int32, (tc, w), 0)
    xm1 = jnp.where(row == 0, prev_ref[SUBLANES - 1:SUBLANES, :] * has_prev, pltpu.roll(x, 1, 0))
    xp1 = jnp.where(row == tc - 1, next_ref[0:1, :] * has_next, pltpu.roll(x, tc - 1, 0))
    u = cb_ref[...] + xm1 * cw_ref[0:1, :] + x * cw_ref[1:2, :] + xp1 * cw_ref[2:3, :]
    for o, o_ref in enumerate((o0_ref, o1_ref, o2_ref)):
        o_ref[...] = u[:, o * HY_WIDTH:(o + 1) * HY_WIDTH]


def hy_short_conv(p, conv_w, conv_b, B, S):
    T = p.shape[0]
    tc, w = ROW_TILE, (HY_ORDER + 1) * HY_WIDTH
    assert conv_w.shape[0] == 3 and HY_ORDER == 2 and HY_OFF % w == 0
    col, n_blk, first = HY_OFF // w, S // tc, (T - B * S) // tc
    halo = tc // SUBLANES

    def blk(b, j):
        return first + b * n_blk + j
    out = pl.BlockSpec((tc, HY_WIDTH), lambda b, j: (b * n_blk + j, 0))
    return pl.pallas_call(
        _hy_conv_kernel,
        grid=(B, n_blk),
        in_specs=[pl.BlockSpec((tc, w), lambda b, j: (blk(b, j), col)),
                  pl.BlockSpec((SUBLANES, w), lambda b, j: (blk(b, j) * halo - 1, col)),
                  pl.BlockSpec((SUBLANES, w), lambda b, j: (jnp.minimum((blk(b, j) + 1) * halo, T // SUBLANES - 1), col)),
                  pl.BlockSpec((3, w), lambda b, j: (0, 0)), pl.BlockSpec((1, w), lambda b, j: (0, 0))],
        out_specs=[out, out, out],
        out_shape=[jax.ShapeDtypeStruct((B * S, HY_WIDTH), F32)] * 3,
        compiler_params=_cparams(("arbitrary", "arbitrary")),
        name="hy_short_conv",
    )(p, p, p, conv_w, conv_b[None, :])


def _hy_rows_kernel(x_ref, fr_ref, fi_ref, ar_ref, ai_ref):
    xb = x_ref[0].astype(BF16)
    ar_ref[0] = jnp.dot(fr_ref[...], xb, preferred_element_type=F32)
    ai_ref[0] = jnp.dot(fi_ref[...], xb, preferred_element_type=F32)


def hy_dft_rows(x, fr, fi):
    G, K, cols = x.shape
    n1 = fr.shape[0]
    ct = min(HY_COL_TILE, cols)
    out = pl.BlockSpec((1, n1, ct), lambda g, j: (g, 0, j))
    return pl.pallas_call(
        _hy_rows_kernel,
        grid=(G, cols // ct),
        in_specs=[pl.BlockSpec((1, K, ct), lambda g, j: (g, 0, j)),
                  pl.BlockSpec((n1, K), lambda g, j: (0, 0)), pl.BlockSpec((n1, K), lambda g, j: (0, 0))],
        out_specs=[out, out],
        out_shape=[jax.ShapeDtypeStruct((G, n1, cols), F32)] * 2,
        compiler_params=_cparams(("arbitrary", "arbitrary")),
        name="hy_dft_rows",
    )(x, fr[:, :K], fi[:, :K])


def _hy_mid_kernel(ar_ref, ai_ref, twr_ref, twi_ref, f2r_ref, f2i_ref, *rest, conv):
    if conv:
        hr_ref, hi_ref, gr_ref, gi_ref = rest
    else:
        gr_ref, gi_ref = rest
    f2r, f2i = f2r_ref[...], f2i_ref[...]

    def mm(a, b):
        return jnp.dot(a, b, preferred_element_type=F32)
    for j in range(ar_ref.shape[1]):
        ar, ai = ar_ref[0, j], ai_ref[0, j]
        twr, twi = twr_ref[j], twi_ref[j]
        pr = (ar * twr - ai * twi).astype(BF16)
        pi = (ar * twi + ai * twr).astype(BF16)
        xr = mm(f2r, pr) - mm(f2i, pi)
        xi = mm(f2r, pi) + mm(f2i, pr)
        if not conv:
            gr_ref[0, j] = xr
            gi_ref[0, j] = xi
            continue
        hr, hi = hr_ref[0, j], hi_ref[0, j]
        yr = (xr * hr - xi * hi).astype(BF16)
        yi = (xr * hi + xi * hr).astype(BF16)
        gr = mm(f2r, yr) + mm(f2i, yi)
        gi = mm(f2r, yi) - mm(f2i, yr)
        gr_ref[0, j] = gr * twr + gi * twi
        gi_ref[0, j] = gi * twr - gr * twi


def hy_mid(ar, ai, tables, spectrum=None, order=0):
    G, n1 = ar.shape[0], ar.shape[1]
    C = ar.shape[-1]
    _, _, f2r, f2i, twr, twi = tables
    kt = HY_K1_TILE
    blk = pl.BlockSpec((1, kt, HY_N2, C), lambda g, i: (g, i, 0, 0))
    tw = pl.BlockSpec((kt, HY_N2, 1), lambda g, i: (i, 0, 0))
    mat = pl.BlockSpec((HY_N2, HY_N2), lambda g, i: (0, 0))
    ins, specs = [ar, ai, twr, twi, f2r, f2i], [blk, blk, tw, tw, mat, mat]
    if spectrum is not None:
        hspec = pl.BlockSpec((1, kt, HY_N2, C), lambda g, i: (order, i, 0, 0))
        ins, specs = ins + list(spectrum), specs + [hspec, hspec]
    return pl.pallas_call(
        functools.partial(_hy_mid_kernel, conv=spectrum is not None),
        grid=(G, n1 // kt),
        in_specs=specs,
        out_specs=[blk, blk],
        out_shape=[jax.ShapeDtypeStruct(ar.shape, F32)] * 2,
        compiler_params=_cparams(("arbitrary", "arbitrary")),
        name="hy_mid_conv" if spectrum is not None else "hy_mid_filter",
    )(*ins)


def _hy_out_kernel(gr_ref, gi_ref, fr_ref, fi_ref, z_ref, xm_ref, d_ref, o_ref, *, scale):
    y = jnp.dot(fr_ref[...], gr_ref[0].astype(BF16), preferred_element_type=F32)
    y = y + jnp.dot(fi_ref[...], gi_ref[0].astype(BF16), preferred_element_type=F32)
    o_ref[0] = xm_ref[0] * (y * scale + z_ref[0] * d_ref[...])


def hy_out(gr, gi, fr, fi, z, xmul, d_cols):
    G, n1, cols = gr.shape
    rows = z.shape[1]
    ct = min(HY_COL_TILE // 2, cols)
    spec = pl.BlockSpec((1, n1, ct), lambda g, j: (g, 0, j))
    half = pl.BlockSpec((1, rows, ct), lambda g, j: (g, 0, j))
    mat = pl.BlockSpec((rows, n1), lambda g, j: (0, 0))
    return pl.pallas_call(
        functools.partial(_hy_out_kernel, scale=1.0 / (n1 * HY_N2)),
        grid=(G, cols // ct),
        in_specs=[spec, spec, mat, mat, half, half, pl.BlockSpec((1, ct), lambda g, j: (0, j))],
        out_specs=half,
        out_shape=jax.ShapeDtypeStruct(z.shape, F32),
        compiler_params=_cparams(("arbitrary", "arbitrary")),
        name="hy_out",
    )(gr, gi, fr[:rows], fi[:rows], z, xmul, d_cols)


def hyena_taps(L, w1, b1, w2, b2, w3, freq, decay):
    n = jnp.arange(L, dtype=F32)
    t = n / max(L - 1, 1)
    bands = (HY_EMB - 1) // 2
    f = jnp.linspace(1e-4, bands - 1, bands, dtype=F32)
    ang = (2.0 * math.pi / L) * n[:, None] * f
    z = jnp.concatenate([t[:, None], jnp.cos(ang), -jnp.sin(ang)], axis=-1)
    a = jnp.sin(freq * (z @ w1 + b1))
    a = jnp.sin(freq * (a @ w2 + b2))
    w3r = w3.reshape(-1, HY_ORDER, 2, HY_WIDTH)
    dec = jnp.abs(decay).reshape(HY_ORDER, 2, HY_WIDTH)
    a_circ = jnp.concatenate([a, jnp.zeros((1, a.shape[1]), F32), a[:0:-1]], axis=0)
    t_circ = jnp.concatenate([t, jnp.zeros((1,), F32), t[:0:-1]])
    forward = (jnp.arange(2 * L) < L)[None, :, None]

    def side(d):
        return jnp.einsum('lf,foc->olc', a_circ, w3r[:, :, d]) * jnp.exp(-t_circ[None, :, None] * dec[:, d][:, None, :])
    taps = jnp.where(forward, side(0), side(1))
    return taps / jnp.sum(jnp.abs(taps), axis=1, keepdims=True)


def hyena_latents(p, conv_w, conv_b, taps, dskip, B, S):
    C = HY_WIDTH
    tables = _dft_tables(S)
    f1r, f1i = tables[0], tables[1]
    n1 = f1r.shape[0]
    cols = HY_N2 * C
    taps = taps.reshape(HY_ORDER, n1, cols)
    tr, ti = hy_dft_rows(taps, f1r, f1i)
    spectrum = hy_mid(tr.reshape(HY_ORDER, n1, HY_N2, C), ti.reshape(HY_ORDER, n1, HY_N2, C), tables)

    parts = hy_short_conv(p, conv_w, conv_b, B, S)
    rows = S // HY_N2
    z = parts[0].reshape(B, rows, cols)
    for o in range(HY_ORDER):
        ar, ai = hy_dft_rows(z, f1r, f1i)
        gr, gi = hy_mid(ar.reshape(B, n1, HY_N2, C), ai.reshape(B, n1, HY_N2, C), tables, spectrum, o)
        z = hy_out(gr.reshape(B, n1, cols), gi.reshape(B, n1, cols), f1r, f1i, z,
                   parts[o + 1].reshape(B, rows, cols), jnp.tile(dskip[o], HY_N2)[None, :])
    return z.reshape(B * S, C)


def hyena_context(pc, conv_w, conv_b, filt, dskip):
    L = pc.shape[1]
    u = centred_dwconv(pc, conv_w, conv_b)
    parts = jnp.split(u, HY_ORDER + 1, axis=-1)
    z = parts[0]
    for o in range(HY_ORDER):
        z = parts[o + 1] * long_conv(z, filt[:, o, 0], filt[:, o, 1], dskip[o])
    return z


def kernel(x, c, ctx, c_ctx, norm1_g, norm2_g, ada_w, ada_b, w_in, na_qnorm_g, na_knorm_g, na_rpb,
           ml_conv_w, ml_conv_b, ml_gate_b, ml_out_g, lru_conv_w, lru_conv_b, lru_wa, lru_ba, lru_wx, lru_bx,
           lru_lambda, hy_conv_w, hy_conv_b, hy_w1, hy_b1, hy_w2, hy_b2, hy_w3, hy_freq, hy_decay, hy_dskip,
           br_w, br_gate_b, w_o, router_w, router_b, exp_w_gate, exp_w_up, exp_w_down,
           sh_w_gate, sh_w_up, sh_w_down):
    B, S, D = x.shape
    assert ctx.shape[1] == CTX_LEN and D == D_MODEL
    n_ctx = B * CTX_LEN
    T = n_ctx + B * S
    assert n_ctx % ROW_TILE == 0 and S % ROW_TILE == 0 and T % ROUTE_TILE == 0
    tiles = dict(ctx_tiles=n_ctx // ROW_TILE, tiles_per_batch=S // ROW_TILE)
    pos = jnp.arange(S)
    rows, cols = pos // GRID_W, pos % GRID_W

    xa = jnp.concatenate([ctx.reshape(n_ctx, D), x.reshape(B * S, D)], axis=0)
    for l in range(DEPTH):
        need_ctx = l < DEPTH - 1
        cond = jnp.concatenate([c_ctx[None, :], c], axis=0)
        mod = (jax.nn.silu(cond) @ ada_w[l] + ada_b[l]).reshape(1 + B, 6, D)
        sh1sc1, g1, sh2sc2, g2 = mod[:, 0:2], mod[:, 2:3], mod[:, 3:5], mod[:, 5:6]

        w_l = w_in[l]
        w_main = jnp.concatenate([w_l[:, :IF_START], w_l[:, IF_START + IF_WIDTH:]], axis=1).astype(BF16)
        w_if = jnp.pad(w_l[:, IF_START:IF_START + IF_WIDTH], ((0, 0), (0, LANES - IF_WIDTH)))
        p = norm_mod_project(xa, norm1_g[l][None, :], sh1sc1, w_main, **tiles)
        p_if, _ = norm_mod_narrow(xa, norm1_g[l][None, :], sh1sc1, w_if, **tiles)

        ya = neighbourhood_attention(p, na_rpb[l], na_qnorm_g[l], na_knorm_g[l], B, S)
        yb = mlstm(p, p_if, ml_conv_w[l], ml_conv_b[l], ml_gate_b[l], ml_out_g[l], B, S)
        yc = rglru(p, lru_conv_w[l], lru_conv_b[l], lru_wa[l], lru_ba[l], lru_wx[l], lru_bx[l], lru_lambda[l], B, S)
        hy_par = (hy_w1[l], hy_b1[l], hy_w2[l], hy_b2[l], hy_w3[l], hy_freq[l], hy_decay[l])
        yd_lat = hyena_latents(p, hy_conv_w[l], hy_conv_b[l], hyena_taps(S, *hy_par), hy_dskip[l], B, S)
        if need_ctx:
            pc = p[:n_ctx, HY_OFF:GATE_OFF].reshape(B, CTX_LEN, -1)
            yd_ctx = hyena_context(pc, hy_conv_w[l], hy_conv_b[l], hyena_filters(CTX_LEN, *hy_par), hy_dskip[l])
            yd_ctx = yd_ctx.reshape(n_ctx, HY_WIDTH)
        else:
            yd_ctx = jnp.zeros((n_ctx, HY_WIDTH), F32)
        yd = jnp.concatenate([yd_ctx, yd_lat], axis=0)
        xa = merge_branches([ya, yb, yc, yd], p, br_gate_b[l], br_w[l].astype(BF16), w_o[l].astype(BF16), xa, g1,
                            **tiles)

        logits, h2 = norm_mod_narrow(xa, norm2_g[l][None, :], sh2sc2, router_w[l], **tiles)
        idx, wts, rank, counts = route_tokens(logits, router_b[l])
        slot_of, slot_tok, block_exp, n_used = slot_layout(idx, rank, counts, T)
        h2_pad = jnp.concatenate([h2, jnp.zeros((1, D), BF16)], axis=0)
        ys = routed_experts(block_exp, n_used, h2_pad[slot_tok], exp_w_gate, exp_w_up, exp_w_down, l)
        ys_tok = ys[slot_of.reshape(-1)].reshape(TOP_K, T, D)
        xa = ffn_out(h2, ys_tok, wts.T, xa, g2, sh_w_gate[l].astype(BF16), sh_w_up[l].astype(BF16),
                     sh_w_down[l].astype(BF16), n_ctx, S)
    return xa[n_ctx:].reshape(B, S, D)
```

```python
import functools
import math

import numpy as np
import jax
import jax.numpy as jnp
from jax import lax
from jax.experimental import pallas as pl
from jax.experimental.pallas import tpu as pltpu

D_MODEL = 2048
DEPTH = 2
CTX_LEN = 256
GRID_W = 64
NORM_EPS = 1e-6
N_BRANCH = 4
BRANCH_WIDTH = D_MODEL // 4
HEAD_DIM = 128

NA_HEADS = BRANCH_WIDTH // HEAD_DIM
NA_WIN_ROWS = 8
NA_WIN_COLS = 16

ML_HEADS = BRANCH_WIDTH // HEAD_DIM
ML_WIDTH = BRANCH_WIDTH
ML_CHUNK = 128
ROPE_BASE = 10000.0

LRU_WIDTH = BRANCH_WIDTH
LRU_BLOCKS = 4
LRU_BW = LRU_WIDTH // LRU_BLOCKS
LRU_C = 8.0
LRU_CONV = 4

HY_WIDTH = BRANCH_WIDTH
HY_ORDER = 2
HY_EMB = 33
HY_N2 = 256
HY_K1_TILE = 2
HY_COL_TILE = 8192

N_EXPERTS = 128
TOP_K = 8
N_GROUPS = 8
GROUP_SIZE = N_EXPERTS // N_GROUPS
TOPK_GROUPS = 4
ROUTE_SCALE = 2.5

IN_SPLITS = (3 * BRANCH_WIDTH, 3 * ML_WIDTH, ML_WIDTH, 4 * ML_HEADS, LRU_WIDTH, LRU_WIDTH,
             (HY_ORDER + 1) * HY_WIDTH, N_BRANCH * D_MODEL)
IF_START = sum(IN_SPLITS[:3])
IF_WIDTH = IN_SPLITS[3]
MAIN_SPLITS = IN_SPLITS[:3] + IN_SPLITS[4:]
MAIN_OFF = np.concatenate([[0], np.cumsum(MAIN_SPLITS)]).tolist()
MAIN_WIDTH = MAIN_OFF[-1]
NA_OFF, LRU_X_OFF, LRU_G_OFF, HY_OFF, GATE_OFF = MAIN_OFF[0], MAIN_OFF[3], MAIN_OFF[4], MAIN_OFF[5], MAIN_OFF[6]

LANES = 128
SUBLANES = 8
ROW_TILE = 512
PROJ_COL_TILE = 1024
MERGE_COL_TILE = 512
MOE_ROW_TILE = 512
FFN_TILE = 128
ROUTE_TILE = 512
SEQ_CHUNK = CTX_LEN
NA_G = 4
NEG = -1e30
VMEM_LIMIT = 56 * 1024 * 1024

F32 = jnp.float32
BF16 = jnp.bfloat16


def _cparams(sem):
    return pltpu.CompilerParams(dimension_semantics=sem, vmem_limit_bytes=VMEM_LIMIT)


def _seg_of_tile(i, ctx_tiles, tiles_per_batch):
    return jnp.where(i < ctx_tiles, 0, 1 + (i - ctx_tiles) // tiles_per_batch)


def _norm_mod(x, g, shift, scale):
    ms = jnp.mean(x * x, axis=-1, keepdims=True)
    return (x * lax.rsqrt(ms + NORM_EPS) * g) * (1.0 + scale) + shift


def _head_norm(x, g):
    return x * lax.rsqrt(jnp.mean(x * x, axis=-1, keepdims=True) + NORM_EPS) * g


def _proj_kernel(x_ref, g_ref, mod_ref, w_ref, o_ref, h_scr):
    @pl.when(pl.program_id(1) == 0)
    def _():
        h = _norm_mod(x_ref[...], g_ref[...], mod_ref[0, 0:1, :], mod_ref[0, 1:2, :])
        h_scr[...] = h.astype(BF16)
    o_ref[...] = jnp.dot(h_scr[...], w_ref[...], preferred_element_type=F32)


def norm_mod_project(x, g, mod, w, ctx_tiles, tiles_per_batch):
    T, D = x.shape
    N = w.shape[1]
    seg = functools.partial(_seg_of_tile, ctx_tiles=ctx_tiles, tiles_per_batch=tiles_per_batch)
    return pl.pallas_call(
        _proj_kernel,
        grid=(T // ROW_TILE, N // PROJ_COL_TILE),
        in_specs=[
            pl.BlockSpec((ROW_TILE, D), lambda i, j: (i, 0)),
            pl.BlockSpec((1, D), lambda i, j: (0, 0)),
            pl.BlockSpec((1, 2, D), lambda i, j: (seg(i), 0, 0)),
            pl.BlockSpec((D, PROJ_COL_TILE), lambda i, j: (0, j)),
        ],
        out_specs=pl.BlockSpec((ROW_TILE, PROJ_COL_TILE), lambda i, j: (i, j)),
        out_shape=jax.ShapeDtypeStruct((T, N), F32),
        scratch_shapes=[pltpu.VMEM((ROW_TILE, D), BF16)],
        compiler_params=_cparams(("arbitrary", "arbitrary")),
        name="norm_mod_project",
    )(x, g, mod, w)


def _narrow_kernel(x_ref, g_ref, mod_ref, w_ref, o_ref, *h_ref):
    h = _norm_mod(x_ref[...], g_ref[...], mod_ref[0, 0:1, :], mod_ref[0, 1:2, :])
    if h_ref:
        h_ref[0][...] = h
    o_ref[...] = jnp.dot(h, w_ref[...], preferred_element_type=F32, precision=lax.Precision.HIGHEST)


def norm_mod_narrow(x, g, mod, w, keep_h, ctx_tiles, tiles_per_batch):
    T, D = x.shape
    seg = functools.partial(_seg_of_tile, ctx_tiles=ctx_tiles, tiles_per_batch=tiles_per_batch)
    row = pl.BlockSpec((ROW_TILE, D), lambda i: (i, 0))
    out = pl.pallas_call(
        _narrow_kernel,
        grid=(T // ROW_TILE,),
        in_specs=[
            row,
            pl.BlockSpec((1, D), lambda i: (0, 0)),
            pl.BlockSpec((1, 2, D), lambda i: (seg(i), 0, 0)),
            pl.BlockSpec((D, LANES), lambda i: (0, 0)),
        ],
        out_specs=[pl.BlockSpec((ROW_TILE, LANES), lambda i: (i, 0))] + ([row] if keep_h else []),
        out_shape=[jax.ShapeDtypeStruct((T, LANES), F32)] + ([jax.ShapeDtypeStruct((T, D), F32)] if keep_h else []),
        compiler_params=_cparams(("arbitrary",)),
        name="norm_mod_narrow",
    )(x, g, mod, w)
    return out if keep_h else out[0]


def _na_kernel(q_ref, k0_ref, k1_ref, k2_ref, v0_ref, v1_ref, v2_ref, kc_ref, vc_ref, bias_ref, qg_ref, kg_ref,
               o_ref, *, n_rows):
    s = pl.program_id(1)
    qb, kw = bias_ref.shape[1], bias_ref.shape[2]
    first_row = (s - 1) * NA_G
    q_row = first_row + lax.broadcasted_iota(jnp.int32, (qb, kw), 0) // GRID_W
    k_row = first_row - NA_G + lax.broadcasted_iota(jnp.int32, (qb, kw), 1) // GRID_W
    r0 = jnp.clip(q_row - NA_WIN_ROWS // 2, 0, n_rows - NA_WIN_ROWS)
    valid = (k_row >= r0) & (k_row < r0 + NA_WIN_ROWS) & (s > 0)
    nt = (((1,), (1,)), ((), ()))
    for h in range(NA_HEADS):
        hs = slice(h * HEAD_DIM, (h + 1) * HEAD_DIM)
        qn = (_head_norm(q_ref[:, hs], qg_ref[...]) * (HEAD_DIM ** -0.5)).astype(BF16)
        k_win = jnp.concatenate([k0_ref[:, hs], k1_ref[:, hs], k2_ref[:, hs]], axis=0)
        kn = _head_norm(k_win, kg_ref[...]).astype(BF16)
        s_lat = lax.dot_general(qn, kn, nt, preferred_element_type=F32)
        s_lat = jnp.where(valid, s_lat + bias_ref[h], NEG)
        kcn = _head_norm(kc_ref[:, hs], kg_ref[...]).astype(BF16)
        s_ctx = lax.dot_general(qn, kcn, nt, preferred_element_type=F32)
        m = jnp.maximum(jnp.max(s_lat, axis=-1, keepdims=True), jnp.max(s_ctx, axis=-1, keepdims=True))
        p_lat = jnp.exp(s_lat - m)
        p_ctx = jnp.exp(s_ctx - m)
        denom = jnp.sum(p_lat, axis=-1, keepdims=True) + jnp.sum(p_ctx, axis=-1, keepdims=True)
        v_win = jnp.concatenate([v0_ref[:, hs], v1_ref[:, hs], v2_ref[:, hs]], axis=0).astype(BF16)
        o = jnp.dot(p_lat.astype(BF16), v_win, preferred_element_type=F32)
        o = o + jnp.dot(p_ctx.astype(BF16), vc_ref[:, hs].astype(BF16), preferred_element_type=F32)
        o_ref[:, hs] = o / denom


def na_bias_table(rpb):
    W, k_rows = GRID_W, 3 * NA_G
    qb, kw = NA_G * W, k_rows * W
    qc = np.arange(qb)[:, None] % W
    kc = np.arange(kw)[None, :] % W
    c0 = np.clip(qc - NA_WIN_COLS // 2, 0, W - NA_WIN_COLS)
    ok = (kc >= c0) & (kc < c0 + NA_WIN_COLS)
    lo = W - NA_WIN_COLS
    padded = jnp.pad(rpb, ((0, 0), (0, 0), (lo, lo)))
    by_col = jnp.stack([padded[:, :, W - 1 - c:2 * W - 1 - c] for c in range(W)], axis=2)
    assert NA_WIN_ROWS - 1 - NA_G - (NA_G - 1) >= 0 and k_rows - 1 - NA_G + NA_WIN_ROWS - 1 <= 2 * NA_WIN_ROWS - 2
    per_g = [jnp.stack([by_col[:, kr - NA_G - g + NA_WIN_ROWS - 1] for kr in range(k_rows)], axis=2)
             for g in range(NA_G)]
    table = jnp.stack(per_g, axis=1).reshape(rpb.shape[0], qb, kw)
    return jnp.where(ok[None], table, NEG)


def neighbourhood_attention(p, rpb, qg, kg, B, S):
    T = p.shape[0]
    qb = NA_G * GRID_W
    assert qb == CTX_LEN and NA_WIN_ROWS // 2 <= NA_G and NA_WIN_ROWS - NA_WIN_ROWS // 2 - 1 <= NA_G
    ctx_blocks, n_blk, n_rows = B * CTX_LEN // qb, S // qb, S // GRID_W
    c_q = NA_OFF // BRANCH_WIDTH
    c_k, c_v = c_q + 1, c_q + 2

    def q_map(b, s):
        return (jnp.where(s == 0, b, ctx_blocks + b * n_blk + s - 1), c_q)

    def win_spec(d, col):
        return pl.BlockSpec((qb, BRANCH_WIDTH),
                            lambda b, s: (ctx_blocks + b * n_blk + jnp.clip(s - 1 + d, 0, n_blk - 1), col))

    bias = na_bias_table(rpb)
    return pl.pallas_call(
        functools.partial(_na_kernel, n_rows=n_rows),
        grid=(B, 1 + n_blk),
        in_specs=[pl.BlockSpec((qb, BRANCH_WIDTH), q_map),
                  win_spec(-1, c_k), win_spec(0, c_k), win_spec(1, c_k),
                  win_spec(-1, c_v), win_spec(0, c_v), win_spec(1, c_v),
                  pl.BlockSpec((qb, BRANCH_WIDTH), lambda b, s: (b, c_k)),
                  pl.BlockSpec((qb, BRANCH_WIDTH), lambda b, s: (b, c_v)),
                  pl.BlockSpec(bias.shape, lambda b, s: (0, 0, 0)),
                  pl.BlockSpec((1, HEAD_DIM), lambda b, s: (0, 0)),
                  pl.BlockSpec((1, HEAD_DIM), lambda b, s: (0, 0))],
        out_specs=pl.BlockSpec((qb, BRANCH_WIDTH), lambda b, s: (q_map(b, s)[0], 0)),
        out_shape=jax.ShapeDtypeStruct((T, BRANCH_WIDTH), F32),
        compiler_params=_cparams(("arbitrary", "arbitrary")),
        name="neighbourhood_attention",
    )(p, p, p, p, p, p, p, p, p, bias, qg[None, :], kg[None, :])


def _chunk_edges(reverse):
    s = pl.program_id(1)
    n_lat = pl.num_programs(1) - 1
    lat_idx = (n_lat - s) if reverse else (s - 1)
    has_prev = jnp.where((s > 0) & (lat_idx > 0), 1.0, 0.0)
    has_next = jnp.where((s > 0) & (lat_idx < n_lat - 1), 1.0, 0.0)
    return has_prev, has_next


def _conv4(x_ref, prev_ref, next_ref, cw_ref, cb_ref, has_prev, has_next):
    x = x_ref[...]
    tc, w = x.shape
    row = lax.broadcasted_iota(jnp.int32, (tc, w), 0)
    before = prev_ref[SUBLANES - 1:SUBLANES, :] * has_prev
    after0 = next_ref[0:1, :] * has_next
    after1 = next_ref[1:2, :] * has_next
    xm1 = jnp.where(row == 0, before, pltpu.roll(x, 1, 0))
    xp1 = jnp.where(row == tc - 1, after0, pltpu.roll(x, tc - 1, 0))
    xp2 = jnp.where(row == tc - 2, after0, jnp.where(row == tc - 1, after1, pltpu.roll(x, tc - 2, 0)))
    return cb_ref[...] + xm1 * cw_ref[0:1, :] + x * cw_ref[1:2, :] + xp1 * cw_ref[2:3, :] + xp2 * cw_ref[3:4, :]


def _seq_specs(reverse, B, S, T, width):
    tc = SEQ_CHUNK
    n_lat, ctx_blocks, halo = S // tc, B, tc // SUBLANES

    def blk(b, s):
        lat = (n_lat - s) if reverse else (s - 1)
        return jnp.where(s == 0, b, ctx_blocks + b * n_lat + lat)

    def chunk(col):
        return pl.BlockSpec((tc, width), lambda b, s: (blk(b, s), col))

    def prev(col):
        return pl.BlockSpec((SUBLANES, width), lambda b, s: (jnp.maximum(blk(b, s) * halo - 1, 0), col))

    def nxt(col):
        return pl.BlockSpec((SUBLANES, width), lambda b, s: (jnp.minimum((blk(b, s) + 1) * halo, T // SUBLANES - 1), col))

    def full(shape):
        return pl.BlockSpec(shape, lambda b, s: (0,) * len(shape))
    return blk, chunk, prev, nxt, full


def _lru_kernel(x_ref, prev_ref, next_ref, cw_ref, cb_ref, wa_ref, ba_ref, wx_ref, bx_ref, sp_ref, *rest, reverse):
    if reverse:
        hf_ref, g_ref, o_ref, carry = rest
    else:
        o_ref, carry = rest
    s = pl.program_id(1)
    tc, w = x_ref.shape

    @pl.when(s == 0)
    def _():
        carry[...] = jnp.zeros_like(carry)

    has_prev, has_next = _chunk_edges(reverse)
    xc = _conv4(x_ref, prev_ref, next_ref, cw_ref, cb_ref, has_prev, has_next)
    row = lax.broadcasted_iota(jnp.int32, (tc, w), 0)

    def block_diag(w_ref):
        return jnp.concatenate(
            [jnp.dot(xc[:, n * LRU_BW:(n + 1) * LRU_BW], w_ref[n], preferred_element_type=F32,
                     precision=lax.Precision.HIGHEST) for n in range(LRU_BLOCKS)], axis=-1)
    r = jax.nn.sigmoid(block_diag(wa_ref) + ba_ref[...])
    i = jax.nn.sigmoid(block_diag(wx_ref) + bx_ref[...])
    log_a = -LRU_C * r * sp_ref[...]
    a = jnp.exp(log_a)
    u = jnp.sqrt(1.0 - jnp.exp(2.0 * log_a)) * (i * xc)

    step = 1
    while step < tc:
        if reverse:
            keep = row < tc - step
            shift = tc - step
        else:
            keep = row >= step
            shift = step
        a_sh = jnp.where(keep, pltpu.roll(a, shift, 0), 1.0)
        u_sh = jnp.where(keep, pltpu.roll(u, shift, 0), 0.0)
        u = a * u_sh + u
        a = a * a_sh
        step *= 2
    h = u + a * carry[...]
    carry[...] = h[0:1, :] if reverse else h[tc - 1:tc, :]
    if reverse:
        o_ref[...] = (hf_ref[...] + h) * jax.nn.gelu(g_ref[...])
    else:
        o_ref[...] = h


def rglru(p, conv_w, conv_b, wa, ba, wx, bx, lam, B, S):
    T = p.shape[0]
    tc, w = SEQ_CHUNK, LRU_WIDTH
    assert CTX_LEN == tc and S % tc == 0 and LRU_CONV == 4
    c_x, c_g = LRU_X_OFF // w, LRU_G_OFF // w
    sp = jax.nn.softplus(-lam)

    def call(reverse, extra_in, extra_cols):
        blk, chunk, prev, nxt, full = _seq_specs(reverse, B, S, T, w)
        d = 1 if reverse else 0
        return pl.pallas_call(
            functools.partial(_lru_kernel, reverse=reverse),
            grid=(B, 1 + S // tc),
            in_specs=[chunk(c_x), prev(c_x), nxt(c_x),
                      full((LRU_CONV, w)), full((1, w)), full((LRU_BLOCKS, LRU_BW, LRU_BW)), full((1, w)),
                      full((LRU_BLOCKS, LRU_BW, LRU_BW)), full((1, w)), full((1, w))] + [chunk(c) for c in extra_cols],
            out_specs=chunk(0),
            out_shape=jax.ShapeDtypeStruct((T, w), F32),
            scratch_shapes=[pltpu.VMEM((1, w), F32)],
            compiler_params=_cparams(("arbitrary", "arbitrary")),
            name="rglru_bwd" if reverse else "rglru_fwd",
        )(p, p, p, conv_w, conv_b[None, :], wa[d], ba[d][None, :], wx[d], bx[d][None, :], sp[d][None, :], *extra_in)

    h_fwd = call(False, [], [])
    return call(True, [h_fwd, p], [0, c_g])


def _log_sigmoid(x):
    return jnp.minimum(x, 0.0) - jnp.log(1.0 + jnp.exp(-jnp.abs(x)))


def _mlstm_kernel(q_ref, qp_ref, qn_ref, k_ref, kp_ref, kn_ref, v_ref, if_ref, cos_ref, sin_ref, cwq_ref, cbq_ref,
                  cwk_ref, cbk_ref, gb_ref, *rest, reverse):
    if reverse:
        hf_ref, og_ref, outg_ref, o_ref, c_scr, n_scr, m_scr = rest
    else:
        o_ref, c_scr, n_scr, m_scr = rest
    s = pl.program_id(1)
    tc = q_ref.shape[0]
    ck = ML_CHUNK
    d0 = (1 if reverse else 0) * 2 * ML_HEADS

    @pl.when(s == 0)
    def _():
        c_scr[...] = jnp.zeros_like(c_scr)
        n_scr[...] = jnp.zeros_like(n_scr)
        m_scr[...] = jnp.zeros_like(m_scr)

    has_prev, has_next = _chunk_edges(reverse)
    q_all = jax.nn.silu(_conv4(q_ref, qp_ref, qn_ref, cwq_ref, cbq_ref, has_prev, has_next))
    k_all = jax.nn.silu(_conv4(k_ref, kp_ref, kn_ref, cwk_ref, cbk_ref, has_prev, has_next))

    is_lat = s > 0
    cos = jnp.where(is_lat, cos_ref[...], 1.0)
    sin = jnp.where(is_lat, sin_ref[...], 0.0)
    lane = lax.broadcasted_iota(jnp.int32, (tc, HEAD_DIM), 1)
    low_quarter = (lane % (HEAD_DIM // 2)) < (HEAD_DIM // 4)

    def rope(xh):
        swapped = jnp.where(low_quarter, pltpu.roll(xh, HEAD_DIM - HEAD_DIM // 4, 1), pltpu.roll(xh, HEAD_DIM // 4, 1))
        return xh * cos + swapped * sin

    row = lax.broadcasted_iota(jnp.int32, (ck, ck), 0)
    col = lax.broadcasted_iota(jnp.int32, (ck, ck), 1)
    causal = (col >= row) if reverse else (col <= row)
    nt = (((1,), (1,)), ((), ()))
    tn = (((0,), (0,)), ((), ()))

    qs, ks = [], []
    for h in range(ML_HEADS):
        hs = slice(h * HEAD_DIM, (h + 1) * HEAD_DIM)
        qs.append(rope(q_all[:, hs]))
        ks.append(rope(k_all[:, hs]) * (HEAD_DIM ** -0.5))

    subs = range(tc // ck)
    for sub in (reversed(subs) if reverse else subs):
        rs = slice(sub * ck, (sub + 1) * ck)
        gates = if_ref[rs, :] + gb_ref[...]
        b_all = _log_sigmoid(gates)
        step = 1
        while step < ck:
            if reverse:
                b_all = b_all + jnp.where(row < ck - step, pltpu.roll(b_all, ck - step, 0), 0.0)
            else:
                b_all = b_all + jnp.where(row >= step, pltpu.roll(b_all, step, 0), 0.0)
            step *= 2
        b_all_t = b_all.T
        gates_t = gates.T
        last = 0 if reverse else ck - 1
        for h in range(ML_HEADS):
            hs = slice(h * HEAD_DIM, (h + 1) * HEAD_DIM)
            ci, cf = d0 + h, d0 + ML_HEADS + h
            qh, kh, vh = qs[h][rs], ks[h][rs], v_ref[rs, hs]
            b_col, b_row = b_all[:, cf:cf + 1], b_all_t[cf:cf + 1, :]
            i_col, i_row = gates[:, ci:ci + 1], gates_t[ci:ci + 1, :]
            b_last = b_all[last:last + 1, cf:cf + 1]
            m_prev = m_scr[h:h + 1, 0:1]
            d = jnp.where(causal, b_col - b_row + i_row, -jnp.inf)
            m_t = jnp.maximum(b_col + m_prev, jnp.max(d, axis=-1, keepdims=True))
            qb, kb, vb = qh.astype(BF16), kh.astype(BF16), vh.astype(BF16)
            w = jnp.exp(d - m_t) * lax.dot_general(qb, kb, nt, preferred_element_type=F32)
            carry_w = jnp.exp(b_col + m_prev - m_t)
            num = jnp.dot(w.astype(BF16), vb, preferred_element_type=F32)
            num = num + carry_w * jnp.dot(qb, c_scr[h].astype(BF16), preferred_element_type=F32)
            den = jnp.sum(w, axis=-1, keepdims=True) + carry_w * jnp.sum(qh * n_scr[h:h + 1, :], axis=-1, keepdims=True)
            hh = num / jnp.maximum(jnp.abs(den), jnp.exp(-m_t))
            g_col = b_last - b_col + i_col
            m_new = jnp.maximum(b_last + m_prev, jnp.max(g_col, axis=0, keepdims=True))
            keep = jnp.exp(b_last + m_prev - m_new)
            kw = kh * jnp.exp(g_col - m_new)
            c_scr[h] = keep * c_scr[h] + lax.dot_general(kw.astype(BF16), vb, tn, preferred_element_type=F32)
            n_scr[h:h + 1, :] = keep * n_scr[h:h + 1, :] + jnp.sum(kw, axis=0, keepdims=True)
            m_scr[h:h + 1, :] = jnp.broadcast_to(m_new, (1, LANES))
            if reverse:
                hsum = hf_ref[rs, hs] + hh
                o_ref[rs, hs] = _head_norm(hsum, outg_ref[:, hs]) * jax.nn.sigmoid(og_ref[rs, hs])
            else:
                o_ref[rs, hs] = hh


def rope_tables(S):
    quarter = HEAD_DIM // 4
    inv = ROPE_BASE ** (-jnp.arange(quarter, dtype=F32) / quarter)
    pos = jnp.arange(S)
    ang_r = (pos // GRID_W).astype(F32)[:, None] * inv
    ang_c = (pos % GRID_W).astype(F32)[:, None] * inv
    cos = jnp.concatenate([jnp.cos(ang_r), jnp.cos(ang_r), jnp.cos(ang_c), jnp.cos(ang_c)], axis=-1)
    sin = jnp.concatenate([-jnp.sin(ang_r), jnp.sin(ang_r), -jnp.sin(ang_c), jnp.sin(ang_c)], axis=-1)
    return cos, sin


def mlstm(p, p_if, conv_w, conv_b, gate_b, out_g, B, S):
    T = p.shape[0]
    tc, w = SEQ_CHUNK, ML_WIDTH
    assert CTX_LEN == tc and S % tc == 0 and tc % ML_CHUNK == 0 and conv_w.shape[0] == 4
    c_q = MAIN_OFF[1] // w
    c_k, c_v, c_o = c_q + 1, c_q + 2, MAIN_OFF[2] // w
    cos, sin = rope_tables(S)
    gb = jnp.pad(gate_b.reshape(1, -1), ((0, 0), (0, LANES - gate_b.size)))
    n_lat = S // tc

    def call(reverse, extra_in, extra_specs):
        blk, chunk, prev, nxt, full = _seq_specs(reverse, B, S, T, w)

        def lat_rows(b, s):
            return (jnp.clip((n_lat - s) if reverse else (s - 1), 0, n_lat - 1), 0)
        table = pl.BlockSpec((tc, HEAD_DIM), lat_rows)
        return pl.pallas_call(
            functools.partial(_mlstm_kernel, reverse=reverse),
            grid=(B, 1 + n_lat),
            in_specs=[chunk(c_q), prev(c_q), nxt(c_q), chunk(c_k), prev(c_k), nxt(c_k), chunk(c_v),
                      pl.BlockSpec((tc, LANES), lambda b, s: (blk(b, s), 0)), table, table,
                      full((4, w)), full((1, w)), full((4, w)), full((1, w)), full((1, LANES))] + extra_specs(chunk, full),
            out_specs=chunk(0),
            out_shape=jax.ShapeDtypeStruct((T, w), F32),
            scratch_shapes=[pltpu.VMEM((ML_HEADS, HEAD_DIM, HEAD_DIM), F32), pltpu.VMEM((SUBLANES, HEAD_DIM), F32),
                            pltpu.VMEM((SUBLANES, LANES), F32)],
            compiler_params=_cparams(("arbitrary", "arbitrary")),
            name="mlstm_bwd" if reverse else "mlstm_fwd",
        )(p, p, p, p, p, p, p, p_if, cos, sin, conv_w[:, :w], conv_b[None, :w], conv_w[:, w:], conv_b[None, w:], gb,
          *extra_in)

    h_fwd = call(False, [], lambda chunk, full: [])
    return call(True, [h_fwd, p, out_g[None, :]], lambda chunk, full: [chunk(0), chunk(c_o), full((1, w))])


def _merge_kernel(ya_ref, yb_ref, yc_ref, yd_ref, ga_ref, gb_ref, gc_ref, gd_ref, bias_ref, bw_ref, wo_ref,
                  x_ref, g1_ref, o_ref, acc_ref):
    n = pl.program_id(1)
    z = None
    for b, (y_ref, gate_ref) in enumerate(((ya_ref, ga_ref), (yb_ref, gb_ref), (yc_ref, gc_ref), (yd_ref, gd_ref))):
        t = jnp.dot(y_ref[...].astype(BF16), bw_ref[b], preferred_element_type=F32)
        t = jax.nn.sigmoid(gate_ref[...] + bias_ref[b:b + 1, :]) * t
        z = t if z is None else z + t
    part = jnp.dot(z.astype(BF16), wo_ref[...], preferred_element_type=F32)

    @pl.when(n == 0)
    def _():
        acc_ref[...] = part

    @pl.when(n > 0)
    def _():
        acc_ref[...] += part

    @pl.when(n == pl.num_programs(1) - 1)
    def _():
        o_ref[...] = x_ref[...] + g1_ref[0] * acc_ref[...]


def merge_branches(ys, p, bias, bw, wo, x, g1, ctx_tiles, tiles_per_batch):
    T, D = x.shape
    ct = MERGE_COL_TILE
    seg = functools.partial(_seg_of_tile, ctx_tiles=ctx_tiles, tiles_per_batch=tiles_per_batch)
    y_spec = pl.BlockSpec((ROW_TILE, BRANCH_WIDTH), lambda i, n: (i, 0))

    def gate_spec(b):
        base = (GATE_OFF + b * D) // ct
        return pl.BlockSpec((ROW_TILE, ct), lambda i, n: (i, base + n))

    return pl.pallas_call(
        _merge_kernel,
        grid=(T // ROW_TILE, D // ct),
        in_specs=[y_spec, y_spec, y_spec, y_spec,
                  gate_spec(0), gate_spec(1), gate_spec(2), gate_spec(3),
                  pl.BlockSpec((N_BRANCH, ct), lambda i, n: (0, n)),
                  pl.BlockSpec((N_BRANCH, BRANCH_WIDTH, ct), lambda i, n: (0, 0, n)),
                  pl.BlockSpec((ct, D), lambda i, n: (n, 0)),
                  pl.BlockSpec((ROW_TILE, D), lambda i, n: (i, 0)),
                  pl.BlockSpec((1, 1, D), lambda i, n: (seg(i), 0, 0))],
        out_specs=pl.BlockSpec((ROW_TILE, D), lambda i, n: (i, 0)),
        out_shape=jax.ShapeDtypeStruct((T, D), F32),
        scratch_shapes=[pltpu.VMEM((ROW_TILE, D), F32)],
        compiler_params=_cparams(("arbitrary", "arbitrary")),
        name="merge_branches",
    )(*ys, p, p, p, p, bias, bw, wo, x, g1)


def _route_kernel(lg_ref, rb_ref, idx_ref, w_ref, rank_ref, cnt_ref, run_scr):
    @pl.when(pl.program_id(0) == 0)
    def _():
        run_scr[...] = jnp.zeros_like(run_scr)

    tm = lg_ref.shape[0]
    E = N_EXPERTS
    scores = jax.nn.sigmoid(lg_ref[...].T)
    biased = scores + rb_ref[...]
    ninf = -jnp.inf

    gscore = []
    for g in range(N_GROUPS):
        blk = biased[g * GROUP_SIZE:(g + 1) * GROUP_SIZE, :]
        m1 = jnp.max(blk, axis=0, keepdims=True)
        n_top = jnp.sum(jnp.where(blk == m1, 1.0, 0.0), axis=0, keepdims=True)
        m2 = jnp.max(jnp.where(blk < m1, blk, ninf), axis=0, keepdims=True)
        gscore.append(m1 + jnp.where(n_top >= 2.0, m1, m2))
    parts = []
    for g in range(N_GROUPS):
        beaten = jnp.zeros_like(gscore[g])
        for o in range(N_GROUPS):
            if o != g:
                wins = (gscore[o] >= gscore[g]) if o < g else (gscore[o] > gscore[g])
                beaten = beaten + jnp.where(wins, 1.0, 0.0)
        keep = beaten < float(TOPK_GROUPS)
        parts.append(jnp.where(keep, biased[g * GROUP_SIZE:(g + 1) * GROUP_SIZE, :], ninf))
    cur = jnp.concatenate(parts, axis=0)

    eidx = lax.broadcasted_iota(jnp.int32, (E, tm), 0).astype(F32)
    picks, wts = [], []
    sel = jnp.zeros((E, tm), F32)
    for _ in range(TOP_K):
        m = jnp.max(cur, axis=0, keepdims=True)
        ik = jnp.min(jnp.where(cur == m, eidx, float(E)), axis=0, keepdims=True)
        hit = eidx == ik
        picks.append(ik)
        wts.append(jnp.sum(jnp.where(hit, scores, 0.0), axis=0, keepdims=True))
        sel = sel + jnp.where(hit, 1.0, 0.0)
        cur = jnp.where(hit, ninf, cur)
    total = wts[0]
    for k in range(1, TOP_K):
        total = total + wts[k]

    before = lax.broadcasted_iota(jnp.int32, (tm, tm), 0) < lax.broadcasted_iota(jnp.int32, (tm, tm), 1)
    prefix = jnp.dot(sel.astype(BF16), jnp.where(before, 1.0, 0.0).astype(BF16), preferred_element_type=F32)
    base = prefix + run_scr[:, 0:1]
    for k in range(TOP_K):
        idx_ref[k:k + 1, :] = picks[k].astype(jnp.int32)
        w_ref[k:k + 1, :] = wts[k] / total * ROUTE_SCALE
        rank_ref[k:k + 1, :] = jnp.sum(jnp.where(eidx == picks[k], base, 0.0), axis=0, keepdims=True).astype(jnp.int32)
    run_scr[...] = run_scr[...] + jnp.sum(sel, axis=1, keepdims=True)
    cnt_ref[...] = run_scr[...]


def route_tokens(logits, router_b):
    T, E = logits.shape
    tm = ROUTE_TILE
    per_k = pl.BlockSpec((TOP_K, tm), lambda i: (0, i))
    idx, wts, rank, cnt = pl.pallas_call(
        _route_kernel,
        grid=(T // tm,),
        in_specs=[pl.BlockSpec((tm, E), lambda i: (i, 0)), pl.BlockSpec((E, 1), lambda i: (0, 0))],
        out_specs=[per_k, per_k, per_k, pl.BlockSpec((E, LANES), lambda i: (0, 0))],
        out_shape=[jax.ShapeDtypeStruct((TOP_K, T), jnp.int32), jax.ShapeDtypeStruct((TOP_K, T), F32),
                   jax.ShapeDtypeStruct((TOP_K, T), jnp.int32), jax.ShapeDtypeStruct((E, LANES), F32)],
        scratch_shapes=[pltpu.VMEM((E, LANES), F32)],
        compiler_params=_cparams(("arbitrary",)),
        name="route_tokens",
    )(logits, router_b[:, None])
    return idx, wts, rank, cnt[:, 0].astype(jnp.int32)


def slot_layout(idx, rank, counts, T):
    E, tb = N_EXPERTS, MOE_ROW_TILE
    padded = (counts + tb - 1) // tb * tb
    pend = jnp.cumsum(padded)
    first = pend - padded
    experts = jnp.arange(E, dtype=jnp.int32)
    slot_of = jnp.sum(jnp.where(idx[:, :, None] == experts, first, 0), axis=-1) + rank
    n_blocks = -(-T * TOP_K // tb) + E
    tok = jnp.broadcast_to(jnp.arange(T, dtype=jnp.int32), (TOP_K, T))
    slot_tok = jnp.zeros((n_blocks * tb,), jnp.int32).at[slot_of.reshape(-1)].set(tok.reshape(-1))
    block_exp = jnp.minimum(jnp.searchsorted(pend, jnp.arange(n_blocks) * tb, side='right'), E - 1).astype(jnp.int32)
    n_used = (pend[-1] // tb).astype(jnp.int32).reshape(1)
    return slot_of, slot_tok.reshape(n_blocks, 1, tb), block_exp, n_used


def _row_gather_copy(src_hbm, row, dst, sem):
    return pltpu.make_async_copy(src_hbm.at[pl.ds(row, 1)], dst, sem)


def _expert_kernel(be_ref, nb_ref, tok_ref, tok_next_ref, h_hbm, wg_ref, wu_ref, wd_ref, o_ref,
                   xbuf, sem, wg_s, wu_s, wd_s):
    i = pl.program_id(0)
    nb = nb_ref[0]
    live = i < nb
    tb = xbuf.shape[1]

    def fetch(idx_ref, slot):
        def body(r, carry):
            _row_gather_copy(h_hbm, idx_ref[0, 0, r], xbuf.at[slot, pl.ds(r, 1)], sem.at[slot]).start()
            return carry
        lax.fori_loop(0, tb, body, 0, unroll=8)

    @pl.when((i == 0) & live)
    def _():
        fetch(tok_ref, 0)

    @pl.when(i + 1 < nb)
    def _():
        fetch(tok_next_ref, (i + 1) % 2)

    @pl.when(live & ((i == 0) | (be_ref[i] != be_ref[jnp.maximum(i - 1, 0)])))
    def _():
        wg_s[...] = wg_ref[0, 0].astype(BF16)
        wu_s[...] = wu_ref[0, 0].astype(BF16)
        wd_s[...] = wd_ref[0, 0].astype(BF16)

    @pl.when(live)
    def _():
        slot = i % 2
        pltpu.make_async_copy(h_hbm.at[pl.ds(0, tb)], xbuf.at[slot], sem.at[slot]).wait()
        xb = xbuf[slot].astype(BF16)
        a = jax.nn.silu(jnp.dot(xb, wg_s[...], preferred_element_type=F32))
        a = a * jnp.dot(xb, wu_s[...], preferred_element_type=F32)
        o_ref[...] = jnp.dot(a.astype(BF16), wd_s[...], preferred_element_type=F32)

    @pl.when(i >= nb)
    def _():
        o_ref[...] = jnp.zeros_like(o_ref)


def routed_experts(block_exp, n_used, slot_tok, h, wg, wu, wd, layer):
    n_blocks, _, tb = slot_tok.shape
    D = h.shape[1]
    H = wg.shape[-1]

    def wsel(i, be, nb):
        return (layer, be[jnp.minimum(i, nb[0] - 1)], 0, 0)

    def toks(ahead):
        return pl.BlockSpec((1, 1, tb), lambda i, be, nb: (jnp.minimum(i + ahead, n_blocks - 1), 0, 0),
                            memory_space=pltpu.SMEM)

    grid_spec = pltpu.PrefetchScalarGridSpec(
        num_scalar_prefetch=2,
        grid=(n_blocks,),
        in_specs=[toks(0), toks(1),
                  pl.BlockSpec(memory_space=pl.ANY),
                  pl.BlockSpec((1, 1, D, H), wsel),
                  pl.BlockSpec((1, 1, D, H), wsel),
                  pl.BlockSpec((1, 1, H, D), wsel)],
        out_specs=pl.BlockSpec((tb, D), lambda i, be, nb: (i, 0)),
        scratch_shapes=[pltpu.VMEM((2, tb, D), F32), pltpu.SemaphoreType.DMA((2,)),
                        pltpu.VMEM((D, H), BF16), pltpu.VMEM((D, H), BF16), pltpu.VMEM((H, D), BF16)],
    )
    return pl.pallas_call(
        _expert_kernel,
        grid_spec=grid_spec,
        out_shape=jax.ShapeDtypeStruct((n_blocks * tb, D), F32),
        compiler_params=_cparams(("arbitrary",)),
        name="routed_experts",
    )(block_exp, n_used, slot_tok, slot_tok, h, wg, wu, wd)


def _ffn_out_kernel(slot_ref, slot_next_ref, ys_hbm, h_ref, wt_ref, x_ref, g2_ref, sg_ref, su_ref, sd_ref, o_ref,
                    ybuf, sem):
    i = pl.program_id(0)
    tm = h_ref.shape[0]

    def fetch(idx_ref, slot):
        for k in range(TOP_K):
            def body(r, carry):
                _row_gather_copy(ys_hbm, idx_ref[0, k, r], ybuf.at[slot, k, pl.ds(r, 1)], sem.at[slot]).start()
                return carry
            lax.fori_loop(0, tm, body, 0, unroll=8)

    @pl.when(i == 0)
    def _():
        fetch(slot_ref, 0)

    @pl.when(i + 1 < pl.num_programs(0))
    def _():
        fetch(slot_next_ref, (i + 1) % 2)

    h = h_ref[...].astype(BF16)
    a = jax.nn.silu(jnp.dot(h, sg_ref[...], preferred_element_type=F32))
    a = a * jnp.dot(h, su_ref[...], preferred_element_type=F32)
    acc = jnp.dot(a.astype(BF16), sd_ref[...], preferred_element_type=F32)
    wt = wt_ref[...]
    slot = i % 2
    for k in range(TOP_K):
        pltpu.make_async_copy(ys_hbm.at[pl.ds(0, tm)], ybuf.at[slot, k], sem.at[slot]).wait()
    for k in range(TOP_K):
        acc = acc + ybuf[slot, k] * wt[:, k:k + 1]
    o_ref[...] = x_ref[...] + g2_ref[0] * acc


def ffn_out(h, ys, slot_of, wts, x, g2, sg, su, sd, n_ctx, S):
    T, D = x.shape
    H = sg.shape[-1]
    tm = FFN_TILE
    n_tiles = T // tm
    seg = functools.partial(_seg_of_tile, ctx_tiles=n_ctx // tm, tiles_per_batch=S // tm)
    row = pl.BlockSpec((tm, D), lambda i: (i, 0))
    slots = slot_of.reshape(TOP_K, n_tiles, tm).transpose(1, 0, 2)

    def tile_slots(ahead):
        return pl.BlockSpec((1, TOP_K, tm), lambda i: (jnp.minimum(i + ahead, n_tiles - 1), 0, 0),
                            memory_space=pltpu.SMEM)
    return pl.pallas_call(
        _ffn_out_kernel,
        grid=(n_tiles,),
        in_specs=[tile_slots(0), tile_slots(1),
                  pl.BlockSpec(memory_space=pl.ANY),
                  row,
                  pl.BlockSpec((tm, TOP_K), lambda i: (i, 0)),
                  row,
                  pl.BlockSpec((1, 1, D), lambda i: (seg(i), 0, 0)),
                  pl.BlockSpec((D, H), lambda i: (0, 0)),
                  pl.BlockSpec((D, H), lambda i: (0, 0)),
                  pl.BlockSpec((H, D), lambda i: (0, 0))],
        out_specs=row,
        out_shape=jax.ShapeDtypeStruct((T, D), F32),
        scratch_shapes=[pltpu.VMEM((2, TOP_K, tm, D), F32), pltpu.SemaphoreType.DMA((2,))],
        compiler_params=_cparams(("arbitrary",)),
        name="ffn_out",
    )(slots, slots, ys, h, wts, x, g2, sg, su, sd)


def rms_norm(x, g):
    xf = x.astype(F32)
    y = xf * lax.rsqrt(jnp.mean(xf * xf, axis=-1, keepdims=True) + NORM_EPS)
    return (y * g.astype(F32)).astype(x.dtype)


def centred_dwconv(x, w, b):
    K, L = w.shape[0], x.shape[1]
    lo = (K - 1) // 2
    xp = jnp.pad(x, ((0, 0), (lo, K - 1 - lo), (0, 0)))
    y = b
    for j in range(K):
        y = y + xp[:, j:j + L] * w[j]
    return y


def to_heads(a, n_heads):
    B, L, W = a.shape
    return a.reshape(B, L, n_heads, W // n_heads).transpose(0, 2, 1, 3)


def from_heads(a):
    B, H, L, Dh = a.shape
    return a.transpose(0, 2, 1, 3).reshape(B, L, H * Dh)


def rope_2d(x, rows, cols):
    dh = x.shape[-1]
    half, quarter = dh // 2, dh // 4
    inv = ROPE_BASE ** (-jnp.arange(quarter, dtype=F32) / quarter)

    def rot(xa, pos):
        ang = pos.astype(F32)[:, None] * inv
        cos, sin = jnp.cos(ang).astype(xa.dtype), jnp.sin(ang).astype(xa.dtype)
        x1, x2 = xa[..., :quarter], xa[..., quarter:]
        return jnp.concatenate([x1 * cos - x2 * sin, x1 * sin + x2 * cos], axis=-1)
    return jnp.concatenate([rot(x[..., :half], rows), rot(x[..., half:], cols)], axis=-1)


def mlstm_scan(q, k, v, ig, fg, state):
    B, H, L, Dh = q.shape
    nc = L // ML_CHUNK

    def chunks(a):
        return jnp.moveaxis(a.reshape(B, H, nc, ML_CHUNK, *a.shape[3:]), 2, 0)
    lower = jnp.tril(jnp.ones((ML_CHUNK, ML_CHUNK), dtype=bool))

    def step(carry, xs):
        C, n, m = carry
        qc, kc, vc, ic, fc = xs
        b = jnp.cumsum(jax.nn.log_sigmoid(fc), axis=-1)
        d = jnp.where(lower, b[..., :, None] - b[..., None, :] + ic[..., None, :], -jnp.inf)
        m_t = jnp.maximum(b + m[..., None], jnp.max(d, axis=-1))
        w = jnp.exp(d - m_t[..., None]) * jnp.einsum('bhtd,bhsd->bhts', qc, kc)
        carry_w = jnp.exp(b + m[..., None] - m_t)
        num = jnp.einsum('bhts,bhsd->bhtd', w, vc) + carry_w[..., None] * jnp.einsum('bhtd,bhde->bhte', qc, C)
        den = jnp.sum(w, axis=-1) + carry_w * jnp.einsum('bhtd,bhd->bht', qc, n)
        h = num / jnp.maximum(jnp.abs(den), jnp.exp(-m_t))[..., None]
        g = b[..., -1:] - b + ic
        m_new = jnp.maximum(b[..., -1] + m, jnp.max(g, axis=-1))
        keep = jnp.exp(b[..., -1] + m - m_new)
        wg = jnp.exp(g - m_new[..., None])
        C = keep[..., None, None] * C + jnp.einsum('bhs,bhsd,bhse->bhde', wg, kc, vc)
        n = keep[..., None] * n + jnp.einsum('bhs,bhsd->bhd', wg, kc)
        return (C, n, m_new), h
    state, hs = lax.scan(step, state, (chunks(q), chunks(k), chunks(v), chunks(ig), chunks(fg)))
    return jnp.moveaxis(hs, 0, 2).reshape(B, H, L, Dh), state


def mlstm_branch(pc_qkv, pl_qkv, pc_o, pl_o, pc_if, pl_if, conv_w, conv_b, gate_b, out_g, rows, cols, need_ctx):
    def prep(p_qkv, p_if, rotary):
        q, k, v = jnp.split(p_qkv, 3, axis=-1)
        qk = jax.nn.silu(centred_dwconv(jnp.concatenate([q, k], axis=-1), conv_w, conv_b))
        q, k = jnp.split(qk, 2, axis=-1)
        q = to_heads(q, ML_HEADS).astype(F32)
        k = to_heads(k, ML_HEADS).astype(F32)
        v = to_heads(v, ML_HEADS).astype(F32)
        if rotary:
            q, k = rope_2d(q, rows, cols), rope_2d(k, rows, cols)
        k = k * (HEAD_DIM ** -0.5)
        B, L, _ = p_if.shape
        g = p_if.astype(F32).reshape(B, L, 2, 2, ML_HEADS) + gate_b
        return q, k, v, g.transpose(2, 3, 0, 4, 1)
    qc, kc, vc, gc = prep(pc_qkv, pc_if, False)
    ql, kl, vl, gl = prep(pl_qkv, pl_if, True)
    B = ql.shape[0]
    zero = (jnp.zeros((B, ML_HEADS, HEAD_DIM, HEAD_DIM), F32),
            jnp.zeros((B, ML_HEADS, HEAD_DIM), F32), jnp.zeros((B, ML_HEADS), F32))
    flip = lambda a: jnp.flip(a, axis=2)
    hc_f, st_f = mlstm_scan(qc, kc, vc, gc[0, 0], gc[0, 1], zero)
    hl_f, _ = mlstm_scan(ql, kl, vl, gl[0, 0], gl[0, 1], st_f)
    hc_b, st_b = mlstm_scan(flip(qc), flip(kc), flip(vc), flip(gc[1, 0]), flip(gc[1, 1]), zero)
    hl_b, _ = mlstm_scan(flip(ql), flip(kl), flip(vl), flip(gl[1, 0]), flip(gl[1, 1]), st_b)

    def readout(h, o):
        h = from_heads(rms_norm(h, out_g.reshape(ML_HEADS, 1, HEAD_DIM)))
        return (h * jax.nn.sigmoid(o.astype(F32))).astype(o.dtype)
    yl = readout(hl_f + flip(hl_b), pl_o)
    yc = readout(hc_f + flip(hc_b), pc_o) if need_ctx else None
    return yc, yl


def hyena_filters(L, w1, b1, w2, b2, w3, freq, decay):
    n = jnp.arange(L, dtype=F32)
    t = n / max(L - 1, 1)
    bands = (HY_EMB - 1) // 2
    f = jnp.linspace(1e-4, bands - 1, bands, dtype=F32)
    ang = (2.0 * math.pi / L) * n[:, None] * f
    z = jnp.concatenate([t[:, None], jnp.cos(ang), -jnp.sin(ang)], axis=-1)
    a = jnp.sin(freq * (z @ w1 + b1))
    a = jnp.sin(freq * (a @ w2 + b2))
    h = ((a @ w3) * jnp.exp(-t[:, None] * jnp.abs(decay))).astype(F32)
    h = h.reshape(L, HY_ORDER, 2, HY_WIDTH)
    l1 = jnp.sum(jnp.abs(h[:, :, 0]), axis=0) + jnp.sum(jnp.abs(h[1:, :, 1]), axis=0)
    return h / l1[None, :, None]


def long_conv(u, hf, hb, d):
    B, L, W = u.shape
    taps = jnp.concatenate([hf, jnp.zeros((1, W), hf.dtype), hb[:0:-1]], axis=0)
    y = jnp.fft.irfft(jnp.fft.rfft(u, n=2 * L, axis=1) * jnp.fft.rfft(taps, axis=0)[None], n=2 * L, axis=1)[:, :L]
    return y + u * d


def hyena_branch(pc, pl_, conv_w, conv_b, w1, b1, w2, b2, w3, freq, decay, dskip, need_ctx):
    def run(p):
        L = p.shape[1]
        u = centred_dwconv(p, conv_w, conv_b).astype(F32)
        parts = jnp.split(u, HY_ORDER + 1, axis=-1)
        filt = hyena_filters(L, w1, b1, w2, b2, w3, freq, decay)
        z = parts[0]
        for o in range(HY_ORDER):
            z = parts[o + 1] * long_conv(z, filt[:, o, 0], filt[:, o, 1], dskip[o])
        return z.astype(p.dtype)
    yl = run(pl_)
    yc = run(pc) if need_ctx else None
    return yc, yl


def _dft_tables(L):
    n = 2 * L
    n1 = n // HY_N2
    assert n == n1 * HY_N2 and n1 % HY_K1_TILE == 0
    a1 = 2.0 * np.pi * np.outer(np.arange(n1), np.arange(n1)) / n1
    a2 = 2.0 * np.pi * np.outer(np.arange(HY_N2), np.arange(HY_N2)) / HY_N2
    at = 2.0 * np.pi * np.outer(np.arange(n1), np.arange(HY_N2)) / n

    def pair(a, dt):
        return jnp.asarray(np.cos(a), dt), jnp.asarray(-np.sin(a), dt)
    f1r, f1i = pair(a1, BF16)
    f2r, f2i = pair(a2, BF16)
    twr, twi = pair(at, F32)
    return f1r, f1i, f2r, f2i, twr[:, :, None], twi[:, :, None]


def _hy_conv_kernel(x_ref, prev_ref, next_ref, cw_ref, cb_ref, o0_ref, o1_ref, o2_ref):
    j = pl.program_id(1)
    has_prev = jnp.where(j > 0, 1.0, 0.0)
    has_next = jnp.where(j < pl.num_programs(1) - 1, 1.0, 0.0)
    x = x_ref[...]
    tc, w = x.shape
    row = lax.broadcasted_iota(jnp.int32, (tc, w), 0)
    xm1 = jnp.where(row == 0, prev_ref[SUBLANES - 1:SUBLANES, :] * has_prev, pltpu.roll(x, 1, 0))
    xp1 = jnp.where(row == tc - 1, next_ref[0:1, :] * has_next, pltpu.roll(x, tc - 1, 0))
    u = cb_ref[...] + xm1 * cw_ref[0:1, :] + x * cw_ref[1:2, :] + xp1 * cw_ref[2:3, :]
    for o, o_ref in enumerate((o0_ref, o1_ref, o2_ref)):
        o_ref[...] = u[:, o * HY_WIDTH:(o + 1) * HY_WIDTH]


def hy_short_conv(p, conv_w, conv_b, B, S):
    T = p.shape[0]
    tc, w = ROW_TILE, (HY_ORDER + 1) * HY_WIDTH
    assert conv_w.shape[0] == 3 and HY_ORDER == 2 and HY_OFF % w == 0
    col, n_blk, first = HY_OFF // w, S // tc, (T - B * S) // tc
    halo = tc // SUBLANES

    def blk(b, j):
        return first + b * n_blk + j
    out = pl.BlockSpec((tc, HY_WIDTH), lambda b, j: (b * n_blk + j, 0))
    return pl.pallas_call(
        _hy_conv_kernel,
        grid=(B, n_blk),
        in_specs=[pl.BlockSpec((tc, w), lambda b, j: (blk(b, j), col)),
                  pl.BlockSpec((SUBLANES, w), lambda b, j: (blk(b, j) * halo - 1, col)),
                  pl.BlockSpec((SUBLANES, w), lambda b, j: (jnp.minimum((blk(b, j) + 1) * halo, T // SUBLANES - 1), col)),
                  pl.BlockSpec((3, w), lambda b, j: (0, 0)), pl.BlockSpec((1, w), lambda b, j: (0, 0))],
        out_specs=[out, out, out],
        out_shape=[jax.ShapeDtypeStruct((B * S, HY_WIDTH), F32)] * 3,
        compiler_params=_cparams(("arbitrary", "arbitrary")),
        name="hy_short_conv",
    )(p, p, p, conv_w, conv_b[None, :])


def _hy_rows_kernel(x_ref, fr_ref, fi_ref, ar_ref, ai_ref):
    xb = x_ref[0].astype(BF16)
    ar_ref[0] = jnp.dot(fr_ref[...], xb, preferred_element_type=F32)
    ai_ref[0] = jnp.dot(fi_ref[...], xb, preferred_element_type=F32)


def hy_dft_rows(x, fr, fi):
    G, K, cols = x.shape
    n1 = fr.shape[0]
    ct = min(HY_COL_TILE, cols)
    out = pl.BlockSpec((1, n1, ct), lambda g, j: (g, 0, j))
    return pl.pallas_call(
        _hy_rows_kernel,
        grid=(G, cols // ct),
        in_specs=[pl.BlockSpec((1, K, ct), lambda g, j: (g, 0, j)),
                  pl.BlockSpec((n1, K), lambda g, j: (0, 0)), pl.BlockSpec((n1, K), lambda g, j: (0, 0))],
        out_specs=[out, out],
        out_shape=[jax.ShapeDtypeStruct((G, n1, cols), F32)] * 2,
        compiler_params=_cparams(("arbitrary", "arbitrary")),
        name="hy_dft_rows",
    )(x, fr[:, :K], fi[:, :K])


def _hy_mid_kernel(ar_ref, ai_ref, twr_ref, twi_ref, f2r_ref, f2i_ref, *rest, conv):
    if conv:
        hr_ref, hi_ref, gr_ref, gi_ref = rest
    else:
        gr_ref, gi_ref = rest
    f2r, f2i = f2r_ref[...], f2i_ref[...]

    def mm(a, b):
        return jnp.dot(a, b, preferred_element_type=F32)
    for j in range(ar_ref.shape[1]):
        ar, ai = ar_ref[0, j], ai_ref[0, j]
        twr, twi = twr_ref[j], twi_ref[j]
        pr = (ar * twr - ai * twi).astype(BF16)
        pi = (ar * twi + ai * twr).astype(BF16)
        xr = mm(f2r, pr) - mm(f2i, pi)
        xi = mm(f2r, pi) + mm(f2i, pr)
        if not conv:
            gr_ref[0, j] = xr
            gi_ref[0, j] = xi
            continue
        hr, hi = hr_ref[0, j], hi_ref[0, j]
        yr = (xr * hr - xi * hi).astype(BF16)
        yi = (xr * hi + xi * hr).astype(BF16)
        gr = mm(f2r, yr) + mm(f2i, yi)
        gi = mm(f2r, yi) - mm(f2i, yr)
        gr_ref[0, j] = gr * twr + gi * twi
        gi_ref[0, j] = gi * twr - gr * twi


def hy_mid(ar, ai, tables, spectrum=None, order=0):
    G, n1 = ar.shape[0], ar.shape[1]
    C = ar.shape[-1]
    _, _, f2r, f2i, twr, twi = tables
    kt = HY_K1_TILE
    blk = pl.BlockSpec((1, kt, HY_N2, C), lambda g, i: (g, i, 0, 0))
    tw = pl.BlockSpec((kt, HY_N2, 1), lambda g, i: (i, 0, 0))
    mat = pl.BlockSpec((HY_N2, HY_N2), lambda g, i: (0, 0))
    ins, specs = [ar, ai, twr, twi, f2r, f2i], [blk, blk, tw, tw, mat, mat]
    if spectrum is not None:
        hspec = pl.BlockSpec((1, kt, HY_N2, C), lambda g, i: (order, i, 0, 0))
        ins, specs = ins + list(spectrum), specs + [hspec, hspec]
    return pl.pallas_call(
        functools.partial(_hy_mid_kernel, conv=spectrum is not None),
        grid=(G, n1 // kt),
        in_specs=specs,
        out_specs=[blk, blk],
        out_shape=[jax.ShapeDtypeStruct(ar.shape, F32)] * 2,
        compiler_params=_cparams(("arbitrary", "arbitrary")),
        name="hy_mid_conv" if spectrum is not None else "hy_mid_filter",
    )(*ins)


def _hy_out_kernel(gr_ref, gi_ref, fr_ref, fi_ref, z_ref, xm_ref, d_ref, o_ref, *, scale):
    y = jnp.dot(fr_ref[...], gr_ref[0].astype(BF16), preferred_element_type=F32)
    y = y + jnp.dot(fi_ref[...], gi_ref[0].astype(BF16), preferred_element_type=F32)
    o_ref[0] = xm_ref[0] * (y * scale + z_ref[0] * d_ref[...])


def hy_out(gr, gi, fr, fi, z, xmul, d_cols):
    G, n1, cols = gr.shape
    rows = z.shape[1]
    ct = min(HY_COL_TILE // 2, cols)
    spec = pl.BlockSpec((1, n1, ct), lambda g, j: (g, 0, j))
    half = pl.BlockSpec((1, rows, ct), lambda g, j: (g, 0, j))
    mat = pl.BlockSpec((rows, n1), lambda g, j: (0, 0))
    return pl.pallas_call(
        functools.partial(_hy_out_kernel, scale=1.0 / (n1 * HY_N2)),
        grid=(G, cols // ct),
        in_specs=[spec, spec, mat, mat, half, half, pl.BlockSpec((1, ct), lambda g, j: (0, j))],
        out_specs=half,
        out_shape=jax.ShapeDtypeStruct(z.shape, F32),
        compiler_params=_cparams(("arbitrary", "arbitrary")),
        name="hy_out",
    )(gr, gi, fr[:rows], fi[:rows], z, xmul, d_cols)


def hyena_taps(L, w1, b1, w2, b2, w3, freq, decay):
    n = jnp.arange(L, dtype=F32)
    t = n / max(L - 1, 1)
    bands = (HY_EMB - 1) // 2
    f = jnp.linspace(1e-4, bands - 1, bands, dtype=F32)
    ang = (2.0 * math.pi / L) * n[:, None] * f
    z = jnp.concatenate([t[:, None], jnp.cos(ang), -jnp.sin(ang)], axis=-1)
    a = jnp.sin(freq * (z @ w1 + b1))
    a = jnp.sin(freq * (a @ w2 + b2))
    w3r = w3.reshape(-1, HY_ORDER, 2, HY_WIDTH)
    dec = jnp.abs(decay).reshape(HY_ORDER, 2, HY_WIDTH)
    a_circ = jnp.concatenate([a, jnp.zeros((1, a.shape[1]), F32), a[:0:-1]], axis=0)
    t_circ = jnp.concatenate([t, jnp.zeros((1,), F32), t[:0:-1]])
    forward = (jnp.arange(2 * L) < L)[None, :, None]

    def side(d):
        return jnp.einsum('lf,foc->olc', a_circ, w3r[:, :, d]) * jnp.exp(-t_circ[None, :, None] * dec[:, d][:, None, :])
    taps = jnp.where(forward, side(0), side(1))
    return taps / jnp.sum(jnp.abs(taps), axis=1, keepdims=True)


def hyena_latents(p, conv_w, conv_b, taps, dskip, B, S):
    C = HY_WIDTH
    tables = _dft_tables(S)
    f1r, f1i = tables[0], tables[1]
    n1 = f1r.shape[0]
    cols = HY_N2 * C
    taps = taps.reshape(HY_ORDER, n1, cols)
    tr, ti = hy_dft_rows(taps, f1r, f1i)
    spectrum = hy_mid(tr.reshape(HY_ORDER, n1, HY_N2, C), ti.reshape(HY_ORDER, n1, HY_N2, C), tables)

    parts = hy_short_conv(p, conv_w, conv_b, B, S)
    rows = S // HY_N2
    z = parts[0].reshape(B, rows, cols)
    for o in range(HY_ORDER):
        ar, ai = hy_dft_rows(z, f1r, f1i)
        gr, gi = hy_mid(ar.reshape(B, n1, HY_N2, C), ai.reshape(B, n1, HY_N2, C), tables, spectrum, o)
        z = hy_out(gr.reshape(B, n1, cols), gi.reshape(B, n1, cols), f1r, f1i, z,
                   parts[o + 1].reshape(B, rows, cols), jnp.tile(dskip[o], HY_N2)[None, :])
    return z.reshape(B * S, C)


def hyena_context(pc, conv_w, conv_b, filt, dskip):
    L = pc.shape[1]
    u = centred_dwconv(pc, conv_w, conv_b)
    parts = jnp.split(u, HY_ORDER + 1, axis=-1)
    z = parts[0]
    for o in range(HY_ORDER):
        z = parts[o + 1] * long_conv(z, filt[:, o, 0], filt[:, o, 1], dskip[o])
    return z


def kernel(x, c, ctx, c_ctx, norm1_g, norm2_g, ada_w, ada_b, w_in, na_qnorm_g, na_knorm_g, na_rpb,
           ml_conv_w, ml_conv_b, ml_gate_b, ml_out_g, lru_conv_w, lru_conv_b, lru_wa, lru_ba, lru_wx, lru_bx,
           lru_lambda, hy_conv_w, hy_conv_b, hy_w1, hy_b1, hy_w2, hy_b2, hy_w3, hy_freq, hy_decay, hy_dskip,
           br_w, br_gate_b, w_o, router_w, router_b, exp_w_gate, exp_w_up, exp_w_down,
           sh_w_gate, sh_w_up, sh_w_down):
    B, S, D = x.shape
    assert ctx.shape[1] == CTX_LEN and D == D_MODEL
    n_ctx = B * CTX_LEN
    T = n_ctx + B * S
    assert n_ctx % ROW_TILE == 0 and S % ROW_TILE == 0 and T % ROUTE_TILE == 0
    tiles = dict(ctx_tiles=n_ctx // ROW_TILE, tiles_per_batch=S // ROW_TILE)
    pos = jnp.arange(S)
    rows, cols = pos // GRID_W, pos % GRID_W

    xa = jnp.concatenate([ctx.reshape(n_ctx, D), x.reshape(B * S, D)], axis=0)
    for l in range(DEPTH):
        need_ctx = l < DEPTH - 1
        cond = jnp.concatenate([c_ctx[None, :], c], axis=0)
        mod = (jax.nn.silu(cond) @ ada_w[l] + ada_b[l]).reshape(1 + B, 6, D)
        sh1sc1, g1, sh2sc2, g2 = mod[:, 0:2], mod[:, 2:3], mod[:, 3:5], mod[:, 5:6]

        w_l = w_in[l]
        w_main = jnp.concatenate([w_l[:, :IF_START], w_l[:, IF_START + IF_WIDTH:]], axis=1).astype(BF16)
        w_if = jnp.pad(w_l[:, IF_START:IF_START + IF_WIDTH], ((0, 0), (0, LANES - IF_WIDTH)))
        p = norm_mod_project(xa, norm1_g[l][None, :], sh1sc1, w_main, **tiles)
        p_if = norm_mod_narrow(xa, norm1_g[l][None, :], sh1sc1, w_if, False, **tiles)

        ya = neighbourhood_attention(p, na_rpb[l], na_qnorm_g[l], na_knorm_g[l], B, S)
        yb = mlstm(p, p_if, ml_conv_w[l], ml_conv_b[l], ml_gate_b[l], ml_out_g[l], B, S)
        yc = rglru(p, lru_conv_w[l], lru_conv_b[l], lru_wa[l], lru_ba[l], lru_wx[l], lru_bx[l], lru_lambda[l], B, S)
        hy_par = (hy_w1[l], hy_b1[l], hy_w2[l], hy_b2[l], hy_w3[l], hy_freq[l], hy_decay[l])
        yd_lat = hyena_latents(p, hy_conv_w[l], hy_conv_b[l], hyena_taps(S, *hy_par), hy_dskip[l], B, S)
        if need_ctx:
            pc = p[:n_ctx, HY_OFF:GATE_OFF].reshape(B, CTX_LEN, -1)
            yd_ctx = hyena_context(pc, hy_conv_w[l], hy_conv_b[l], hyena_filters(CTX_LEN, *hy_par), hy_dskip[l])
            yd_ctx = yd_ctx.reshape(n_ctx, HY_WIDTH)
        else:
            yd_ctx = jnp.zeros((n_ctx, HY_WIDTH), F32)
        yd = jnp.concatenate([yd_ctx, yd_lat], axis=0)
        xa = merge_branches([ya, yb, yc, yd], p, br_gate_b[l], br_w[l].astype(BF16), w_o[l].astype(BF16), xa, g1,
                            **tiles)

        logits, h2 = norm_mod_narrow(xa, norm2_g[l][None, :], sh2sc2, router_w[l], True, **tiles)
        idx, wts, rank, counts = route_tokens(logits, router_b[l])
        slot_of, slot_tok, block_exp, n_used = slot_layout(idx, rank, counts, T)
        ys = routed_experts(block_exp, n_used, slot_tok, h2, exp_w_gate, exp_w_up, exp_w_down, l)
        xa = ffn_out(h2, ys, slot_of, wts.T, xa, g2, sh_w_gate[l].astype(BF16), sh_w_up[l].astype(BF16),
                     sh_w_down[l].astype(BF16), n_ctx, S)
    return xa[n_ctx:].reshape(B, S, D)
```

```python
import functools
import math

import numpy as np
import jax
import jax.numpy as jnp
from jax import lax
from jax.experimental import pallas as pl
from jax.experimental.pallas import tpu as pltpu

D_MODEL = 2048
DEPTH = 2
CTX_LEN = 256
GRID_W = 64
NORM_EPS = 1e-6
N_BRANCH = 4
BRANCH_WIDTH = D_MODEL // 4
HEAD_DIM = 128

NA_HEADS = BRANCH_WIDTH // HEAD_DIM
NA_WIN_ROWS = 8
NA_WIN_COLS = 16

ML_HEADS = BRANCH_WIDTH // HEAD_DIM
ML_WIDTH = BRANCH_WIDTH
ML_CHUNK = 128
ROPE_BASE = 10000.0

LRU_WIDTH = BRANCH_WIDTH
LRU_BLOCKS = 4
LRU_BW = LRU_WIDTH // LRU_BLOCKS
LRU_C = 8.0
LRU_CONV = 4

HY_WIDTH = BRANCH_WIDTH
HY_ORDER = 2
HY_EMB = 33
HY_N2 = 256
HY_K1_TILE = 2
HY_COL_TILE = 8192

N_EXPERTS = 128
TOP_K = 8
N_GROUPS = 8
GROUP_SIZE = N_EXPERTS // N_GROUPS
TOPK_GROUPS = 4
ROUTE_SCALE = 2.5

IN_SPLITS = (3 * BRANCH_WIDTH, 3 * ML_WIDTH, ML_WIDTH, 4 * ML_HEADS, LRU_WIDTH, LRU_WIDTH,
             (HY_ORDER + 1) * HY_WIDTH, N_BRANCH * D_MODEL)
IF_START = sum(IN_SPLITS[:3])
IF_WIDTH = IN_SPLITS[3]
MAIN_SPLITS = IN_SPLITS[:3] + IN_SPLITS[4:]
MAIN_OFF = np.concatenate([[0], np.cumsum(MAIN_SPLITS)]).tolist()
MAIN_WIDTH = MAIN_OFF[-1]
NA_OFF, LRU_X_OFF, LRU_G_OFF, HY_OFF, GATE_OFF = MAIN_OFF[0], MAIN_OFF[3], MAIN_OFF[4], MAIN_OFF[5], MAIN_OFF[6]

LANES = 128
SUBLANES = 8
ROW_SLAB = D_MODEL // LANES
ROW_TILE = 512
PROJ_COL_TILE = 1024
MERGE_COL_TILE = 512
MOE_ROW_TILE = 512
FFN_TILE = 128
ROUTE_TILE = 512
SEQ_CHUNK = CTX_LEN
NA_G = 4
NEG = -1e30
VMEM_LIMIT = 56 * 1024 * 1024

F32 = jnp.float32
BF16 = jnp.bfloat16


def _cparams(sem):
    return pltpu.CompilerParams(dimension_semantics=sem, vmem_limit_bytes=VMEM_LIMIT)


def _seg_of_tile(i, ctx_tiles, tiles_per_batch):
    return jnp.where(i < ctx_tiles, 0, 1 + (i - ctx_tiles) // tiles_per_batch)


def _norm_mod(x, g, shift, scale):
    ms = jnp.mean(x * x, axis=-1, keepdims=True)
    return (x * lax.rsqrt(ms + NORM_EPS) * g) * (1.0 + scale) + shift


def _head_norm(x, g):
    return x * lax.rsqrt(jnp.mean(x * x, axis=-1, keepdims=True) + NORM_EPS) * g


def _proj_kernel(x_ref, g_ref, mod_ref, w_ref, o_ref, h_scr):
    @pl.when(pl.program_id(1) == 0)
    def _():
        h = _norm_mod(x_ref[...], g_ref[...], mod_ref[0, 0:1, :], mod_ref[0, 1:2, :])
        h_scr[...] = h.astype(BF16)
    o_ref[...] = jnp.dot(h_scr[...], w_ref[...], preferred_element_type=F32)


def norm_mod_project(x, g, mod, w, ctx_tiles, tiles_per_batch):
    T, D = x.shape
    N = w.shape[1]
    seg = functools.partial(_seg_of_tile, ctx_tiles=ctx_tiles, tiles_per_batch=tiles_per_batch)
    return pl.pallas_call(
        _proj_kernel,
        grid=(T // ROW_TILE, N // PROJ_COL_TILE),
        in_specs=[
            pl.BlockSpec((ROW_TILE, D), lambda i, j: (i, 0)),
            pl.BlockSpec((1, D), lambda i, j: (0, 0)),
            pl.BlockSpec((1, 2, D), lambda i, j: (seg(i), 0, 0)),
            pl.BlockSpec((D, PROJ_COL_TILE), lambda i, j: (0, j)),
        ],
        out_specs=pl.BlockSpec((ROW_TILE, PROJ_COL_TILE), lambda i, j: (i, j)),
        out_shape=jax.ShapeDtypeStruct((T, N), F32),
        scratch_shapes=[pltpu.VMEM((ROW_TILE, D), BF16)],
        compiler_params=_cparams(("arbitrary", "arbitrary")),
        name="norm_mod_project",
    )(x, g, mod, w)


def _narrow_kernel(x_ref, g_ref, mod_ref, w_ref, o_ref, *h_ref):
    h = _norm_mod(x_ref[...], g_ref[...], mod_ref[0, 0:1, :], mod_ref[0, 1:2, :])
    if h_ref:
        h_ref[0][...] = h
    o_ref[...] = jnp.dot(h, w_ref[...], preferred_element_type=F32, precision=lax.Precision.HIGHEST)


def norm_mod_narrow(x, g, mod, w, keep_h, ctx_tiles, tiles_per_batch):
    T, D = x.shape
    seg = functools.partial(_seg_of_tile, ctx_tiles=ctx_tiles, tiles_per_batch=tiles_per_batch)
    row = pl.BlockSpec((ROW_TILE, D), lambda i: (i, 0))
    out = pl.pallas_call(
        _narrow_kernel,
        grid=(T // ROW_TILE,),
        in_specs=[
            row,
            pl.BlockSpec((1, D), lambda i: (0, 0)),
            pl.BlockSpec((1, 2, D), lambda i: (seg(i), 0, 0)),
            pl.BlockSpec((D, LANES), lambda i: (0, 0)),
        ],
        out_specs=[pl.BlockSpec((ROW_TILE, LANES), lambda i: (i, 0))] + ([row] if keep_h else []),
        out_shape=[jax.ShapeDtypeStruct((T, LANES), F32)] + ([jax.ShapeDtypeStruct((T, D), F32)] if keep_h else []),
        compiler_params=_cparams(("arbitrary",)),
        name="norm_mod_narrow",
    )(x, g, mod, w)
    return out if keep_h else out[0]


def _na_kernel(q_ref, k0_ref, k1_ref, k2_ref, v0_ref, v1_ref, v2_ref, kc_ref, vc_ref, bias_ref, qg_ref, kg_ref,
               o_ref, *, n_rows):
    s = pl.program_id(1)
    qb, kw = bias_ref.shape[1], bias_ref.shape[2]
    first_row = (s - 1) * NA_G
    q_row = first_row + lax.broadcasted_iota(jnp.int32, (qb, kw), 0) // GRID_W
    k_row = first_row - NA_G + lax.broadcasted_iota(jnp.int32, (qb, kw), 1) // GRID_W
    r0 = jnp.clip(q_row - NA_WIN_ROWS // 2, 0, n_rows - NA_WIN_ROWS)
    valid = (k_row >= r0) & (k_row < r0 + NA_WIN_ROWS) & (s > 0)
    nt = (((1,), (1,)), ((), ()))
    for h in range(NA_HEADS):
        hs = slice(h * HEAD_DIM, (h + 1) * HEAD_DIM)
        qn = (_head_norm(q_ref[:, hs], qg_ref[...]) * (HEAD_DIM ** -0.5)).astype(BF16)
        k_win = jnp.concatenate([k0_ref[:, hs], k1_ref[:, hs], k2_ref[:, hs]], axis=0)
        kn = _head_norm(k_win, kg_ref[...]).astype(BF16)
        s_lat = lax.dot_general(qn, kn, nt, preferred_element_type=F32)
        s_lat = jnp.where(valid, s_lat + bias_ref[h], NEG)
        kcn = _head_norm(kc_ref[:, hs], kg_ref[...]).astype(BF16)
        s_ctx = lax.dot_general(qn, kcn, nt, preferred_element_type=F32)
        m = jnp.maximum(jnp.max(s_lat, axis=-1, keepdims=True), jnp.max(s_ctx, axis=-1, keepdims=True))
        p_lat = jnp.exp(s_lat - m)
        p_ctx = jnp.exp(s_ctx - m)
        denom = jnp.sum(p_lat, axis=-1, keepdims=True) + jnp.sum(p_ctx, axis=-1, keepdims=True)
        v_win = jnp.concatenate([v0_ref[:, hs], v1_ref[:, hs], v2_ref[:, hs]], axis=0).astype(BF16)
        o = jnp.dot(p_lat.astype(BF16), v_win, preferred_element_type=F32)
        o = o + jnp.dot(p_ctx.astype(BF16), vc_ref[:, hs].astype(BF16), preferred_element_type=F32)
        o_ref[:, hs] = o / denom


def na_bias_table(rpb):
    W, k_rows = GRID_W, 3 * NA_G
    qb, kw = NA_G * W, k_rows * W
    qc = np.arange(qb)[:, None] % W
    kc = np.arange(kw)[None, :] % W
    c0 = np.clip(qc - NA_WIN_COLS // 2, 0, W - NA_WIN_COLS)
    ok = (kc >= c0) & (kc < c0 + NA_WIN_COLS)
    lo = W - NA_WIN_COLS
    padded = jnp.pad(rpb, ((0, 0), (0, 0), (lo, lo)))
    by_col = jnp.stack([padded[:, :, W - 1 - c:2 * W - 1 - c] for c in range(W)], axis=2)
    assert NA_WIN_ROWS - 1 - NA_G - (NA_G - 1) >= 0 and k_rows - 1 - NA_G + NA_WIN_ROWS - 1 <= 2 * NA_WIN_ROWS - 2
    per_g = [jnp.stack([by_col[:, kr - NA_G - g + NA_WIN_ROWS - 1] for kr in range(k_rows)], axis=2)
             for g in range(NA_G)]
    table = jnp.stack(per_g, axis=1).reshape(rpb.shape[0], qb, kw)
    return jnp.where(ok[None], table, NEG)


def neighbourhood_attention(p, rpb, qg, kg, B, S):
    T = p.shape[0]
    qb = NA_G * GRID_W
    assert qb == CTX_LEN and NA_WIN_ROWS // 2 <= NA_G and NA_WIN_ROWS - NA_WIN_ROWS // 2 - 1 <= NA_G
    ctx_blocks, n_blk, n_rows = B * CTX_LEN // qb, S // qb, S // GRID_W
    c_q = NA_OFF // BRANCH_WIDTH
    c_k, c_v = c_q + 1, c_q + 2

    def q_map(b, s):
        return (jnp.where(s == 0, b, ctx_blocks + b * n_blk + s - 1), c_q)

    def win_spec(d, col):
        return pl.BlockSpec((qb, BRANCH_WIDTH),
                            lambda b, s: (ctx_blocks + b * n_blk + jnp.clip(s - 1 + d, 0, n_blk - 1), col))

    bias = na_bias_table(rpb)
    return pl.pallas_call(
        functools.partial(_na_kernel, n_rows=n_rows),
        grid=(B, 1 + n_blk),
        in_specs=[pl.BlockSpec((qb, BRANCH_WIDTH), q_map),
                  win_spec(-1, c_k), win_spec(0, c_k), win_spec(1, c_k),
                  win_spec(-1, c_v), win_spec(0, c_v), win_spec(1, c_v),
                  pl.BlockSpec((qb, BRANCH_WIDTH), lambda b, s: (b, c_k)),
                  pl.BlockSpec((qb, BRANCH_WIDTH), lambda b, s: (b, c_v)),
                  pl.BlockSpec(bias.shape, lambda b, s: (0, 0, 0)),
                  pl.BlockSpec((1, HEAD_DIM), lambda b, s: (0, 0)),
                  pl.BlockSpec((1, HEAD_DIM), lambda b, s: (0, 0))],
        out_specs=pl.BlockSpec((qb, BRANCH_WIDTH), lambda b, s: (q_map(b, s)[0], 0)),
        out_shape=jax.ShapeDtypeStruct((T, BRANCH_WIDTH), F32),
        compiler_params=_cparams(("arbitrary", "arbitrary")),
        name="neighbourhood_attention",
    )(p, p, p, p, p, p, p, p, p, bias, qg[None, :], kg[None, :])


def _chunk_edges(reverse):
    s = pl.program_id(1)
    n_lat = pl.num_programs(1) - 1
    lat_idx = (n_lat - s) if reverse else (s - 1)
    has_prev = jnp.where((s > 0) & (lat_idx > 0), 1.0, 0.0)
    has_next = jnp.where((s > 0) & (lat_idx < n_lat - 1), 1.0, 0.0)
    return has_prev, has_next


def _conv4(x_ref, prev_ref, next_ref, cw_ref, cb_ref, has_prev, has_next):
    x = x_ref[...]
    tc, w = x.shape
    row = lax.broadcasted_iota(jnp.int32, (tc, w), 0)
    before = prev_ref[SUBLANES - 1:SUBLANES, :] * has_prev
    after0 = next_ref[0:1, :] * has_next
    after1 = next_ref[1:2, :] * has_next
    xm1 = jnp.where(row == 0, before, pltpu.roll(x, 1, 0))
    xp1 = jnp.where(row == tc - 1, after0, pltpu.roll(x, tc - 1, 0))
    xp2 = jnp.where(row == tc - 2, after0, jnp.where(row == tc - 1, after1, pltpu.roll(x, tc - 2, 0)))
    return cb_ref[...] + xm1 * cw_ref[0:1, :] + x * cw_ref[1:2, :] + xp1 * cw_ref[2:3, :] + xp2 * cw_ref[3:4, :]


def _seq_specs(reverse, B, S, T, width):
    tc = SEQ_CHUNK
    n_lat, ctx_blocks, halo = S // tc, B, tc // SUBLANES

    def blk(b, s):
        lat = (n_lat - s) if reverse else (s - 1)
        return jnp.where(s == 0, b, ctx_blocks + b * n_lat + lat)

    def chunk(col):
        return pl.BlockSpec((tc, width), lambda b, s: (blk(b, s), col))

    def prev(col):
        return pl.BlockSpec((SUBLANES, width), lambda b, s: (jnp.maximum(blk(b, s) * halo - 1, 0), col))

    def nxt(col):
        return pl.BlockSpec((SUBLANES, width), lambda b, s: (jnp.minimum((blk(b, s) + 1) * halo, T // SUBLANES - 1), col))

    def full(shape):
        return pl.BlockSpec(shape, lambda b, s: (0,) * len(shape))
    return blk, chunk, prev, nxt, full


def _lru_kernel(x_ref, prev_ref, next_ref, cw_ref, cb_ref, wa_ref, ba_ref, wx_ref, bx_ref, sp_ref, *rest, reverse):
    if reverse:
        hf_ref, g_ref, o_ref, carry = rest
    else:
        o_ref, carry = rest
    s = pl.program_id(1)
    tc, w = x_ref.shape

    @pl.when(s == 0)
    def _():
        carry[...] = jnp.zeros_like(carry)

    has_prev, has_next = _chunk_edges(reverse)
    xc = _conv4(x_ref, prev_ref, next_ref, cw_ref, cb_ref, has_prev, has_next)
    row = lax.broadcasted_iota(jnp.int32, (tc, w), 0)

    def block_diag(w_ref):
        return jnp.concatenate(
            [jnp.dot(xc[:, n * LRU_BW:(n + 1) * LRU_BW], w_ref[n], preferred_element_type=F32,
                     precision=lax.Precision.HIGHEST) for n in range(LRU_BLOCKS)], axis=-1)
    r = jax.nn.sigmoid(block_diag(wa_ref) + ba_ref[...])
    i = jax.nn.sigmoid(block_diag(wx_ref) + bx_ref[...])
    log_a = -LRU_C * r * sp_ref[...]
    a = jnp.exp(log_a)
    u = jnp.sqrt(1.0 - jnp.exp(2.0 * log_a)) * (i * xc)

    step = 1
    while step < tc:
        if reverse:
            keep = row < tc - step
            shift = tc - step
        else:
            keep = row >= step
            shift = step
        a_sh = jnp.where(keep, pltpu.roll(a, shift, 0), 1.0)
        u_sh = jnp.where(keep, pltpu.roll(u, shift, 0), 0.0)
        u = a * u_sh + u
        a = a * a_sh
        step *= 2
    h = u + a * carry[...]
    carry[...] = h[0:1, :] if reverse else h[tc - 1:tc, :]
    if reverse:
        o_ref[...] = (hf_ref[...] + h) * jax.nn.gelu(g_ref[...])
    else:
        o_ref[...] = h


def rglru(p, conv_w, conv_b, wa, ba, wx, bx, lam, B, S):
    T = p.shape[0]
    tc, w = SEQ_CHUNK, LRU_WIDTH
    assert CTX_LEN == tc and S % tc == 0 and LRU_CONV == 4
    c_x, c_g = LRU_X_OFF // w, LRU_G_OFF // w
    sp = jax.nn.softplus(-lam)

    def call(reverse, extra_in, extra_cols):
        blk, chunk, prev, nxt, full = _seq_specs(reverse, B, S, T, w)
        d = 1 if reverse else 0
        return pl.pallas_call(
            functools.partial(_lru_kernel, reverse=reverse),
            grid=(B, 1 + S // tc),
            in_specs=[chunk(c_x), prev(c_x), nxt(c_x),
                      full((LRU_CONV, w)), full((1, w)), full((LRU_BLOCKS, LRU_BW, LRU_BW)), full((1, w)),
                      full((LRU_BLOCKS, LRU_BW, LRU_BW)), full((1, w)), full((1, w))] + [chunk(c) for c in extra_cols],
            out_specs=chunk(0),
            out_shape=jax.ShapeDtypeStruct((T, w), F32),
            scratch_shapes=[pltpu.VMEM((1, w), F32)],
            compiler_params=_cparams(("arbitrary", "arbitrary")),
            name="rglru_bwd" if reverse else "rglru_fwd",
        )(p, p, p, conv_w, conv_b[None, :], wa[d], ba[d][None, :], wx[d], bx[d][None, :], sp[d][None, :], *extra_in)

    h_fwd = call(False, [], [])
    return call(True, [h_fwd, p], [0, c_g])


def _log_sigmoid(x):
    return jnp.minimum(x, 0.0) - jnp.log(1.0 + jnp.exp(-jnp.abs(x)))


def _mlstm_kernel(q_ref, qp_ref, qn_ref, k_ref, kp_ref, kn_ref, v_ref, if_ref, cos_ref, sin_ref, cwq_ref, cbq_ref,
                  cwk_ref, cbk_ref, gb_ref, *rest, reverse):
    if reverse:
        hf_ref, og_ref, outg_ref, o_ref, c_scr, n_scr, m_scr = rest
    else:
        o_ref, c_scr, n_scr, m_scr = rest
    s = pl.program_id(1)
    tc = q_ref.shape[0]
    ck = ML_CHUNK
    d0 = (1 if reverse else 0) * 2 * ML_HEADS

    @pl.when(s == 0)
    def _():
        c_scr[...] = jnp.zeros_like(c_scr)
        n_scr[...] = jnp.zeros_like(n_scr)
        m_scr[...] = jnp.zeros_like(m_scr)

    has_prev, has_next = _chunk_edges(reverse)
    q_all = jax.nn.silu(_conv4(q_ref, qp_ref, qn_ref, cwq_ref, cbq_ref, has_prev, has_next))
    k_all = jax.nn.silu(_conv4(k_ref, kp_ref, kn_ref, cwk_ref, cbk_ref, has_prev, has_next))

    is_lat = s > 0
    cos = jnp.where(is_lat, cos_ref[...], 1.0)
    sin = jnp.where(is_lat, sin_ref[...], 0.0)
    lane = lax.broadcasted_iota(jnp.int32, (tc, HEAD_DIM), 1)
    low_quarter = (lane % (HEAD_DIM // 2)) < (HEAD_DIM // 4)

    def rope(xh):
        swapped = jnp.where(low_quarter, pltpu.roll(xh, HEAD_DIM - HEAD_DIM // 4, 1), pltpu.roll(xh, HEAD_DIM // 4, 1))
        return xh * cos + swapped * sin

    row = lax.broadcasted_iota(jnp.int32, (ck, ck), 0)
    col = lax.broadcasted_iota(jnp.int32, (ck, ck), 1)
    causal = (col >= row) if reverse else (col <= row)
    nt = (((1,), (1,)), ((), ()))
    tn = (((0,), (0,)), ((), ()))

    qs, ks = [], []
    for h in range(ML_HEADS):
        hs = slice(h * HEAD_DIM, (h + 1) * HEAD_DIM)
        qs.append(rope(q_all[:, hs]))
        ks.append(rope(k_all[:, hs]) * (HEAD_DIM ** -0.5))

    subs = range(tc // ck)
    for sub in (reversed(subs) if reverse else subs):
        rs = slice(sub * ck, (sub + 1) * ck)
        gates = if_ref[rs, :] + gb_ref[...]
        b_all = _log_sigmoid(gates)
        step = 1
        while step < ck:
            if reverse:
                b_all = b_all + jnp.where(row < ck - step, pltpu.roll(b_all, ck - step, 0), 0.0)
            else:
                b_all = b_all + jnp.where(row >= step, pltpu.roll(b_all, step, 0), 0.0)
            step *= 2
        b_all_t = b_all.T
        gates_t = gates.T
        last = 0 if reverse else ck - 1
        for h in range(ML_HEADS):
            hs = slice(h * HEAD_DIM, (h + 1) * HEAD_DIM)
            ci, cf = d0 + h, d0 + ML_HEADS + h
            qh, kh, vh = qs[h][rs], ks[h][rs], v_ref[rs, hs]
            b_col, b_row = b_all[:, cf:cf + 1], b_all_t[cf:cf + 1, :]
            i_col, i_row = gates[:, ci:ci + 1], gates_t[ci:ci + 1, :]
            b_last = b_all[last:last + 1, cf:cf + 1]
            m_prev = m_scr[h:h + 1, 0:1]
            d = jnp.where(causal, b_col - b_row + i_row, -jnp.inf)
            m_t = jnp.maximum(b_col + m_prev, jnp.max(d, axis=-1, keepdims=True))
            qb, kb, vb = qh.astype(BF16), kh.astype(BF16), vh.astype(BF16)
            w = jnp.exp(d - m_t) * lax.dot_general(qb, kb, nt, preferred_element_type=F32)
            carry_w = jnp.exp(b_col + m_prev - m_t)
            num = jnp.dot(w.astype(BF16), vb, preferred_element_type=F32)
            num = num + carry_w * jnp.dot(qb, c_scr[h].astype(BF16), preferred_element_type=F32)
            den = jnp.sum(w, axis=-1, keepdims=True) + carry_w * jnp.sum(qh * n_scr[h:h + 1, :], axis=-1, keepdims=True)
            hh = num / jnp.maximum(jnp.abs(den), jnp.exp(-m_t))
            g_col = b_last - b_col + i_col
            m_new = jnp.maximum(b_last + m_prev, jnp.max(g_col, axis=0, keepdims=True))
            keep = jnp.exp(b_last + m_prev - m_new)
            kw = kh * jnp.exp(g_col - m_new)
            c_scr[h] = keep * c_scr[h] + lax.dot_general(kw.astype(BF16), vb, tn, preferred_element_type=F32)
            n_scr[h:h + 1, :] = keep * n_scr[h:h + 1, :] + jnp.sum(kw, axis=0, keepdims=True)
            m_scr[h:h + 1, :] = jnp.broadcast_to(m_new, (1, LANES))
            if reverse:
                hsum = hf_ref[rs, hs] + hh
                o_ref[rs, hs] = _head_norm(hsum, outg_ref[:, hs]) * jax.nn.sigmoid(og_ref[rs, hs])
            else:
                o_ref[rs, hs] = hh


def rope_tables(S):
    quarter = HEAD_DIM // 4
    inv = ROPE_BASE ** (-jnp.arange(quarter, dtype=F32) / quarter)
    pos = jnp.arange(S)
    ang_r = (pos // GRID_W).astype(F32)[:, None] * inv
    ang_c = (pos % GRID_W).astype(F32)[:, None] * inv
    cos = jnp.concatenate([jnp.cos(ang_r), jnp.cos(ang_r), jnp.cos(ang_c), jnp.cos(ang_c)], axis=-1)
    sin = jnp.concatenate([-jnp.sin(ang_r), jnp.sin(ang_r), -jnp.sin(ang_c), jnp.sin(ang_c)], axis=-1)
    return cos, sin


def mlstm(p, p_if, conv_w, conv_b, gate_b, out_g, B, S):
    T = p.shape[0]
    tc, w = SEQ_CHUNK, ML_WIDTH
    assert CTX_LEN == tc and S % tc == 0 and tc % ML_CHUNK == 0 and conv_w.shape[0] == 4
    c_q = MAIN_OFF[1] // w
    c_k, c_v, c_o = c_q + 1, c_q + 2, MAIN_OFF[2] // w
    cos, sin = rope_tables(S)
    gb = jnp.pad(gate_b.reshape(1, -1), ((0, 0), (0, LANES - gate_b.size)))
    n_lat = S // tc

    def call(reverse, extra_in, extra_specs):
        blk, chunk, prev, nxt, full = _seq_specs(reverse, B, S, T, w)

        def lat_rows(b, s):
            return (jnp.clip((n_lat - s) if reverse else (s - 1), 0, n_lat - 1), 0)
        table = pl.BlockSpec((tc, HEAD_DIM), lat_rows)
        return pl.pallas_call(
            functools.partial(_mlstm_kernel, reverse=reverse),
            grid=(B, 1 + n_lat),
            in_specs=[chunk(c_q), prev(c_q), nxt(c_q), chunk(c_k), prev(c_k), nxt(c_k), chunk(c_v),
                      pl.BlockSpec((tc, LANES), lambda b, s: (blk(b, s), 0)), table, table,
                      full((4, w)), full((1, w)), full((4, w)), full((1, w)), full((1, LANES))] + extra_specs(chunk, full),
            out_specs=chunk(0),
            out_shape=jax.ShapeDtypeStruct((T, w), F32),
            scratch_shapes=[pltpu.VMEM((ML_HEADS, HEAD_DIM, HEAD_DIM), F32), pltpu.VMEM((SUBLANES, HEAD_DIM), F32),
                            pltpu.VMEM((SUBLANES, LANES), F32)],
            compiler_params=_cparams(("arbitrary", "arbitrary")),
            name="mlstm_bwd" if reverse else "mlstm_fwd",
        )(p, p, p, p, p, p, p, p_if, cos, sin, conv_w[:, :w], conv_b[None, :w], conv_w[:, w:], conv_b[None, w:], gb,
          *extra_in)

    h_fwd = call(False, [], lambda chunk, full: [])
    return call(True, [h_fwd, p, out_g[None, :]], lambda chunk, full: [chunk(0), chunk(c_o), full((1, w))])


def _merge_kernel(ya_ref, yb_ref, yc_ref, yd_ref, ga_ref, gb_ref, gc_ref, gd_ref, bias_ref, bw_ref, wo_ref,
                  x_ref, g1_ref, o_ref, acc_ref):
    n = pl.program_id(1)
    z = None
    for b, (y_ref, gate_ref) in enumerate(((ya_ref, ga_ref), (yb_ref, gb_ref), (yc_ref, gc_ref), (yd_ref, gd_ref))):
        t = jnp.dot(y_ref[...].astype(BF16), bw_ref[b], preferred_element_type=F32)
        t = jax.nn.sigmoid(gate_ref[...] + bias_ref[b:b + 1, :]) * t
        z = t if z is None else z + t
    part = jnp.dot(z.astype(BF16), wo_ref[...], preferred_element_type=F32)

    @pl.when(n == 0)
    def _():
        acc_ref[...] = part

    @pl.when(n > 0)
    def _():
        acc_ref[...] += part

    @pl.when(n == pl.num_programs(1) - 1)
    def _():
        o_ref[...] = x_ref[...] + g1_ref[0] * acc_ref[...]


def merge_branches(ys, p, bias, bw, wo, x, g1, ctx_tiles, tiles_per_batch):
    T, D = x.shape
    ct = MERGE_COL_TILE
    seg = functools.partial(_seg_of_tile, ctx_tiles=ctx_tiles, tiles_per_batch=tiles_per_batch)
    y_spec = pl.BlockSpec((ROW_TILE, BRANCH_WIDTH), lambda i, n: (i, 0))

    def gate_spec(b):
        base = (GATE_OFF + b * D) // ct
        return pl.BlockSpec((ROW_TILE, ct), lambda i, n: (i, base + n))

    return pl.pallas_call(
        _merge_kernel,
        grid=(T // ROW_TILE, D // ct),
        in_specs=[y_spec, y_spec, y_spec, y_spec,
                  gate_spec(0), gate_spec(1), gate_spec(2), gate_spec(3),
                  pl.BlockSpec((N_BRANCH, ct), lambda i, n: (0, n)),
                  pl.BlockSpec((N_BRANCH, BRANCH_WIDTH, ct), lambda i, n: (0, 0, n)),
                  pl.BlockSpec((ct, D), lambda i, n: (n, 0)),
                  pl.BlockSpec((ROW_TILE, D), lambda i, n: (i, 0)),
                  pl.BlockSpec((1, 1, D), lambda i, n: (seg(i), 0, 0))],
        out_specs=pl.BlockSpec((ROW_TILE, D), lambda i, n: (i, 0)),
        out_shape=jax.ShapeDtypeStruct((T, D), F32),
        scratch_shapes=[pltpu.VMEM((ROW_TILE, D), F32)],
        compiler_params=_cparams(("arbitrary", "arbitrary")),
        name="merge_branches",
    )(*ys, p, p, p, p, bias, bw, wo, x, g1)


def _route_kernel(lg_ref, rb_ref, idx_ref, w_ref, rank_ref, cnt_ref, run_scr):
    @pl.when(pl.program_id(0) == 0)
    def _():
        run_scr[...] = jnp.zeros_like(run_scr)

    tm = lg_ref.shape[0]
    E = N_EXPERTS
    scores = jax.nn.sigmoid(lg_ref[...].T)
    biased = scores + rb_ref[...]
    ninf = -jnp.inf

    gscore = []
    for g in range(N_GROUPS):
        blk = biased[g * GROUP_SIZE:(g + 1) * GROUP_SIZE, :]
        m1 = jnp.max(blk, axis=0, keepdims=True)
        n_top = jnp.sum(jnp.where(blk == m1, 1.0, 0.0), axis=0, keepdims=True)
        m2 = jnp.max(jnp.where(blk < m1, blk, ninf), axis=0, keepdims=True)
        gscore.append(m1 + jnp.where(n_top >= 2.0, m1, m2))
    parts = []
    for g in range(N_GROUPS):
        beaten = jnp.zeros_like(gscore[g])
        for o in range(N_GROUPS):
            if o != g:
                wins = (gscore[o] >= gscore[g]) if o < g else (gscore[o] > gscore[g])
                beaten = beaten + jnp.where(wins, 1.0, 0.0)
        keep = beaten < float(TOPK_GROUPS)
        parts.append(jnp.where(keep, biased[g * GROUP_SIZE:(g + 1) * GROUP_SIZE, :], ninf))
    cur = jnp.concatenate(parts, axis=0)

    eidx = lax.broadcasted_iota(jnp.int32, (E, tm), 0).astype(F32)
    picks, wts = [], []
    sel = jnp.zeros((E, tm), F32)
    for _ in range(TOP_K):
        m = jnp.max(cur, axis=0, keepdims=True)
        ik = jnp.min(jnp.where(cur == m, eidx, float(E)), axis=0, keepdims=True)
        hit = eidx == ik
        picks.append(ik)
        wts.append(jnp.sum(jnp.where(hit, scores, 0.0), axis=0, keepdims=True))
        sel = sel + jnp.where(hit, 1.0, 0.0)
        cur = jnp.where(hit, ninf, cur)
    total = wts[0]
    for k in range(1, TOP_K):
        total = total + wts[k]

    before = lax.broadcasted_iota(jnp.int32, (tm, tm), 0) < lax.broadcasted_iota(jnp.int32, (tm, tm), 1)
    prefix = jnp.dot(sel.astype(BF16), jnp.where(before, 1.0, 0.0).astype(BF16), preferred_element_type=F32)
    base = prefix + run_scr[:, 0:1]
    for k in range(TOP_K):
        idx_ref[k:k + 1, :] = picks[k].astype(jnp.int32)
        w_ref[k:k + 1, :] = wts[k] / total * ROUTE_SCALE
        rank_ref[k:k + 1, :] = jnp.sum(jnp.where(eidx == picks[k], base, 0.0), axis=0, keepdims=True).astype(jnp.int32)
    run_scr[...] = run_scr[...] + jnp.sum(sel, axis=1, keepdims=True)
    cnt_ref[...] = run_scr[...]


def route_tokens(logits, router_b):
    T, E = logits.shape
    tm = ROUTE_TILE
    per_k = pl.BlockSpec((TOP_K, tm), lambda i: (0, i))
    idx, wts, rank, cnt = pl.pallas_call(
        _route_kernel,
        grid=(T // tm,),
        in_specs=[pl.BlockSpec((tm, E), lambda i: (i, 0)), pl.BlockSpec((E, 1), lambda i: (0, 0))],
        out_specs=[per_k, per_k, per_k, pl.BlockSpec((E, LANES), lambda i: (0, 0))],
        out_shape=[jax.ShapeDtypeStruct((TOP_K, T), jnp.int32), jax.ShapeDtypeStruct((TOP_K, T), F32),
                   jax.ShapeDtypeStruct((TOP_K, T), jnp.int32), jax.ShapeDtypeStruct((E, LANES), F32)],
        scratch_shapes=[pltpu.VMEM((E, LANES), F32)],
        compiler_params=_cparams(("arbitrary",)),
        name="route_tokens",
    )(logits, router_b[:, None])
    return idx, wts, rank, cnt[:, 0].astype(jnp.int32)


def slot_layout(idx, rank, counts, T):
    E, tb = N_EXPERTS, MOE_ROW_TILE
    padded = (counts + tb - 1) // tb * tb
    pend = jnp.cumsum(padded)
    first = pend - padded
    experts = jnp.arange(E, dtype=jnp.int32)
    slot_of = jnp.sum(jnp.where(idx[:, :, None] == experts, first, 0), axis=-1) + rank
    n_blocks = -(-T * TOP_K // tb) + E
    tok = jnp.broadcast_to(jnp.arange(T, dtype=jnp.int32), (TOP_K, T))
    slot_tok = jnp.zeros((n_blocks * tb,), jnp.int32).at[slot_of.reshape(-1)].set(tok.reshape(-1))
    block_exp = jnp.minimum(jnp.searchsorted(pend, jnp.arange(n_blocks) * tb, side='right'), E - 1).astype(jnp.int32)
    n_used = (pend[-1] // tb).astype(jnp.int32).reshape(1)
    return slot_of, slot_tok.reshape(n_blocks, 1, tb), block_exp, n_used


def _slab_copy(src_hbm, row, dst, r, sem):
    src = src_hbm.at[pl.ds(pl.multiple_of(row * ROW_SLAB, ROW_SLAB), ROW_SLAB)]
    return pltpu.make_async_copy(src, dst.at[pl.ds(pl.multiple_of(r * ROW_SLAB, ROW_SLAB), ROW_SLAB)], sem)


def _rows_from_slabs(ref, n):
    return jnp.concatenate([ref[pl.ds(s, n, stride=ROW_SLAB), :] for s in range(ROW_SLAB)], axis=1)


def _expert_kernel(be_ref, nb_ref, tok_ref, tok_next_ref, h_hbm, wg_ref, wu_ref, wd_ref, o_ref,
                   xbuf, sem, wg_s, wu_s, wd_s):
    i = pl.program_id(0)
    nb = nb_ref[0]
    live = i < nb
    tb = xbuf.shape[1] // ROW_SLAB

    def fetch(idx_ref, slot):
        def body(r, carry):
            _slab_copy(h_hbm, idx_ref[0, 0, r], xbuf.at[slot], r, sem.at[slot]).start()
            return carry
        lax.fori_loop(0, tb, body, 0, unroll=8)

    @pl.when((i == 0) & live)
    def _():
        fetch(tok_ref, 0)

    @pl.when(i + 1 < nb)
    def _():
        fetch(tok_next_ref, (i + 1) % 2)

    @pl.when(live & ((i == 0) | (be_ref[i] != be_ref[jnp.maximum(i - 1, 0)])))
    def _():
        wg_s[...] = wg_ref[0, 0].astype(BF16)
        wu_s[...] = wu_ref[0, 0].astype(BF16)
        wd_s[...] = wd_ref[0, 0].astype(BF16)

    @pl.when(live)
    def _():
        slot = i % 2
        pltpu.make_async_copy(h_hbm.at[pl.ds(0, tb * ROW_SLAB)], xbuf.at[slot], sem.at[slot]).wait()
        xb = _rows_from_slabs(xbuf.at[slot], tb).astype(BF16)
        a = jax.nn.silu(jnp.dot(xb, wg_s[...], preferred_element_type=F32))
        a = a * jnp.dot(xb, wu_s[...], preferred_element_type=F32)
        y = jnp.dot(a.astype(BF16), wd_s[...], preferred_element_type=F32)
        for s in range(ROW_SLAB):
            o_ref[pl.ds(s, tb, stride=ROW_SLAB), :] = y[:, s * LANES:(s + 1) * LANES]

    @pl.when(i >= nb)
    def _():
        o_ref[...] = jnp.zeros_like(o_ref)


def routed_experts(block_exp, n_used, slot_tok, h_slabs, wg, wu, wd, layer):
    n_blocks, _, tb = slot_tok.shape
    D = ROW_SLAB * LANES
    H = wg.shape[-1]

    def wsel(i, be, nb):
        return (layer, be[jnp.minimum(i, nb[0] - 1)], 0, 0)

    def toks(ahead):
        return pl.BlockSpec((1, 1, tb), lambda i, be, nb: (jnp.minimum(i + ahead, n_blocks - 1), 0, 0),
                            memory_space=pltpu.SMEM)

    grid_spec = pltpu.PrefetchScalarGridSpec(
        num_scalar_prefetch=2,
        grid=(n_blocks,),
        in_specs=[toks(0), toks(1),
                  pl.BlockSpec(memory_space=pl.ANY),
                  pl.BlockSpec((1, 1, D, H), wsel),
                  pl.BlockSpec((1, 1, D, H), wsel),
                  pl.BlockSpec((1, 1, H, D), wsel)],
        out_specs=pl.BlockSpec((tb * ROW_SLAB, LANES), lambda i, be, nb: (i, 0)),
        scratch_shapes=[pltpu.VMEM((2, tb * ROW_SLAB, LANES), F32), pltpu.SemaphoreType.DMA((2,)),
                        pltpu.VMEM((D, H), BF16), pltpu.VMEM((D, H), BF16), pltpu.VMEM((H, D), BF16)],
    )
    return pl.pallas_call(
        _expert_kernel,
        grid_spec=grid_spec,
        out_shape=jax.ShapeDtypeStruct((n_blocks * tb * ROW_SLAB, LANES), F32),
        compiler_params=_cparams(("arbitrary",)),
        name="routed_experts",
    )(block_exp, n_used, slot_tok, slot_tok, h_slabs, wg, wu, wd)


def _ffn_out_kernel(slot_ref, slot_next_ref, ys_hbm, h_ref, wt_ref, x_ref, g2_ref, sg_ref, su_ref, sd_ref, o_ref,
                    ybuf, sem):
    i = pl.program_id(0)
    tm = h_ref.shape[0]

    def fetch(idx_ref, slot):
        for k in range(TOP_K):
            def body(r, carry):
                _slab_copy(ys_hbm, idx_ref[0, k, r], ybuf.at[slot, k], r, sem.at[slot]).start()
                return carry
            lax.fori_loop(0, tm, body, 0, unroll=8)

    @pl.when(i == 0)
    def _():
        fetch(slot_ref, 0)

    @pl.when(i + 1 < pl.num_programs(0))
    def _():
        fetch(slot_next_ref, (i + 1) % 2)

    h = h_ref[...].astype(BF16)
    a = jax.nn.silu(jnp.dot(h, sg_ref[...], preferred_element_type=F32))
    a = a * jnp.dot(h, su_ref[...], preferred_element_type=F32)
    acc = jnp.dot(a.astype(BF16), sd_ref[...], preferred_element_type=F32)
    wt = wt_ref[...]
    slot = i % 2
    for k in range(TOP_K):
        pltpu.make_async_copy(ys_hbm.at[pl.ds(0, tm * ROW_SLAB)], ybuf.at[slot, k], sem.at[slot]).wait()
    for k in range(TOP_K):
        acc = acc + _rows_from_slabs(ybuf.at[slot, k], tm) * wt[:, k:k + 1]
    o_ref[...] = x_ref[...] + g2_ref[0] * acc


def ffn_out(h, ys, slot_of, wts, x, g2, sg, su, sd, n_ctx, S):
    T, D = x.shape
    H = sg.shape[-1]
    tm = FFN_TILE
    n_tiles = T // tm
    seg = functools.partial(_seg_of_tile, ctx_tiles=n_ctx // tm, tiles_per_batch=S // tm)
    row = pl.BlockSpec((tm, D), lambda i: (i, 0))
    slots = slot_of.reshape(TOP_K, n_tiles, tm).transpose(1, 0, 2)

    def tile_slots(ahead):
        return pl.BlockSpec((1, TOP_K, tm), lambda i: (jnp.minimum(i + ahead, n_tiles - 1), 0, 0),
                            memory_space=pltpu.SMEM)
    return pl.pallas_call(
        _ffn_out_kernel,
        grid=(n_tiles,),
        in_specs=[tile_slots(0), tile_slots(1),
                  pl.BlockSpec(memory_space=pl.ANY),
                  row,
                  pl.BlockSpec((tm, TOP_K), lambda i: (i, 0)),
                  row,
                  pl.BlockSpec((1, 1, D), lambda i: (seg(i), 0, 0)),
                  pl.BlockSpec((D, H), lambda i: (0, 0)),
                  pl.BlockSpec((D, H), lambda i: (0, 0)),
                  pl.BlockSpec((H, D), lambda i: (0, 0))],
        out_specs=row,
        out_shape=jax.ShapeDtypeStruct((T, D), F32),
        scratch_shapes=[pltpu.VMEM((2, TOP_K, tm * ROW_SLAB, LANES), F32), pltpu.SemaphoreType.DMA((2,))],
        compiler_params=_cparams(("arbitrary",)),
        name="ffn_out",
    )(slots, slots, ys, h, wts, x, g2, sg, su, sd)


def centred_dwconv(x, w, b):
    K, L = w.shape[0], x.shape[1]
    lo = (K - 1) // 2
    xp = jnp.pad(x, ((0, 0), (lo, K - 1 - lo), (0, 0)))
    y = b
    for j in range(K):
        y = y + xp[:, j:j + L] * w[j]
    return y


def hyena_filters(L, w1, b1, w2, b2, w3, freq, decay):
    n = jnp.arange(L, dtype=F32)
    t = n / max(L - 1, 1)
    bands = (HY_EMB - 1) // 2
    f = jnp.linspace(1e-4, bands - 1, bands, dtype=F32)
    ang = (2.0 * math.pi / L) * n[:, None] * f
    z = jnp.concatenate([t[:, None], jnp.cos(ang), -jnp.sin(ang)], axis=-1)
    a = jnp.sin(freq * (z @ w1 + b1))
    a = jnp.sin(freq * (a @ w2 + b2))
    h = ((a @ w3) * jnp.exp(-t[:, None] * jnp.abs(decay))).astype(F32)
    h = h.reshape(L, HY_ORDER, 2, HY_WIDTH)
    l1 = jnp.sum(jnp.abs(h[:, :, 0]), axis=0) + jnp.sum(jnp.abs(h[1:, :, 1]), axis=0)
    return h / l1[None, :, None]


def long_conv(u, hf, hb, d):
    B, L, W = u.shape
    taps = jnp.concatenate([hf, jnp.zeros((1, W), hf.dtype), hb[:0:-1]], axis=0)
    y = jnp.fft.irfft(jnp.fft.rfft(u, n=2 * L, axis=1) * jnp.fft.rfft(taps, axis=0)[None], n=2 * L, axis=1)[:, :L]
    return y + u * d


def _dft_tables(L):
    n = 2 * L
    n1 = n // HY_N2
    assert n == n1 * HY_N2 and n1 % HY_K1_TILE == 0
    a1 = 2.0 * np.pi * np.outer(np.arange(n1), np.arange(n1)) / n1
    a2 = 2.0 * np.pi * np.outer(np.arange(HY_N2), np.arange(HY_N2)) / HY_N2
    at = 2.0 * np.pi * np.outer(np.arange(n1), np.arange(HY_N2)) / n

    def pair(a, dt):
        return jnp.asarray(np.cos(a), dt), jnp.asarray(-np.sin(a), dt)
    f1r, f1i = pair(a1, BF16)
    f2r, f2i = pair(a2, BF16)
    twr, twi = pair(at, F32)
    return f1r, f1i, f2r, f2i, twr[:, :, None], twi[:, :, None]


def _hy_conv_kernel(x_ref, prev_ref, next_ref, cw_ref, cb_ref, o0_ref, o1_ref, o2_ref):
    j = pl.program_id(1)
    has_prev = jnp.where(j > 0, 1.0, 0.0)
    has_next = jnp.where(j < pl.num_programs(1) - 1, 1.0, 0.0)
    x = x_ref[...]
    tc, w = x.shape
    row = lax.broadcasted_iota(jnp.int32, (tc, w), 0)
    xm1 = jnp.where(row == 0, prev_ref[SUBLANES - 1:SUBLANES, :] * has_prev, pltpu.roll(x, 1, 0))
    xp1 = jnp.where(row == tc - 1, next_ref[0:1, :] * has_next, pltpu.roll(x, tc - 1, 0))
    u = cb_ref[...] + xm1 * cw_ref[0:1, :] + x * cw_ref[1:2, :] + xp1 * cw_ref[2:3, :]
    for o, o_ref in enumerate((o0_ref, o1_ref, o2_ref)):
        o_ref[...] = u[:, o * HY_WIDTH:(o + 1) * HY_WIDTH]


def hy_short_conv(p, conv_w, conv_b, B, S):
    T = p.shape[0]
    tc, w = ROW_TILE, (HY_ORDER + 1) * HY_WIDTH
    assert conv_w.shape[0] == 3 and HY_ORDER == 2 and HY_OFF % w == 0
    col, n_blk, first = HY_OFF // w, S // tc, (T - B * S) // tc
    halo = tc // SUBLANES

    def blk(b, j):
        return first + b * n_blk + j
    out = pl.BlockSpec((tc, HY_WIDTH), lambda b, j: (b * n_blk + j, 0))
    return pl.pallas_call(
        _hy_conv_kernel,
        grid=(B, n_blk),
        in_specs=[pl.BlockSpec((tc, w), lambda b, j: (blk(b, j), col)),
                  pl.BlockSpec((SUBLANES, w), lambda b, j: (blk(b, j) * halo - 1, col)),
                  pl.BlockSpec((SUBLANES, w), lambda b, j: (jnp.minimum((blk(b, j) + 1) * halo, T // SUBLANES - 1), col)),
                  pl.BlockSpec((3, w), lambda b, j: (0, 0)), pl.BlockSpec((1, w), lambda b, j: (0, 0))],
        out_specs=[out, out, out],
        out_shape=[jax.ShapeDtypeStruct((B * S, HY_WIDTH), F32)] * 3,
        compiler_params=_cparams(("arbitrary", "arbitrary")),
        name="hy_short_conv",
    )(p, p, p, conv_w, conv_b[None, :])


def _hy_rows_kernel(x_ref, fr_ref, fi_ref, ar_ref, ai_ref):
    xb = x_ref[0].astype(BF16)
    ar_ref[0] = jnp.dot(fr_ref[...], xb, preferred_element_type=F32)
    ai_ref[0] = jnp.dot(fi_ref[...], xb, preferred_element_type=F32)


def hy_dft_rows(x, fr, fi):
    G, K, cols = x.shape
    n1 = fr.shape[0]
    ct = min(HY_COL_TILE, cols)
    out = pl.BlockSpec((1, n1, ct), lambda g, j: (g, 0, j))
    return pl.pallas_call(
        _hy_rows_kernel,
        grid=(G, cols // ct),
        in_specs=[pl.BlockSpec((1, K, ct), lambda g, j: (g, 0, j)),
                  pl.BlockSpec((n1, K), lambda g, j: (0, 0)), pl.BlockSpec((n1, K), lambda g, j: (0, 0))],
        out_specs=[out, out],
        out_shape=[jax.ShapeDtypeStruct((G, n1, cols), F32)] * 2,
        compiler_params=_cparams(("arbitrary", "arbitrary")),
        name="hy_dft_rows",
    )(x, fr[:, :K], fi[:, :K])


def _hy_mid_kernel(ar_ref, ai_ref, twr_ref, twi_ref, f2r_ref, f2i_ref, *rest, conv):
    if conv:
        hr_ref, hi_ref, gr_ref, gi_ref = rest
    else:
        gr_ref, gi_ref = rest
    f2r, f2i = f2r_ref[...], f2i_ref[...]

    def mm(a, b):
        return jnp.dot(a, b, preferred_element_type=F32)
    for j in range(ar_ref.shape[1]):
        ar, ai = ar_ref[0, j], ai_ref[0, j]
        twr, twi = twr_ref[j], twi_ref[j]
        pr = (ar * twr - ai * twi).astype(BF16)
        pi = (ar * twi + ai * twr).astype(BF16)
        xr = mm(f2r, pr) - mm(f2i, pi)
        xi = mm(f2r, pi) + mm(f2i, pr)
        if not conv:
            gr_ref[0, j] = xr
            gi_ref[0, j] = xi
            continue
        hr, hi = hr_ref[0, j], hi_ref[0, j]
        yr = (xr * hr - xi * hi).astype(BF16)
        yi = (xr * hi + xi * hr).astype(BF16)
        gr = mm(f2r, yr) + mm(f2i, yi)
        gi = mm(f2r, yi) - mm(f2i, yr)
        gr_ref[0, j] = gr * twr + gi * twi
        gi_ref[0, j] = gi * twr - gr * twi


def hy_mid(ar, ai, tables, spectrum=None, order=0):
    G, n1 = ar.shape[0], ar.shape[1]
    C = ar.shape[-1]
    _, _, f2r, f2i, twr, twi = tables
    kt = HY_K1_TILE
    blk = pl.BlockSpec((1, kt, HY_N2, C), lambda g, i: (g, i, 0, 0))
    tw = pl.BlockSpec((kt, HY_N2, 1), lambda g, i: (i, 0, 0))
    mat = pl.BlockSpec((HY_N2, HY_N2), lambda g, i: (0, 0))
    ins, specs = [ar, ai, twr, twi, f2r, f2i], [blk, blk, tw, tw, mat, mat]
    if spectrum is not None:
        hspec = pl.BlockSpec((1, kt, HY_N2, C), lambda g, i: (order, i, 0, 0))
        ins, specs = ins + list(spectrum), specs + [hspec, hspec]
    return pl.pallas_call(
        functools.partial(_hy_mid_kernel, conv=spectrum is not None),
        grid=(G, n1 // kt),
        in_specs=specs,
        out_specs=[blk, blk],
        out_shape=[jax.ShapeDtypeStruct(ar.shape, F32)] * 2,
        compiler_params=_cparams(("arbitrary", "arbitrary")),
        name="hy_mid_conv" if spectrum is not None else "hy_mid_filter",
    )(*ins)


def _hy_out_kernel(gr_ref, gi_ref, fr_ref, fi_ref, z_ref, xm_ref, d_ref, o_ref, *, scale):
    y = jnp.dot(fr_ref[...], gr_ref[0].astype(BF16), preferred_element_type=F32)
    y = y + jnp.dot(fi_ref[...], gi_ref[0].astype(BF16), preferred_element_type=F32)
    o_ref[0] = xm_ref[0] * (y * scale + z_ref[0] * d_ref[...])


def hy_out(gr, gi, fr, fi, z, xmul, d_cols):
    G, n1, cols = gr.shape
    rows = z.shape[1]
    ct = min(HY_COL_TILE // 2, cols)
    spec = pl.BlockSpec((1, n1, ct), lambda g, j: (g, 0, j))
    half = pl.BlockSpec((1, rows, ct), lambda g, j: (g, 0, j))
    mat = pl.BlockSpec((rows, n1), lambda g, j: (0, 0))
    return pl.pallas_call(
        functools.partial(_hy_out_kernel, scale=1.0 / (n1 * HY_N2)),
        grid=(G, cols // ct),
        in_specs=[spec, spec, mat, mat, half, half, pl.BlockSpec((1, ct), lambda g, j: (0, j))],
        out_specs=half,
        out_shape=jax.ShapeDtypeStruct(z.shape, F32),
        compiler_params=_cparams(("arbitrary", "arbitrary")),
        name="hy_out",
    )(gr, gi, fr[:rows], fi[:rows], z, xmul, d_cols)


def hyena_taps(L, w1, b1, w2, b2, w3, freq, decay):
    n = jnp.arange(L, dtype=F32)
    t = n / max(L - 1, 1)
    bands = (HY_EMB - 1) // 2
    f = jnp.linspace(1e-4, bands - 1, bands, dtype=F32)
    ang = (2.0 * math.pi / L) * n[:, None] * f
    z = jnp.concatenate([t[:, None], jnp.cos(ang), -jnp.sin(ang)], axis=-1)
    a = jnp.sin(freq * (z @ w1 + b1))
    a = jnp.sin(freq * (a @ w2 + b2))
    w3r = w3.reshape(-1, HY_ORDER, 2, HY_WIDTH)
    dec = jnp.abs(decay).reshape(HY_ORDER, 2, HY_WIDTH)
    a_circ = jnp.concatenate([a, jnp.zeros((1, a.shape[1]), F32), a[:0:-1]], axis=0)
    t_circ = jnp.concatenate([t, jnp.zeros((1,), F32), t[:0:-1]])
    forward = (jnp.arange(2 * L) < L)[None, :, None]

    def side(d):
        return jnp.einsum('lf,foc->olc', a_circ, w3r[:, :, d]) * jnp.exp(-t_circ[None, :, None] * dec[:, d][:, None, :])
    taps = jnp.where(forward, side(0), side(1))
    return taps / jnp.sum(jnp.abs(taps), axis=1, keepdims=True)


def hyena_latents(p, conv_w, conv_b, taps, dskip, B, S):
    C = HY_WIDTH
    tables = _dft_tables(S)
    f1r, f1i = tables[0], tables[1]
    n1 = f1r.shape[0]
    cols = HY_N2 * C
    taps = taps.reshape(HY_ORDER, n1, cols)
    tr, ti = hy_dft_rows(taps, f1r, f1i)
    spectrum = hy_mid(tr.reshape(HY_ORDER, n1, HY_N2, C), ti.reshape(HY_ORDER, n1, HY_N2, C), tables)

    parts = hy_short_conv(p, conv_w, conv_b, B, S)
    rows = S // HY_N2
    z = parts[0].reshape(B, rows, cols)
    for o in range(HY_ORDER):
        ar, ai = hy_dft_rows(z, f1r, f1i)
        gr, gi = hy_mid(ar.reshape(B, n1, HY_N2, C), ai.reshape(B, n1, HY_N2, C), tables, spectrum, o)
        z = hy_out(gr.reshape(B, n1, cols), gi.reshape(B, n1, cols), f1r, f1i, z,
                   parts[o + 1].reshape(B, rows, cols), jnp.tile(dskip[o], HY_N2)[None, :])
    return z.reshape(B * S, C)


def hyena_context(pc, conv_w, conv_b, filt, dskip):
    L = pc.shape[1]
    u = centred_dwconv(pc, conv_w, conv_b)
    parts = jnp.split(u, HY_ORDER + 1, axis=-1)
    z = parts[0]
    for o in range(HY_ORDER):
        z = parts[o + 1] * long_conv(z, filt[:, o, 0], filt[:, o, 1], dskip[o])
    return z


def kernel(x, c, ctx, c_ctx, norm1_g, norm2_g, ada_w, ada_b, w_in, na_qnorm_g, na_knorm_g, na_rpb,
           ml_conv_w, ml_conv_b, ml_gate_b, ml_out_g, lru_conv_w, lru_conv_b, lru_wa, lru_ba, lru_wx, lru_bx,
           lru_lambda, hy_conv_w, hy_conv_b, hy_w1, hy_b1, hy_w2, hy_b2, hy_w3, hy_freq, hy_decay, hy_dskip,
           br_w, br_gate_b, w_o, router_w, router_b, exp_w_gate, exp_w_up, exp_w_down,
           sh_w_gate, sh_w_up, sh_w_down):
    B, S, D = x.shape
    assert ctx.shape[1] == CTX_LEN and D == D_MODEL
    n_ctx = B * CTX_LEN
    T = n_ctx + B * S
    assert n_ctx % ROW_TILE == 0 and S % ROW_TILE == 0 and T % ROUTE_TILE == 0
    tiles = dict(ctx_tiles=n_ctx // ROW_TILE, tiles_per_batch=S // ROW_TILE)
    pos = jnp.arange(S)
    rows, cols = pos // GRID_W, pos % GRID_W

    xa = jnp.concatenate([ctx.reshape(n_ctx, D), x.reshape(B * S, D)], axis=0)
    for l in range(DEPTH):
        need_ctx = l < DEPTH - 1
        cond = jnp.concatenate([c_ctx[None, :], c], axis=0)
        mod = (jax.nn.silu(cond) @ ada_w[l] + ada_b[l]).reshape(1 + B, 6, D)
        sh1sc1, g1, sh2sc2, g2 = mod[:, 0:2], mod[:, 2:3], mod[:, 3:5], mod[:, 5:6]

        w_l = w_in[l]
        w_main = jnp.concatenate([w_l[:, :IF_START], w_l[:, IF_START + IF_WIDTH:]], axis=1).astype(BF16)
        w_if = jnp.pad(w_l[:, IF_START:IF_START + IF_WIDTH], ((0, 0), (0, LANES - IF_WIDTH)))
        p = norm_mod_project(xa, norm1_g[l][None, :], sh1sc1, w_main, **tiles)
        p_if = norm_mod_narrow(xa, norm1_g[l][None, :], sh1sc1, w_if, False, **tiles)

        ya = neighbourhood_attention(p, na_rpb[l], na_qnorm_g[l], na_knorm_g[l], B, S)
        yb = mlstm(p, p_if, ml_conv_w[l], ml_conv_b[l], ml_gate_b[l], ml_out_g[l], B, S)
        yc = rglru(p, lru_conv_w[l], lru_conv_b[l], lru_wa[l], lru_ba[l], lru_wx[l], lru_bx[l], lru_lambda[l], B, S)
        hy_par = (hy_w1[l], hy_b1[l], hy_w2[l], hy_b2[l], hy_w3[l], hy_freq[l], hy_decay[l])
        yd_lat = hyena_latents(p, hy_conv_w[l], hy_conv_b[l], hyena_taps(S, *hy_par), hy_dskip[l], B, S)
        if need_ctx:
            pc = p[:n_ctx, HY_OFF:GATE_OFF].reshape(B, CTX_LEN, -1)
            yd_ctx = hyena_context(pc, hy_conv_w[l], hy_conv_b[l], hyena_filters(CTX_LEN, *hy_par), hy_dskip[l])
            yd_ctx = yd_ctx.reshape(n_ctx, HY_WIDTH)
        else:
            yd_ctx = jnp.zeros((n_ctx, HY_WIDTH), F32)
        yd = jnp.concatenate([yd_ctx, yd_lat], axis=0)
        xa = merge_branches([ya, yb, yc, yd], p, br_gate_b[l], br_w[l].astype(BF16), w_o[l].astype(BF16), xa, g1,
                            **tiles)

        logits, h2 = norm_mod_narrow(xa, norm2_g[l][None, :], sh2sc2, router_w[l], True, **tiles)
        idx, wts, rank, counts = route_tokens(logits, router_b[l])
        slot_of, slot_tok, block_exp, n_used = slot_layout(idx, rank, counts, T)
        ys = routed_experts(block_exp, n_used, slot_tok, h2.reshape(T * ROW_SLAB, LANES), exp_w_gate, exp_w_up,
                            exp_w_down, l)
        xa = ffn_out(h2, ys, slot_of, wts.T, xa, g2, sh_w_gate[l].astype(BF16), sh_w_up[l].astype(BF16),
                     sh_w_down[l].astype(BF16), n_ctx, S)
    return xa[n_ctx:].reshape(B, S, D)
```
